```python
import jax, jax.numpy as jnp
from jax import lax
import numpy as np

D_MODEL = 1024
BATCH = 4
SEQ = 8192
DEPTH = 1

ATT_HEADS = 16
ATT_KV_HEADS = 4
ATT_HEAD_DIM = 64
ATT_WINDOW = 128
ATT_BLOCK = 128
ROPE_THETA = 10000.0
MLSTM_HEADS = 4
MLSTM_V_DIM = D_MODEL // MLSTM_HEADS
MLSTM_QK_DIM = MLSTM_V_DIM // 2
MLSTM_CHUNK = 128
MLSTM_CONV = 4
D_FF = 2816
RMS_EPS = 1e-5

ATT_Q_W = ATT_HEADS * ATT_HEAD_DIM
ATT_KV_W = ATT_KV_HEADS * ATT_HEAD_DIM
ML_QK_W = MLSTM_HEADS * MLSTM_QK_DIM
ML_V_W = MLSTM_HEADS * MLSTM_V_DIM
SPLITS = (ATT_Q_W, ATT_KV_W, ATT_KV_W, ML_QK_W, ML_QK_W, ML_V_W, ML_V_W,
          MLSTM_HEADS, MLSTM_HEADS, D_MODEL, D_MODEL)
IN_PROJ_W = sum(SPLITS)

kernel_name = "hybrid_swa_sink_mlstm_macaron"


def rms_norm(x, g):
    xf = x.astype(jnp.float32)
    y = xf * lax.rsqrt(jnp.mean(xf * xf, axis=-1, keepdims=True) + RMS_EPS)
    return (y * g.astype(jnp.float32)).astype(x.dtype)


def swiglu(h, w_gate, w_up, w_down):
    return (jax.nn.silu(h @ w_gate) * (h @ w_up)) @ w_down


def rope(x):
    half = x.shape[-1] // 2
    pos = jnp.arange(x.shape[1], dtype=jnp.float32)
    inv_freq = ROPE_THETA ** (-jnp.arange(half, dtype=jnp.float32) / half)
    ang = pos[:, None] * inv_freq[None, :]
    cos = jnp.cos(ang)[None, :, None, :]
    sin = jnp.sin(ang)[None, :, None, :]
    xf = x.astype(jnp.float32)
    x1, x2 = xf[..., :half], xf[..., half:]
    return jnp.concatenate([x1 * cos - x2 * sin, x2 * cos + x1 * sin], axis=-1).astype(x.dtype)


def sliding_window_attention(q, k, v, sinks):
    B, S = q.shape[0], q.shape[1]
    nb = S // ATT_BLOCK
    G = ATT_HEADS // ATT_KV_HEADS
    qb = q.reshape(B, nb, ATT_BLOCK, ATT_KV_HEADS, G, ATT_HEAD_DIM)

    def with_prev(t):
        tb = t.reshape(B, nb, ATT_BLOCK, ATT_KV_HEADS, ATT_HEAD_DIM)
        prev = jnp.pad(tb[:, :-1], ((0, 0), (1, 0), (0, 0), (0, 0), (0, 0)))
        return jnp.concatenate([prev, tb], axis=2)

    kb, vb = with_prev(k), with_prev(v)
    scores = jnp.einsum('bnqhgd,bnkhd->bnhgqk', qb, kb,
                        preferred_element_type=jnp.float32) * (ATT_HEAD_DIM ** -0.5)
    qi = jnp.arange(ATT_BLOCK)[:, None]
    kj = jnp.arange(2 * ATT_BLOCK)[None, :]
    rel = qi + ATT_BLOCK - kj
    band = (rel >= 0) & (rel < ATT_WINDOW)
    key_pos = jnp.arange(nb)[:, None, None] * ATT_BLOCK + kj[None] - ATT_BLOCK
    mask = band[None] & (key_pos >= 0)
    scores = jnp.where(mask[None, :, None, None], scores, -jnp.inf)
    sink = sinks.astype(jnp.float32).reshape(ATT_KV_HEADS, G)[None, None, :, :, None, None]
    m = jnp.maximum(scores.max(axis=-1, keepdims=True), sink)
    p = jnp.exp(scores - m)
    denom = p.sum(axis=-1, keepdims=True) + jnp.exp(sink - m)
    p = (p / denom).astype(v.dtype)
    out = jnp.einsum('bnhgqk,bnkhd->bnqhgd', p, vb)
    return out.reshape(B, S, ATT_Q_W)


def causal_depthwise_conv(x, w, b):
    K, C = w.shape
    y = lax.conv_general_dilated(x, w[:, None, :].astype(x.dtype), window_strides=(1,),
                                 padding=[(K - 1, 0)],
                                 dimension_numbers=('NWC', 'WIO', 'NWC'),
                                 feature_group_count=C)
    return y + b


def mlstm_chunkwise(q, k, v, i_pre, f_pre):
    B, S, H, Dk = q.shape
    Dv = v.shape[-1]
    L = MLSTM_CHUNK
    nc = S // L
    f32 = jnp.float32

    def chunks(t):
        return jnp.moveaxis(t.astype(f32).reshape(B, nc, L, H, -1), 3, 1)

    qc = chunks(q)
    kc = chunks(k) * (Dk ** -0.5)
    vc = chunks(v)
    ig = jnp.moveaxis(i_pre.astype(f32).reshape(B, nc, L, H), 3, 1)
    logf = jax.nn.log_sigmoid(jnp.moveaxis(f_pre.astype(f32).reshape(B, nc, L, H), 3, 1))
    b = jnp.cumsum(logf, axis=-1)
    b_end = b[..., -1]

    w_end = b_end[..., None] - b + ig
    m_loc = w_end.max(axis=-1)
    a = jnp.exp(w_end - m_loc[..., None])
    dC = jnp.einsum('bhclv,bhcld->bhcvd', a[..., None] * vc, kc)
    dn = jnp.einsum('bhcl,bhcld->bhcd', a, kc)

    def step(carry, inp):
        C, n, m = carry
        dC_c, dn_c, m_loc_c, b_end_c = inp
        m_new = jnp.maximum(b_end_c + m, m_loc_c)
        s_old = jnp.exp(b_end_c + m - m_new)
        s_new = jnp.exp(m_loc_c - m_new)
        C_new = s_old[..., None, None] * C + s_new[..., None, None] * dC_c
        n_new = s_old[..., None] * n + s_new[..., None] * dn_c
        return (C_new, n_new, m_new), (C, n, m)

    init = (jnp.zeros((B, H, Dv, Dk), f32), jnp.zeros((B, H, Dk), f32), jnp.zeros((B, H), f32))
    xs = (jnp.moveaxis(dC, 2, 0), jnp.moveaxis(dn, 2, 0),
          jnp.moveaxis(m_loc, 2, 0), jnp.moveaxis(b_end, 2, 0))
    _, (C_prev, n_prev, m_prev) = lax.scan(step, init, xs)
    C_prev = jnp.moveaxis(C_prev, 0, 2)
    n_prev = jnp.moveaxis(n_prev, 0, 2)
    m_prev = jnp.moveaxis(m_prev, 0, 2)

    causal = jnp.tril(jnp.ones((L, L), dtype=bool))
    log_d = jnp.where(causal, b[..., :, None] - b[..., None, :] + ig[..., None, :], -jnp.inf)
    log_inter = b + m_prev[..., None]
    m_t = jnp.maximum(log_d.max(axis=-1), log_inter)
    s = jnp.einsum('bhcld,bhcsd->bhcls', qc, kc) * jnp.exp(log_d - m_t[..., None])
    inter = jnp.exp(log_inter - m_t)
    num = (jnp.einsum('bhcls,bhcsv->bhclv', s, vc)
           + inter[..., None] * jnp.einsum('bhcld,bhcvd->bhclv', qc, C_prev))
    den = s.sum(axis=-1) + inter * jnp.einsum('bhcld,bhcd->bhcl', qc, n_prev)
    h = num / jnp.maximum(jnp.abs(den), jnp.exp(-m_t))[..., None]
    return jnp.moveaxis(h, 1, 3).reshape(B, S, H * Dv).astype(q.dtype)


def hybrid_mixer(h, w_in, b_i, b_f, attn_sinks, conv_w, conv_b, head_norm,
                 w_att, w_mlstm, w_out):
    B, S, _ = h.shape
    z = h @ w_in
    idx = np.cumsum(SPLITS)[:-1].tolist()
    q_a, k_a, v_a, q_m, k_m, v_m, o_m, i_m, f_m, g_a, g_m = jnp.split(z, idx, axis=-1)

    q_a = rope(q_a.reshape(B, S, ATT_HEADS, ATT_HEAD_DIM))
    k_a = rope(k_a.reshape(B, S, ATT_KV_HEADS, ATT_HEAD_DIM))
    v_a = v_a.reshape(B, S, ATT_KV_HEADS, ATT_HEAD_DIM)
    y_a = sliding_window_attention(q_a, k_a, v_a, attn_sinks) @ w_att

    qk = jax.nn.silu(causal_depthwise_conv(jnp.concatenate([q_m, k_m], axis=-1), conv_w, conv_b))
    q_m, k_m = qk[..., :ML_QK_W], qk[..., ML_QK_W:]
    h_m = mlstm_chunkwise(q_m.reshape(B, S, MLSTM_HEADS, MLSTM_QK_DIM),
                          k_m.reshape(B, S, MLSTM_HEADS, MLSTM_QK_DIM),
                          v_m.reshape(B, S, MLSTM_HEADS, MLSTM_V_DIM),
                          i_m + b_i, f_m + b_f)
    h_m = rms_norm(h_m.reshape(B, S, MLSTM_HEADS, MLSTM_V_DIM),
                   head_norm.reshape(MLSTM_HEADS, MLSTM_V_DIM)).reshape(B, S, ML_V_W)
    y_m = (jax.nn.sigmoid(o_m) * h_m) @ w_mlstm

    y = jax.nn.sigmoid(g_a) * y_a + jax.nn.sigmoid(g_m) * y_m
    return y @ w_out


def setup_inputs(seed: int = 0) -> dict:
    key = jax.random.key(seed)
    ks = jax.random.split(key, 24)
    f32 = jnp.float32

    def nrm(k, shape, scale):
        return jax.random.normal(k, shape, f32) * scale

    def gain(k, shape):
        return 1.0 + 0.02 * jax.random.normal(k, shape, f32)

    L = DEPTH
    return {
        "x": jax.random.normal(ks[0], (BATCH, SEQ, D_MODEL), f32),
        "ffn1_norm": gain(ks[1], (L, D_MODEL)),
        "ffn1_w_gate": nrm(ks[2], (L, D_MODEL, D_FF), D_MODEL ** -0.5),
        "ffn1_w_up": nrm(ks[3], (L, D_MODEL, D_FF), D_MODEL ** -0.5),
        "ffn1_w_down": nrm(ks[4], (L, D_FF, D_MODEL), D_FF ** -0.5),
        "mix_norm": gain(ks[5], (L, D_MODEL)),
        "w_in": nrm(ks[6], (L, D_MODEL, IN_PROJ_W), D_MODEL ** -0.5),
        "b_i": nrm(ks[7], (L, MLSTM_HEADS), 0.1),
        "b_f": jnp.linspace(3.0, 6.0, MLSTM_HEADS, dtype=f32)[None, :] + nrm(ks[8], (L, MLSTM_HEADS), 0.1),
        "attn_sinks": nrm(ks[9], (L, ATT_HEADS), 1.0),
        "conv_w": nrm(ks[10], (L, MLSTM_CONV, 2 * ML_QK_W), MLSTM_CONV ** -0.5),
        "conv_b": nrm(ks[11], (L, 2 * ML_QK_W), 0.01),
        "head_norm": gain(ks[12], (L, ML_V_W)),
        "w_att": nrm(ks[13], (L, ATT_Q_W, D_MODEL), ATT_Q_W ** -0.5),
        "w_mlstm": nrm(ks[14], (L, ML_V_W, D_MODEL), ML_V_W ** -0.5),
        "w_out": nrm(ks[15], (L, D_MODEL, D_MODEL), D_MODEL ** -0.5),
        "ffn2_norm": gain(ks[16], (L, D_MODEL)),
        "ffn2_w_gate": nrm(ks[17], (L, D_MODEL, D_FF), D_MODEL ** -0.5),
        "ffn2_w_up": nrm(ks[18], (L, D_MODEL, D_FF), D_MODEL ** -0.5),
        "ffn2_w_down": nrm(ks[19], (L, D_FF, D_MODEL), D_FF ** -0.5),
        "final_norm": gain(ks[20], (D_MODEL,)),
    }


def reference(x, ffn1_norm, ffn1_w_gate, ffn1_w_up, ffn1_w_down, mix_norm, w_in, b_i, b_f,
              attn_sinks, conv_w, conv_b, head_norm, w_att, w_mlstm, w_out,
              ffn2_norm, ffn2_w_gate, ffn2_w_up, ffn2_w_down, final_norm):
    for l in range(DEPTH):
        x = x + 0.5 * swiglu(rms_norm(x, ffn1_norm[l]), ffn1_w_gate[l], ffn1_w_up[l], ffn1_w_down[l])
        x = x + hybrid_mixer(rms_norm(x, mix_norm[l]), w_in[l], b_i[l], b_f[l], attn_sinks[l],
                             conv_w[l], conv_b[l], head_norm[l], w_att[l], w_mlstm[l], w_out[l])
        x = x + 0.5 * swiglu(rms_norm(x, ffn2_norm[l]), ffn2_w_gate[l], ffn2_w_up[l], ffn2_w_down[l])
    return rms_norm(x, final_norm)
```

```python
import functools

import jax
import jax.numpy as jnp
import numpy as np
from jax import lax
from jax.experimental import pallas as pl
from jax.experimental.pallas import tpu as pltpu

F32 = jnp.float32
BF16 = jnp.bfloat16

D_MODEL = 1024
ATT_HEADS = 16
ATT_KV_HEADS = 4
ATT_HEAD_DIM = 64
ATT_BLOCK = 128
ROPE_THETA = 10000.0
MLSTM_HEADS = 4
MLSTM_V_DIM = D_MODEL // MLSTM_HEADS
MLSTM_QK_DIM = MLSTM_V_DIM // 2
MLSTM_CHUNK = 128
MLSTM_CONV = 4
D_FF = 2816
RMS_EPS = 1e-5

ATT_Q_W = ATT_HEADS * ATT_HEAD_DIM
ATT_KV_W = ATT_KV_HEADS * ATT_HEAD_DIM
ML_QK_W = MLSTM_HEADS * MLSTM_QK_DIM
ML_V_W = MLSTM_HEADS * MLSTM_V_DIM

LANES = 128
CONV_TAIL_ROWS = 8
V7X_VMEM_BYTES = 64 * 1024 * 1024
VMEM_LIMIT_BYTES = 56 * 1024 * 1024

TOKEN_TILE = 512
FF_CHUNKS = ((0, 1024), (1024, 2048), (2048, D_FF))

_GROUP = ATT_HEADS // ATT_KV_HEADS
Q_HEAD_PERM = tuple(
    (2 * c + half) * _GROUP + r
    for c in range(ATT_KV_HEADS // 2) for r in range(_GROUP) for half in range(2))


def _resident(shape):
    return pl.BlockSpec(shape, lambda *_: (0,) * len(shape), pipeline_mode=pl.Buffered(1))


def _rms(x, g):
    ms = jnp.mean(x * x, axis=-1, keepdims=True)
    return x * lax.rsqrt(ms + RMS_EPS) * g


def _sigmoid(x):
    return 1.0 / (1.0 + jnp.exp(-x))


def _swiglu(h, wg_ref, wu_ref, wd_ref):
    acc = None
    for lo, hi in FF_CHUNKS:
        g = jnp.dot(h, wg_ref[:, lo:hi], preferred_element_type=F32)
        u = jnp.dot(h, wu_ref[:, lo:hi], preferred_element_type=F32)
        a = (g * _sigmoid(g) * u).astype(BF16)
        d = jnp.dot(a, wd_ref[lo:hi, :], preferred_element_type=F32)
        acc = d if acc is None else acc + d
    return acc


def _ffn1_kernel(x_ref, g_ref, wg_ref, wu_ref, wd_ref, o_ref):
    x = x_ref[...]
    h = _rms(x, g_ref[...]).astype(BF16)
    o_ref[...] = x + 0.5 * _swiglu(h, wg_ref, wu_ref, wd_ref)


def _ffn1(x, g, wg, wu, wd, tm):
    t = x.shape[0]
    tok = pl.BlockSpec((tm, D_MODEL), lambda i: (i, 0))
    return pl.pallas_call(
        _ffn1_kernel,
        grid=(t // tm,),
        in_specs=[tok, _resident((1, D_MODEL)), _resident((D_MODEL, D_FF)), _resident((D_MODEL, D_FF)),
                  _resident((D_FF, D_MODEL))],
        out_specs=tok,
        out_shape=jax.ShapeDtypeStruct((t, D_MODEL), F32),
        compiler_params=pltpu.CompilerParams(dimension_semantics=("arbitrary",),
                                             vmem_limit_bytes=VMEM_LIMIT_BYTES),
        name="ffn1",
    )(x, g, wg, wu, wd)


_C_QA = 0
_C_KA = _C_QA + ATT_Q_W
_C_VA = _C_KA + ATT_KV_W
_C_QKM = _C_VA + ATT_KV_W
_C_VM = _C_QKM + 2 * ML_QK_W
_C_OM = _C_VM + ML_V_W
_C_GA = _C_OM + ML_V_W
_C_GM = _C_GA + D_MODEL
_C_IF = _C_GM + D_MODEL
IN_W = _C_IF + 2 * LANES


def _rope(x, cos, sin_signed):
    lane = lax.broadcasted_iota(jnp.int32, x.shape, 1)
    half = ATT_HEAD_DIM // 2
    partner = jnp.where(lane % ATT_HEAD_DIM < half, pltpu.roll(x, LANES - half, axis=1),
                        pltpu.roll(x, half, axis=1))
    return x * cos + partner * sin_signed


def _inproj_kernel(x_ref, g_ref, cq_ref, sq_ref, ck_ref, sk_ref, w_ref,
                   qa_ref, kva_ref, qkm_ref, vm_ref, om_ref, ga_ref, gm_ref, gi_ref, gf_ref):
    h = _rms(x_ref[...], g_ref[...]).astype(BF16)

    def proj(lo, width):
        return jnp.dot(h, w_ref[:, lo:lo + width], preferred_element_type=F32)

    z = proj(_C_QA, ATT_Q_W)
    cq, sq = cq_ref[...], sq_ref[...]
    for j in range(ATT_Q_W // LANES):
        sl = slice(j * LANES, (j + 1) * LANES)
        qa_ref[:, sl] = _rope(z[:, sl], cq, sq).astype(BF16)

    z = proj(_C_KA, 2 * ATT_KV_W)
    ck, sk = ck_ref[...], sk_ref[...]
    for j in range(ATT_KV_W // LANES):
        sl = slice(j * LANES, (j + 1) * LANES)
        kva_ref[:, sl] = _rope(z[:, sl], ck, sk).astype(BF16)
    kva_ref[:, ATT_KV_W:] = z[:, ATT_KV_W:].astype(BF16)

    qkm_ref[...] = proj(_C_QKM, 2 * ML_QK_W).astype(BF16)
    vm_ref[...] = proj(_C_VM, ML_V_W).astype(BF16)
    om_ref[...] = proj(_C_OM, ML_V_W).astype(BF16)
    ga_ref[...] = proj(_C_GA, D_MODEL).astype(BF16)
    gm_ref[...] = proj(_C_GM, D_MODEL).astype(BF16)
    z = proj(_C_IF, 2 * LANES)
    gi_ref[...] = z[:, :LANES]
    gf_ref[...] = z[:, LANES:]


def _inproj(x1, g, cq, sq, ck, sk, w, tm, seq):
    t = x1.shape[0]
    tiles_per_seq = seq // tm

    def tok(width):
        return pl.BlockSpec((tm, width), lambda i: (i, 0))

    pos = pl.BlockSpec((tm, LANES), lambda i: (i % tiles_per_seq, 0))
    widths = (ATT_Q_W, 2 * ATT_KV_W, 2 * ML_QK_W, ML_V_W, ML_V_W, D_MODEL, D_MODEL)
    out_shape = [jax.ShapeDtypeStruct((t, w_), BF16) for w_ in widths]
    out_shape += [jax.ShapeDtypeStruct((t, LANES), F32)] * 2
    return pl.pallas_call(
        _inproj_kernel,
        grid=(t // tm,),
        in_specs=[tok(D_MODEL), _resident((1, D_MODEL)), pos, pos, pos, pos, _resident((D_MODEL, IN_W))],
        out_specs=[tok(w_) for w_ in widths] + [tok(LANES)] * 2,
        out_shape=out_shape,
        compiler_params=pltpu.CompilerParams(dimension_semantics=("arbitrary",),
                                             vmem_limit_bytes=VMEM_LIMIT_BYTES),
        name="in_proj",
    )(x1, g, cq, sq, ck, sk, w)


def _attn_kernel(q_ref, kvc_ref, kvp_ref, sink_ref, o_ref, p_scr):
    blk = ATT_BLOCK
    has_prev = pl.program_id(1) > 0
    row = lax.broadcasted_iota(jnp.int32, (blk, blk), 0)
    col = lax.broadcasted_iota(jnp.int32, (blk, blk), 1)
    in_cur = col <= row
    low = lax.broadcasted_iota(jnp.int32, (blk, LANES), 1) < ATT_HEAD_DIM
    neg_inf = jnp.float32(-jnp.inf)
    n_chunks = ATT_KV_W // LANES

    for c in range(n_chunks):
        ksl = slice(c * LANES, (c + 1) * LANES)
        kc = kvc_ref[:, ksl].astype(F32)
        kp = kvp_ref[:, ksl].astype(F32)
        zero = jnp.zeros_like(kc)
        kstack = jnp.concatenate(
            [jnp.where(low, kc, zero), jnp.where(low, kp, zero),
             jnp.where(low, zero, kc), jnp.where(low, zero, kp)], axis=0).astype(BF16)
        qstack = jnp.concatenate(
            [q_ref[:, (c * _GROUP + r) * LANES:(c * _GROUP + r + 1) * LANES] for r in range(_GROUP)], axis=0)
        s_all = lax.dot_general(qstack, kstack, (((1,), (1,)), ((), ())), preferred_element_type=F32)
        for r in range(_GROUP):
            for half in range(2):
                p_idx = (c * _GROUP + r) * 2 + half
                rs = slice(r * blk, (r + 1) * blk)
                s_cur = s_all[rs, (2 * half) * blk:(2 * half + 1) * blk]
                s_prev = s_all[rs, (2 * half + 1) * blk:(2 * half + 2) * blk]
                s = jnp.where(in_cur, s_cur, jnp.where(has_prev, s_prev, neg_inf))
                sink = sink_ref[p_idx:p_idx + 1, :]
                m = jnp.maximum(jnp.max(s, axis=-1, keepdims=True), sink[:, :1])
                p = jnp.exp(s - m)
                denom = jnp.sum(p, axis=-1, keepdims=True) + jnp.exp(sink[:, :1] - m)
                p = p * (1.0 / denom)
                zp = jnp.zeros_like(p)
                prs = slice(p_idx * blk, (p_idx + 1) * blk)
                p_scr[prs, :blk] = jnp.where(in_cur, p, zp).astype(BF16)
                p_scr[prs, blk:] = jnp.where(in_cur, zp, p).astype(BF16)

    v_all = jnp.concatenate([kvc_ref[:, ATT_KV_W:], kvp_ref[:, ATT_KV_W:]], axis=0)
    o_all = jnp.dot(p_scr[...], v_all, preferred_element_type=F32)
    for j in range(ATT_Q_W // LANES):
        c = j // _GROUP
        lo = o_all[(2 * j) * blk:(2 * j + 1) * blk, c * LANES:(c + 1) * LANES]
        hi = o_all[(2 * j + 1) * blk:(2 * j + 2) * blk, c * LANES:(c + 1) * LANES]
        o_ref[:, j * LANES:(j + 1) * LANES] = jnp.where(low, lo, hi).astype(BF16)


def _attn(qa, kva, sinks, batch, seq):
    t = qa.shape[0]
    nb = seq // ATT_BLOCK
    return pl.pallas_call(
        _attn_kernel,
        grid=(batch, nb),
        in_specs=[pl.BlockSpec((ATT_BLOCK, ATT_Q_W), lambda b, n: (b * nb + n, 0)),
                  pl.BlockSpec((ATT_BLOCK, 2 * ATT_KV_W), lambda b, n: (b * nb + n, 0)),
                  pl.BlockSpec((ATT_BLOCK, 2 * ATT_KV_W), lambda b, n: (b * nb + jnp.maximum(n - 1, 0), 0)),
                  _resident((ATT_HEADS, LANES))],
        out_specs=pl.BlockSpec((ATT_BLOCK, ATT_Q_W), lambda b, n: (b * nb + n, 0)),
        out_shape=jax.ShapeDtypeStruct((t, ATT_Q_W), BF16),
        scratch_shapes=[pltpu.VMEM((ATT_HEADS * ATT_BLOCK, 2 * ATT_BLOCK), BF16)],
        compiler_params=pltpu.CompilerParams(dimension_semantics=("arbitrary", "arbitrary"),
                                             vmem_limit_bytes=VMEM_LIMIT_BYTES),
        name="swa_attn",
    )(qa, kva, kva, sinks)


def _scan_rows(x, op, fill):
    n = x.shape[0]
    row = lax.broadcasted_iota(jnp.int32, x.shape, 0)
    shift = 1
    while shift < n:
        prev = jnp.where(row >= shift, pltpu.roll(x, shift, axis=0), fill)
        x = op(x, prev)
        shift *= 2
    return x


def _mlstm_kernel(qk_ref, v_ref, o_ref, gi_ref, gf_ref, cw_ref, cb_ref, bi_ref, bf_ref, hn_ref,
                  out_ref, ct_scr, n_scr, m_scr, conv_scr):
    L = MLSTM_CHUNK
    tail = CONV_TAIL_ROWS

    @pl.when(pl.program_id(1) == 0)
    def _():
        ct_scr[...] = jnp.zeros_like(ct_scr)
        n_scr[...] = jnp.zeros_like(n_scr)
        m_scr[...] = jnp.zeros_like(m_scr)
        conv_scr[:tail, :] = jnp.zeros((tail, conv_scr.shape[1]), F32)

    x = qk_ref[...].astype(F32)
    conv_scr[tail:, :] = x
    y = cb_ref[...] + cw_ref[MLSTM_CONV - 1:MLSTM_CONV, :] * x
    for j in range(1, MLSTM_CONV):
        y = y + cw_ref[MLSTM_CONV - 1 - j:MLSTM_CONV - j, :] * conv_scr[tail - j:tail - j + L, :]
    conv_scr[:tail, :] = x[L - tail:, :]
    qk = y * _sigmoid(y)

    ig = gi_ref[...] + bi_ref[...]
    fpre = gf_ref[...] + bf_ref[...]
    logf = jnp.minimum(fpre, 0.0) - jnp.log1p(jnp.exp(-jnp.abs(fpre)))
    b = _scan_rows(logf, jnp.add, 0.0)
    g = ig - b
    cm = _scan_rows(g, jnp.maximum, -jnp.inf)
    m_prev = m_scr[0:1, :]
    u = jnp.maximum(cm, m_prev)
    g_max = cm[L - 1:L, :]
    u_last = u[L - 1:L, :]
    a_t = jnp.exp(g - g_max)
    inter_t = jnp.exp(m_prev - u)
    emt_t = jnp.exp(-(b + u))
    s_old = jnp.exp(m_prev - u_last)
    s_new = jnp.exp(g_max - u_last)
    m_scr[0:1, :] = b[L - 1:L, :] + u_last
    g_rows = g.T

    row = lax.broadcasted_iota(jnp.int32, (L, L), 0)
    col = lax.broadcasted_iota(jnp.int32, (L, L), 1)
    causal = col <= row
    k_scale = MLSTM_QK_DIM ** -0.5

    for h in range(MLSTM_HEADS):
        qs = slice(h * MLSTM_QK_DIM, (h + 1) * MLSTM_QK_DIM)
        ks = slice(ML_QK_W + h * MLSTM_QK_DIM, ML_QK_W + (h + 1) * MLSTM_QK_DIM)
        vs = slice(h * MLSTM_V_DIM, (h + 1) * MLSTM_V_DIM)
        qh = qk[:, qs]
        kh = qk[:, ks] * k_scale
        qb = qh.astype(BF16)
        kb = kh.astype(BF16)
        vb = v_ref[:, vs]
        u_col = u[:, h:h + 1]
        inter_col = inter_t[:, h:h + 1]
        emt_col = emt_t[:, h:h + 1]
        a_col = a_t[:, h:h + 1]

        decay = jnp.where(causal, jnp.exp(g_rows[h:h + 1, :] - u_col), 0.0)
        s = lax.dot_general(qb, kb, (((1,), (1,)), ((), ())), preferred_element_type=F32) * decay
        ct = ct_scr[h]
        n_row = n_scr[h:h + 1, :]
        num = (jnp.dot(s.astype(BF16), vb, preferred_element_type=F32)
               + inter_col * jnp.dot(qb, ct.astype(BF16), preferred_element_type=F32))
        den = jnp.sum(s + inter_col * (qh * n_row), axis=-1, keepdims=True)
        hh = num * (1.0 / jnp.maximum(jnp.abs(den), emt_col))
        hn = _rms(hh, hn_ref[:, vs])
        out_ref[:, vs] = (_sigmoid(o_ref[:, vs].astype(F32)) * hn).astype(BF16)

        av = (a_col * vb.astype(F32)).astype(BF16)
        d_ct = lax.dot_general(kb, av, (((0,), (0,)), ((), ())), preferred_element_type=F32)
        d_n = jnp.sum(a_col * kh, axis=0, keepdims=True)
        so = s_old[:, h:h + 1]
        sn = s_new[:, h:h + 1]
        ct_scr[h] = so * ct + sn * d_ct
        n_scr[h:h + 1, :] = so * n_row + sn * d_n


def _mlstm(qkm, vm, om, gi, gf, cw, cb, bi, bf, hn, batch, seq):
    t = qkm.shape[0]
    nc = seq // MLSTM_CHUNK

    def tok(width):
        return pl.BlockSpec((MLSTM_CHUNK, width), lambda b, c: (b * nc + c, 0))

    return pl.pallas_call(
        _mlstm_kernel,
        grid=(batch, nc),
        in_specs=[tok(2 * ML_QK_W), tok(ML_V_W), tok(ML_V_W), tok(LANES), tok(LANES),
                  _resident((MLSTM_CONV, 2 * ML_QK_W)), _resident((1, 2 * ML_QK_W)),
                  _resident((1, LANES)), _resident((1, LANES)), _resident((1, ML_V_W))],
        out_specs=tok(ML_V_W),
        out_shape=jax.ShapeDtypeStruct((t, ML_V_W), BF16),
        scratch_shapes=[pltpu.VMEM((MLSTM_HEADS, MLSTM_QK_DIM, MLSTM_V_DIM), F32),
                        pltpu.VMEM((8, LANES), F32),
                        pltpu.VMEM((8, LANES), F32),
                        pltpu.VMEM((CONV_TAIL_ROWS + MLSTM_CHUNK, 2 * ML_QK_W), F32)],
        compiler_params=pltpu.CompilerParams(dimension_semantics=("arbitrary", "arbitrary"),
                                             vmem_limit_bytes=VMEM_LIMIT_BYTES),
        name="mlstm",
    )(qkm, vm, om, gi, gf, cw, cb, bi, bf, hn)


def _out_kernel(att_ref, hm_ref, ga_ref, gm_ref, x1_ref, watt_ref, wml_ref, wout_ref,
                g2_ref, wg_ref, wu_ref, wd_ref, gfin_ref, o_ref):
    ya = jnp.dot(att_ref[...], watt_ref[...], preferred_element_type=F32)
    ym = jnp.dot(hm_ref[...], wml_ref[...], preferred_element_type=F32)
    y = _sigmoid(ga_ref[...].astype(F32)) * ya + _sigmoid(gm_ref[...].astype(F32)) * ym
    x2 = x1_ref[...] + jnp.dot(y.astype(BF16), wout_ref[...], preferred_element_type=F32)
    h = _rms(x2, g2_ref[...]).astype(BF16)
    x3 = x2 + 0.5 * _swiglu(h, wg_ref, wu_ref, wd_ref)
    o_ref[...] = _rms(x3, gfin_ref[...])


def _out(att, hm, ga, gm, x1, watt, wml, wout, g2, wg, wu, wd, gfin, tm):
    t = x1.shape[0]
    tok = pl.BlockSpec((tm, D_MODEL), lambda i: (i, 0))
    sq = _resident((D_MODEL, D_MODEL))
    vec = _resident((1, D_MODEL))
    return pl.pallas_call(
        _out_kernel,
        grid=(t // tm,),
        in_specs=[tok, tok, tok, tok, tok, sq, sq, sq, vec, _resident((D_MODEL, D_FF)),
                  _resident((D_MODEL, D_FF)), _resident((D_FF, D_MODEL)), vec],
        out_specs=tok,
        out_shape=jax.ShapeDtypeStruct((t, D_MODEL), F32),
        compiler_params=pltpu.CompilerParams(dimension_semantics=("arbitrary",),
                                             vmem_limit_bytes=VMEM_LIMIT_BYTES),
        name="merge_ffn2",
    )(att, hm, ga, gm, x1, watt, wml, wout, g2, wg, wu, wd, gfin)


def _rope_tables(seq):
    half = ATT_HEAD_DIM // 2
    pos = jnp.arange(seq, dtype=F32)
    inv_freq = ROPE_THETA ** (-jnp.arange(half, dtype=F32) / half)
    ang = pos[:, None] * inv_freq[None, :]
    cos, sin = jnp.cos(ang), jnp.sin(ang)
    reps = LANES // ATT_HEAD_DIM
    cos = jnp.tile(jnp.concatenate([cos, cos], axis=-1), (1, reps))
    sin = jnp.tile(jnp.concatenate([-sin, sin], axis=-1), (1, reps))
    return cos, sin


def _pad_lanes(v):
    return jnp.pad(v, ((0, 0), (0, LANES - v.shape[-1])))


def _layer(x2d, batch, seq, p):
    (ffn1_norm, ffn1_w_gate, ffn1_w_up, ffn1_w_down, mix_norm, w_in, b_i, b_f, attn_sinks, conv_w, conv_b,
     head_norm, w_att, w_mlstm, w_out, ffn2_norm, ffn2_w_gate, ffn2_w_up, ffn2_w_down, final_norm) = p
    tm = min(TOKEN_TILE, seq)
    assert seq % tm == 0 and seq % ATT_BLOCK == 0 and seq % MLSTM_CHUNK == 0

    perm = np.asarray(Q_HEAD_PERM)
    col_perm = (perm[:, None] * ATT_HEAD_DIM + np.arange(ATT_HEAD_DIM)[None, :]).reshape(-1)
    offs = np.cumsum((0, ATT_Q_W, ATT_KV_W, ATT_KV_W, ML_QK_W, ML_QK_W, ML_V_W, ML_V_W,
                      MLSTM_HEADS, MLSTM_HEADS, D_MODEL))
    w_q = w_in[:, offs[0]:offs[1]][:, col_perm]
    w_main = w_in[:, offs[1]:offs[7]]
    w_i = _pad_lanes(w_in[:, offs[7]:offs[8]])
    w_f = _pad_lanes(w_in[:, offs[8]:offs[9]])
    w_g = w_in[:, offs[9]:]
    w_in_r = jnp.concatenate([w_q, w_main, w_g, w_i, w_f], axis=1).astype(BF16)
    assert w_in_r.shape[1] == IN_W

    cos, sin = _rope_tables(seq)
    q_scale = ATT_HEAD_DIM ** -0.5
    sinks = jnp.broadcast_to(attn_sinks[perm][:, None], (ATT_HEADS, LANES)).astype(F32)

    x1 = _ffn1(x2d, ffn1_norm[None], ffn1_w_gate.astype(BF16), ffn1_w_up.astype(BF16),
               ffn1_w_down.astype(BF16), tm)
    qa, kva, qkm, vm, om, ga, gm, gi, gf = _inproj(
        x1, mix_norm[None], cos * q_scale, sin * q_scale, cos, sin, w_in_r, tm, seq)
    att = _attn(qa, kva, sinks, batch, seq)
    hm = _mlstm(qkm, vm, om, gi, gf, conv_w, conv_b[None], _pad_lanes(b_i[None]), _pad_lanes(b_f[None]),
                head_norm[None], batch, seq)
    return _out(att, hm, ga, gm, x1, w_att[col_perm, :].astype(BF16), w_mlstm.astype(BF16),
                w_out.astype(BF16), ffn2_norm[None], ffn2_w_gate.astype(BF16), ffn2_w_up.astype(BF16),
                ffn2_w_down.astype(BF16), final_norm[None], tm)


def kernel(x, ffn1_norm, ffn1_w_gate, ffn1_w_up, ffn1_w_down, mix_norm, w_in, b_i, b_f, attn_sinks, conv_w,
           conv_b, head_norm, w_att, w_mlstm, w_out, ffn2_norm, ffn2_w_gate, ffn2_w_up, ffn2_w_down, final_norm):
    batch, seq, d = x.shape
    assert d == D_MODEL and ffn1_norm.shape[0] == 1, "single-layer kernel"
    per_layer = (ffn1_norm, ffn1_w_gate, ffn1_w_up, ffn1_w_down, mix_norm, w_in, b_i, b_f, attn_sinks, conv_w,
                 conv_b, head_norm, w_att, w_mlstm, w_out, ffn2_norm, ffn2_w_gate, ffn2_w_up, ffn2_w_down)
    params = tuple(a[0] for a in per_layer) + (final_norm,)
    out = _layer(x.reshape(batch * seq, d), batch, seq, params)
    return out.reshape(batch, seq, d)
```

```python
import functools

import jax
import jax.numpy as jnp
import numpy as np
from jax import lax
from jax.experimental import pallas as pl
from jax.experimental.pallas import tpu as pltpu

F32 = jnp.float32
BF16 = jnp.bfloat16

D_MODEL = 1024
ATT_HEADS = 16
ATT_KV_HEADS = 4
ATT_HEAD_DIM = 64
ATT_BLOCK = 128
ROPE_THETA = 10000.0
MLSTM_HEADS = 4
MLSTM_V_DIM = D_MODEL // MLSTM_HEADS
MLSTM_QK_DIM = MLSTM_V_DIM // 2
MLSTM_CHUNK = 128
MLSTM_CONV = 4
D_FF = 2816
RMS_EPS = 1e-5

ATT_Q_W = ATT_HEADS * ATT_HEAD_DIM
ATT_KV_W = ATT_KV_HEADS * ATT_HEAD_DIM
ML_QK_W = MLSTM_HEADS * MLSTM_QK_DIM
ML_V_W = MLSTM_HEADS * MLSTM_V_DIM

LANES = 128
CONV_TAIL_ROWS = 8
V7X_VMEM_BYTES = 64 * 1024 * 1024
VMEM_LIMIT_BYTES = 56 * 1024 * 1024

TOKEN_TILE = 512
FF_CHUNKS = ((0, 1024), (1024, 2048), (2048, D_FF))

_GROUP = ATT_HEADS // ATT_KV_HEADS
Q_HEAD_PERM = tuple(
    (2 * c + half) * _GROUP + r
    for c in range(ATT_KV_HEADS // 2) for r in range(_GROUP) for half in range(2))


def _resident(shape):
    return pl.BlockSpec(shape, lambda *_: (0,) * len(shape), pipeline_mode=pl.Buffered(1))


def _rms(x, g):
    ms = jnp.mean(x * x, axis=-1, keepdims=True)
    return x * lax.rsqrt(ms + RMS_EPS) * g


def _sigmoid(x):
    return 1.0 / (1.0 + jnp.exp(-x))


def _swiglu(h, wg_ref, wu_ref, wd_ref):
    acc = None
    for lo, hi in FF_CHUNKS:
        g = jnp.dot(h, wg_ref[:, lo:hi], preferred_element_type=F32)
        u = jnp.dot(h, wu_ref[:, lo:hi], preferred_element_type=F32)
        a = (g * _sigmoid(g) * u).astype(BF16)
        d = jnp.dot(a, wd_ref[lo:hi, :], preferred_element_type=F32)
        acc = d if acc is None else acc + d
    return acc


def _ffn1_kernel(x_ref, g_ref, wg_ref, wu_ref, wd_ref, o_ref):
    x = x_ref[...]
    h = _rms(x, g_ref[...]).astype(BF16)
    o_ref[...] = x + 0.5 * _swiglu(h, wg_ref, wu_ref, wd_ref)


def _ffn1(x, g, wg, wu, wd, tm):
    t = x.shape[0]
    tok = pl.BlockSpec((tm, D_MODEL), lambda i: (i, 0))
    return pl.pallas_call(
        _ffn1_kernel,
        grid=(t // tm,),
        in_specs=[tok, _resident((1, D_MODEL)), _resident((D_MODEL, D_FF)), _resident((D_MODEL, D_FF)),
                  _resident((D_FF, D_MODEL))],
        out_specs=tok,
        out_shape=jax.ShapeDtypeStruct((t, D_MODEL), F32),
        compiler_params=pltpu.CompilerParams(dimension_semantics=("arbitrary",),
                                             vmem_limit_bytes=VMEM_LIMIT_BYTES),
        name="ffn1",
    )(x, g, wg, wu, wd)


_C_QA = 0
_C_KA = _C_QA + ATT_Q_W
_C_VA = _C_KA + ATT_KV_W
_C_QKM = _C_VA + ATT_KV_W
_C_VM = _C_QKM + 2 * ML_QK_W
_C_OM = _C_VM + ML_V_W
_C_GA = _C_OM + ML_V_W
_C_GM = _C_GA + D_MODEL
_C_IF = _C_GM + D_MODEL
IN_W = _C_IF + 2 * LANES


def _rope(x, cos, sin_signed):
    lane = lax.broadcasted_iota(jnp.int32, x.shape, 1)
    half = ATT_HEAD_DIM // 2
    partner = jnp.where(lane % ATT_HEAD_DIM < half, pltpu.roll(x, LANES - half, axis=1),
                        pltpu.roll(x, half, axis=1))
    return x * cos + partner * sin_signed


def _inproj_kernel(x_ref, g_ref, cq_ref, sq_ref, ck_ref, sk_ref, w_ref,
                   qa_ref, kva_ref, qkm_ref, vm_ref, om_ref, ga_ref, gm_ref, gi_ref, gf_ref):
    h = _rms(x_ref[...], g_ref[...]).astype(BF16)

    def proj(lo, width):
        return jnp.dot(h, w_ref[:, lo:lo + width], preferred_element_type=F32)

    z = proj(_C_QA, ATT_Q_W)
    cq, sq = cq_ref[...], sq_ref[...]
    for j in range(ATT_Q_W // LANES):
        sl = slice(j * LANES, (j + 1) * LANES)
        qa_ref[:, sl] = _rope(z[:, sl], cq, sq).astype(BF16)

    z = proj(_C_KA, 2 * ATT_KV_W)
    ck, sk = ck_ref[...], sk_ref[...]
    for j in range(ATT_KV_W // LANES):
        sl = slice(j * LANES, (j + 1) * LANES)
        kva_ref[:, sl] = _rope(z[:, sl], ck, sk).astype(BF16)
    kva_ref[:, ATT_KV_W:] = z[:, ATT_KV_W:].astype(BF16)

    qkm_ref[...] = proj(_C_QKM, 2 * ML_QK_W).astype(BF16)
    vm_ref[...] = proj(_C_VM, ML_V_W).astype(BF16)
    om_ref[...] = proj(_C_OM, ML_V_W).astype(BF16)
    ga_ref[...] = proj(_C_GA, D_MODEL).astype(BF16)
    gm_ref[...] = proj(_C_GM, D_MODEL).astype(BF16)
    z = proj(_C_IF, 2 * LANES)
    gi_ref[...] = z[:, :LANES]
    gf_ref[...] = z[:, LANES:]


def _inproj(x1, g, cq, sq, ck, sk, w, tm, seq):
    t = x1.shape[0]
    tiles_per_seq = seq // tm

    def tok(width):
        return pl.BlockSpec((tm, width), lambda i: (i, 0))

    pos = pl.BlockSpec((tm, LANES), lambda i: (i % tiles_per_seq, 0))
    widths = (ATT_Q_W, 2 * ATT_KV_W, 2 * ML_QK_W, ML_V_W, ML_V_W, D_MODEL, D_MODEL)
    out_shape = [jax.ShapeDtypeStruct((t, w_), BF16) for w_ in widths]
    out_shape += [jax.ShapeDtypeStruct((t, LANES), F32)] * 2
    return pl.pallas_call(
        _inproj_kernel,
        grid=(t // tm,),
        in_specs=[tok(D_MODEL), _resident((1, D_MODEL)), pos, pos, pos, pos, _resident((D_MODEL, IN_W))],
        out_specs=[tok(w_) for w_ in widths] + [tok(LANES)] * 2,
        out_shape=out_shape,
        compiler_params=pltpu.CompilerParams(dimension_semantics=("arbitrary",),
                                             vmem_limit_bytes=VMEM_LIMIT_BYTES),
        name="in_proj",
    )(x1, g, cq, sq, ck, sk, w)


def _attn_kernel(q_ref, kvc_ref, kvp_ref, sink_ref, o_ref, pt_scr):
    blk = ATT_BLOCK
    has_prev = pl.program_id(1) > 0
    key = lax.broadcasted_iota(jnp.int32, (blk, blk), 0)
    qry = lax.broadcasted_iota(jnp.int32, (blk, blk), 1)
    in_cur = key <= qry
    low = lax.broadcasted_iota(jnp.int32, (blk, LANES), 1) < ATT_HEAD_DIM
    neg_inf = jnp.float32(-jnp.inf)
    n_chunks = ATT_KV_W // LANES

    for c in range(n_chunks):
        ksl = slice(c * LANES, (c + 1) * LANES)
        kc = kvc_ref[:, ksl].astype(F32)
        kp = kvp_ref[:, ksl].astype(F32)
        zero = jnp.zeros_like(kc)
        kstack = jnp.concatenate(
            [jnp.where(low, kc, zero), jnp.where(low, kp, zero),
             jnp.where(low, zero, kc), jnp.where(low, zero, kp)], axis=0).astype(BF16)
        qstack = jnp.concatenate(
            [q_ref[:, (c * _GROUP + r) * LANES:(c * _GROUP + r + 1) * LANES] for r in range(_GROUP)], axis=0)
        st_all = lax.dot_general(kstack, qstack, (((1,), (1,)), ((), ())), preferred_element_type=F32)
        for r in range(_GROUP):
            qs = slice(r * blk, (r + 1) * blk)
            for half in range(2):
                p_idx = (c * _GROUP + r) * 2 + half
                s_cur = st_all[(2 * half) * blk:(2 * half + 1) * blk, qs]
                s_prev = st_all[(2 * half + 1) * blk:(2 * half + 2) * blk, qs]
                s = jnp.where(in_cur, s_cur, jnp.where(has_prev, s_prev, neg_inf))
                sink = sink_ref[p_idx:p_idx + 1, :]
                m = jnp.maximum(jnp.max(s, axis=0, keepdims=True), sink)
                p = jnp.exp(s - m)
                denom = jnp.sum(p, axis=0, keepdims=True) + jnp.exp(sink - m)
                p = p * (1.0 / denom)
                zp = jnp.zeros_like(p)
                pt_scr[c * 2 + half, :blk, qs] = jnp.where(in_cur, p, zp).astype(BF16)
                pt_scr[c * 2 + half, blk:, qs] = jnp.where(in_cur, zp, p).astype(BF16)

    v_cat = jnp.concatenate([kvc_ref[:, ATT_KV_W:], kvp_ref[:, ATT_KV_W:]], axis=0).astype(F32)
    vt_all = v_cat.T.astype(BF16)
    for c in range(n_chunks):
        ot = [jnp.dot(vt_all[(2 * c + half) * ATT_HEAD_DIM:(2 * c + half + 1) * ATT_HEAD_DIM, :],
                      pt_scr[c * 2 + half], preferred_element_type=F32) for half in range(2)]
        for r in range(_GROUP):
            j = c * _GROUP + r
            qs = slice(r * blk, (r + 1) * blk)
            o_t = jnp.concatenate([ot[0][:, qs], ot[1][:, qs]], axis=0)
            o_ref[:, j * LANES:(j + 1) * LANES] = o_t.T.astype(BF16)


def _attn(qa, kva, sinks, batch, seq):
    t = qa.shape[0]
    nb = seq // ATT_BLOCK
    return pl.pallas_call(
        _attn_kernel,
        grid=(batch, nb),
        in_specs=[pl.BlockSpec((ATT_BLOCK, ATT_Q_W), lambda b, n: (b * nb + n, 0)),
                  pl.BlockSpec((ATT_BLOCK, 2 * ATT_KV_W), lambda b, n: (b * nb + n, 0)),
                  pl.BlockSpec((ATT_BLOCK, 2 * ATT_KV_W), lambda b, n: (b * nb + jnp.maximum(n - 1, 0), 0)),
                  _resident((ATT_HEADS, LANES))],
        out_specs=pl.BlockSpec((ATT_BLOCK, ATT_Q_W), lambda b, n: (b * nb + n, 0)),
        out_shape=jax.ShapeDtypeStruct((t, ATT_Q_W), BF16),
        scratch_shapes=[pltpu.VMEM((ATT_KV_HEADS, 2 * ATT_BLOCK, _GROUP * ATT_BLOCK), BF16)],
        compiler_params=pltpu.CompilerParams(dimension_semantics=("arbitrary", "arbitrary"),
                                             vmem_limit_bytes=VMEM_LIMIT_BYTES),
        name="swa_attn",
    )(qa, kva, kva, sinks)


def _scan_rows(x, op, fill):
    n = x.shape[0]
    row = lax.broadcasted_iota(jnp.int32, x.shape, 0)
    shift = 1
    while shift < n:
        prev = jnp.where(row >= shift, pltpu.roll(x, shift, axis=0), fill)
        x = op(x, prev)
        shift *= 2
    return x


def _mlstm_kernel(qk_ref, v_ref, o_ref, gi_ref, gf_ref, cw_ref, cb_ref, bi_ref, bf_ref, hn_ref,
                  out_ref, ct_scr, n_scr, m_scr, conv_scr):
    L = MLSTM_CHUNK
    tail = CONV_TAIL_ROWS

    @pl.when(pl.program_id(1) == 0)
    def _():
        ct_scr[...] = jnp.zeros_like(ct_scr)
        n_scr[...] = jnp.zeros_like(n_scr)
        m_scr[...] = jnp.zeros_like(m_scr)
        conv_scr[:tail, :] = jnp.zeros((tail, conv_scr.shape[1]), F32)

    x = qk_ref[...].astype(F32)
    conv_scr[tail:, :] = x
    y = cb_ref[...] + cw_ref[MLSTM_CONV - 1:MLSTM_CONV, :] * x
    for j in range(1, MLSTM_CONV):
        y = y + cw_ref[MLSTM_CONV - 1 - j:MLSTM_CONV - j, :] * conv_scr[tail - j:tail - j + L, :]
    conv_scr[:tail, :] = x[L - tail:, :]
    qk = y * _sigmoid(y)

    ig = gi_ref[...] + bi_ref[...]
    fpre = gf_ref[...] + bf_ref[...]
    logf = jnp.minimum(fpre, 0.0) - jnp.log1p(jnp.exp(-jnp.abs(fpre)))
    b = _scan_rows(logf, jnp.add, 0.0)
    g = ig - b
    cm = _scan_rows(g, jnp.maximum, -jnp.inf)
    m_prev = m_scr[0:1, :]
    u = jnp.maximum(cm, m_prev)
    g_max = cm[L - 1:L, :]
    u_last = u[L - 1:L, :]
    a_t = jnp.exp(g - g_max)
    inter_t = jnp.exp(m_prev - u)
    emt_t = jnp.exp(-(b + u))
    s_old = jnp.exp(m_prev - u_last)
    s_new = jnp.exp(g_max - u_last)
    m_scr[0:1, :] = b[L - 1:L, :] + u_last
    g_rows = g.T

    row = lax.broadcasted_iota(jnp.int32, (L, L), 0)
    col = lax.broadcasted_iota(jnp.int32, (L, L), 1)
    causal = col <= row
    k_scale = MLSTM_QK_DIM ** -0.5

    for h in range(MLSTM_HEADS):
        qs = slice(h * MLSTM_QK_DIM, (h + 1) * MLSTM_QK_DIM)
        ks = slice(ML_QK_W + h * MLSTM_QK_DIM, ML_QK_W + (h + 1) * MLSTM_QK_DIM)
        vs = slice(h * MLSTM_V_DIM, (h + 1) * MLSTM_V_DIM)
        qh = qk[:, qs]
        kh = qk[:, ks] * k_scale
        qb = qh.astype(BF16)
        kb = kh.astype(BF16)
        vb = v_ref[:, vs]
        u_col = u[:, h:h + 1]
        inter_col = inter_t[:, h:h + 1]
        emt_col = emt_t[:, h:h + 1]
        a_col = a_t[:, h:h + 1]

        decay = jnp.where(causal, jnp.exp(g_rows[h:h + 1, :] - u_col), 0.0)
        s = lax.dot_general(qb, kb, (((1,), (1,)), ((), ())), preferred_element_type=F32) * decay
        ct = ct_scr[h]
        n_row = n_scr[h:h + 1, :]
        num = (jnp.dot(s.astype(BF16), vb, preferred_element_type=F32)
               + inter_col * jnp.dot(qb, ct.astype(BF16), preferred_element_type=F32))
        den = jnp.sum(s + inter_col * (qh * n_row), axis=-1, keepdims=True)
        hh = num * (1.0 / jnp.maximum(jnp.abs(den), emt_col))
        hn = _rms(hh, hn_ref[:, vs])
        out_ref[:, vs] = (_sigmoid(o_ref[:, vs].astype(F32)) * hn).astype(BF16)

        av = (a_col * vb.astype(F32)).astype(BF16)
        d_ct = lax.dot_general(kb, av, (((0,), (0,)), ((), ())), preferred_element_type=F32)
        d_n = jnp.sum(a_col * kh, axis=0, keepdims=True)
        so = s_old[:, h:h + 1]
        sn = s_new[:, h:h + 1]
        ct_scr[h] = so * ct + sn * d_ct
        n_scr[h:h + 1, :] = so * n_row + sn * d_n


def _mlstm(qkm, vm, om, gi, gf, cw, cb, bi, bf, hn, batch, seq):
    t = qkm.shape[0]
    nc = seq // MLSTM_CHUNK

    def tok(width):
        return pl.BlockSpec((MLSTM_CHUNK, width), lambda b, c: (b * nc + c, 0))

    return pl.pallas_call(
        _mlstm_kernel,
        grid=(batch, nc),
        in_specs=[tok(2 * ML_QK_W), tok(ML_V_W), tok(ML_V_W), tok(LANES), tok(LANES),
                  _resident((MLSTM_CONV, 2 * ML_QK_W)), _resident((1, 2 * ML_QK_W)),
                  _resident((1, LANES)), _resident((1, LANES)), _resident((1, ML_V_W))],
        out_specs=tok(ML_V_W),
        out_shape=jax.ShapeDtypeStruct((t, ML_V_W), BF16),
        scratch_shapes=[pltpu.VMEM((MLSTM_HEADS, MLSTM_QK_DIM, MLSTM_V_DIM), F32),
                        pltpu.VMEM((8, LANES), F32),
                        pltpu.VMEM((8, LANES), F32),
                        pltpu.VMEM((CONV_TAIL_ROWS + MLSTM_CHUNK, 2 * ML_QK_W), F32)],
        compiler_params=pltpu.CompilerParams(dimension_semantics=("arbitrary", "arbitrary"),
                                             vmem_limit_bytes=VMEM_LIMIT_BYTES),
        name="mlstm",
    )(qkm, vm, om, gi, gf, cw, cb, bi, bf, hn)


def _out_kernel(att_ref, hm_ref, ga_ref, gm_ref, x1_ref, watt_ref, wml_ref, wout_ref,
                g2_ref, wg_ref, wu_ref, wd_ref, gfin_ref, o_ref):
    ya = jnp.dot(att_ref[...], watt_ref[...], preferred_element_type=F32)
    ym = jnp.dot(hm_ref[...], wml_ref[...], preferred_element_type=F32)
    y = _sigmoid(ga_ref[...].astype(F32)) * ya + _sigmoid(gm_ref[...].astype(F32)) * ym
    x2 = x1_ref[...] + jnp.dot(y.astype(BF16), wout_ref[...], preferred_element_type=F32)
    h = _rms(x2, g2_ref[...]).astype(BF16)
    x3 = x2 + 0.5 * _swiglu(h, wg_ref, wu_ref, wd_ref)
    o_ref[...] = _rms(x3, gfin_ref[...])


def _out(att, hm, ga, gm, x1, watt, wml, wout, g2, wg, wu, wd, gfin, tm):
    t = x1.shape[0]
    tok = pl.BlockSpec((tm, D_MODEL), lambda i: (i, 0))
    sq = _resident((D_MODEL, D_MODEL))
    vec = _resident((1, D_MODEL))
    return pl.pallas_call(
        _out_kernel,
        grid=(t // tm,),
        in_specs=[tok, tok, tok, tok, tok, sq, sq, sq, vec, _resident((D_MODEL, D_FF)),
                  _resident((D_MODEL, D_FF)), _resident((D_FF, D_MODEL)), vec],
        out_specs=tok,
        out_shape=jax.ShapeDtypeStruct((t, D_MODEL), F32),
        compiler_params=pltpu.CompilerParams(dimension_semantics=("arbitrary",),
                                             vmem_limit_bytes=VMEM_LIMIT_BYTES),
        name="merge_ffn2",
    )(att, hm, ga, gm, x1, watt, wml, wout, g2, wg, wu, wd, gfin)


def _rope_tables(seq):
    half = ATT_HEAD_DIM // 2
    pos = jnp.arange(seq, dtype=F32)
    inv_freq = ROPE_THETA ** (-jnp.arange(half, dtype=F32) / half)
    ang = pos[:, None] * inv_freq[None, :]
    cos, sin = jnp.cos(ang), jnp.sin(ang)
    reps = LANES // ATT_HEAD_DIM
    cos = jnp.tile(jnp.concatenate([cos, cos], axis=-1), (1, reps))
    sin = jnp.tile(jnp.concatenate([-sin, sin], axis=-1), (1, reps))
    return cos, sin


def _pad_lanes(v):
    return jnp.pad(v, ((0, 0), (0, LANES - v.shape[-1])))


def _layer(x2d, batch, seq, p):
    (ffn1_norm, ffn1_w_gate, ffn1_w_up, ffn1_w_down, mix_norm, w_in, b_i, b_f, attn_sinks, conv_w, conv_b,
     head_norm, w_att, w_mlstm, w_out, ffn2_norm, ffn2_w_gate, ffn2_w_up, ffn2_w_down, final_norm) = p
    tm = min(TOKEN_TILE, seq)
    assert seq % tm == 0 and seq % ATT_BLOCK == 0 and seq % MLSTM_CHUNK == 0

    perm = np.asarray(Q_HEAD_PERM)
    col_perm = (perm[:, None] * ATT_HEAD_DIM + np.arange(ATT_HEAD_DIM)[None, :]).reshape(-1)
    offs = np.cumsum((0, ATT_Q_W, ATT_KV_W, ATT_KV_W, ML_QK_W, ML_QK_W, ML_V_W, ML_V_W,
                      MLSTM_HEADS, MLSTM_HEADS, D_MODEL))
    w_q = w_in[:, offs[0]:offs[1]][:, col_perm]
    w_main = w_in[:, offs[1]:offs[7]]
    w_i = _pad_lanes(w_in[:, offs[7]:offs[8]])
    w_f = _pad_lanes(w_in[:, offs[8]:offs[9]])
    w_g = w_in[:, offs[9]:]
    w_in_r = jnp.concatenate([w_q, w_main, w_g, w_i, w_f], axis=1).astype(BF16)
    assert w_in_r.shape[1] == IN_W

    cos, sin = _rope_tables(seq)
    q_scale = ATT_HEAD_DIM ** -0.5
    sinks = jnp.broadcast_to(attn_sinks[perm][:, None], (ATT_HEADS, LANES)).astype(F32)

    x1 = _ffn1(x2d, ffn1_norm[None], ffn1_w_gate.astype(BF16), ffn1_w_up.astype(BF16),
               ffn1_w_down.astype(BF16), tm)
    qa, kva, qkm, vm, om, ga, gm, gi, gf = _inproj(
        x1, mix_norm[None], cos * q_scale, sin * q_scale, cos, sin, w_in_r, tm, seq)
    att = _attn(qa, kva, sinks, batch, seq)
    hm = _mlstm(qkm, vm, om, gi, gf, conv_w, conv_b[None], _pad_lanes(b_i[None]), _pad_lanes(b_f[None]),
                head_norm[None], batch, seq)
    return _out(att, hm, ga, gm, x1, w_att[col_perm, :].astype(BF16), w_mlstm.astype(BF16),
                w_out.astype(BF16), ffn2_norm[None], ffn2_w_gate.astype(BF16), ffn2_w_up.astype(BF16),
                ffn2_w_down.astype(BF16), final_norm[None], tm)


def kernel(x, ffn1_norm, ffn1_w_gate, ffn1_w_up, ffn1_w_down, mix_norm, w_in, b_i, b_f, attn_sinks, conv_w,
           conv_b, head_norm, w_att, w_mlstm, w_out, ffn2_norm, ffn2_w_gate, ffn2_w_up, ffn2_w_down, final_norm):
    batch, seq, d = x.shape
    assert d == D_MODEL and ffn1_norm.shape[0] == 1, "single-layer kernel"
    per_layer = (ffn1_norm, ffn1_w_gate, ffn1_w_up, ffn1_w_down, mix_norm, w_in, b_i, b_f, attn_sinks, conv_w,
                 conv_b, head_norm, w_att, w_mlstm, w_out, ffn2_norm, ffn2_w_gate, ffn2_w_up, ffn2_w_down)
    params = tuple(a[0] for a in per_layer) + (final_norm,)
    out = _layer(x.reshape(batch * seq, d), batch, seq, params)
    return out.reshape(batch, seq, d)
```

```python
import functools

import jax
import jax.numpy as jnp
import numpy as np
from jax import lax
from jax.experimental import pallas as pl
from jax.experimental.pallas import tpu as pltpu

F32 = jnp.float32
BF16 = jnp.bfloat16

D_MODEL = 1024
ATT_HEADS = 16
ATT_KV_HEADS = 4
ATT_HEAD_DIM = 64
ATT_BLOCK = 128
ROPE_THETA = 10000.0
MLSTM_HEADS = 4
MLSTM_V_DIM = D_MODEL // MLSTM_HEADS
MLSTM_QK_DIM = MLSTM_V_DIM // 2
MLSTM_CHUNK = 128
MLSTM_CONV = 4
D_FF = 2816
RMS_EPS = 1e-5

ATT_Q_W = ATT_HEADS * ATT_HEAD_DIM
ATT_KV_W = ATT_KV_HEADS * ATT_HEAD_DIM
ML_QK_W = MLSTM_HEADS * MLSTM_QK_DIM
ML_V_W = MLSTM_HEADS * MLSTM_V_DIM

LANES = 128
SUBLANES = 8
CONV_TAIL_ROWS = SUBLANES
VMEM_LIMIT_BYTES = 56 * 1024 * 1024

TOKEN_TILE = 512
FF_CHUNKS = ((0, 1024), (1024, 2048), (2048, D_FF))

_GROUP = ATT_HEADS // ATT_KV_HEADS
Q_HEAD_PERM = tuple(
    (2 * c + half) * _GROUP + r
    for c in range(ATT_KV_HEADS // 2) for r in range(_GROUP) for half in range(2))


def _resident(shape):
    return pl.BlockSpec(shape, lambda *_: (0,) * len(shape), pipeline_mode=pl.Buffered(1))


def _rms(x, g):
    ms = jnp.mean(x * x, axis=-1, keepdims=True)
    return x * lax.rsqrt(ms + RMS_EPS) * g


def _sigmoid(x):
    return 1.0 / (1.0 + jnp.exp(-x))


def _swiglu(h, wg_ref, wu_ref, wd_ref):
    acc = None
    for lo, hi in FF_CHUNKS:
        g = jnp.dot(h, wg_ref[:, lo:hi], preferred_element_type=F32)
        u = jnp.dot(h, wu_ref[:, lo:hi], preferred_element_type=F32)
        a = (g * _sigmoid(g) * u).astype(BF16)
        d = jnp.dot(a, wd_ref[lo:hi, :], preferred_element_type=F32)
        acc = d if acc is None else acc + d
    return acc


def _ffn1_kernel(x_ref, g_ref, wg_ref, wu_ref, wd_ref, o_ref):
    x = x_ref[...]
    h = _rms(x, g_ref[...]).astype(BF16)
    o_ref[...] = x + 0.5 * _swiglu(h, wg_ref, wu_ref, wd_ref)


def _ffn1(x, g, wg, wu, wd, tm):
    t = x.shape[0]
    tok = pl.BlockSpec((tm, D_MODEL), lambda i: (i, 0))
    return pl.pallas_call(
        _ffn1_kernel,
        grid=(t // tm,),
        in_specs=[tok, _resident((1, D_MODEL)), _resident((D_MODEL, D_FF)), _resident((D_MODEL, D_FF)),
                  _resident((D_FF, D_MODEL))],
        out_specs=tok,
        out_shape=jax.ShapeDtypeStruct((t, D_MODEL), F32),
        compiler_params=pltpu.CompilerParams(dimension_semantics=("arbitrary",),
                                             vmem_limit_bytes=VMEM_LIMIT_BYTES),
        name="ffn1",
    )(x, g, wg, wu, wd)


_C_QA = 0
_C_KA = _C_QA + ATT_Q_W
_C_VA = _C_KA + ATT_KV_W
_C_QKM = _C_VA + ATT_KV_W
_C_VM = _C_QKM + 2 * ML_QK_W
_C_OM = _C_VM + ML_V_W
_C_GA = _C_OM + ML_V_W
_C_GM = _C_GA + D_MODEL
_C_IF = _C_GM + D_MODEL
IN_W = _C_IF + 2 * LANES


def _rope(x, cos, sin_signed):
    lane = lax.broadcasted_iota(jnp.int32, x.shape, 1)
    half = ATT_HEAD_DIM // 2
    partner = jnp.where(lane % ATT_HEAD_DIM < half, pltpu.roll(x, LANES - half, axis=1),
                        pltpu.roll(x, half, axis=1))
    return x * cos + partner * sin_signed


def _attn_scores(q_blk, k_cur, k_prev):
    low = lax.broadcasted_iota(jnp.int32, (ATT_BLOCK, LANES), 1) < ATT_HEAD_DIM
    out = []
    for c in range(ATT_KV_W // LANES):
        ksl = slice(c * LANES, (c + 1) * LANES)
        kc = k_cur[:, ksl].astype(F32)
        kp = k_prev[:, ksl].astype(F32)
        zero = jnp.zeros_like(kc)
        kstack = jnp.concatenate(
            [jnp.where(low, kc, zero), jnp.where(low, kp, zero),
             jnp.where(low, zero, kc), jnp.where(low, zero, kp)], axis=0).astype(BF16)
        qstack = jnp.concatenate(
            [q_blk[:, (c * _GROUP + r) * LANES:(c * _GROUP + r + 1) * LANES] for r in range(_GROUP)], axis=0)
        out.append(lax.dot_general(kstack, qstack, (((1,), (1,)), ((), ())), preferred_element_type=F32))
    return out


def _attn_finish(st, v_cur, v_prev, has_prev, sink_ref, pt_scr, o_ref, rows):
    blk = ATT_BLOCK
    key = lax.broadcasted_iota(jnp.int32, (blk, blk), 0)
    qry = lax.broadcasted_iota(jnp.int32, (blk, blk), 1)
    in_cur = key <= qry
    neg_inf = jnp.float32(-jnp.inf)
    n_chunks = ATT_KV_W // LANES
    for c in range(n_chunks):
        for r in range(_GROUP):
            qs = slice(r * blk, (r + 1) * blk)
            for half in range(2):
                p_idx = (c * _GROUP + r) * 2 + half
                s_cur = st[c][(2 * half) * blk:(2 * half + 1) * blk, qs]
                s_prev = st[c][(2 * half + 1) * blk:(2 * half + 2) * blk, qs]
                s = jnp.where(in_cur, s_cur, jnp.where(has_prev, s_prev, neg_inf))
                sink = sink_ref[p_idx:p_idx + 1, :]
                m = jnp.maximum(jnp.max(s, axis=0, keepdims=True), sink)
                p = jnp.exp(s - m)
                denom = jnp.sum(p, axis=0, keepdims=True) + jnp.exp(sink - m)
                p = p * (1.0 / denom)
                zp = jnp.zeros_like(p)
                pt_scr[c * 2 + half, :blk, qs] = jnp.where(in_cur, p, zp).astype(BF16)
                pt_scr[c * 2 + half, blk:, qs] = jnp.where(in_cur, zp, p).astype(BF16)

    v_cat = jnp.concatenate([v_cur, v_prev], axis=0).astype(F32)
    vt_all = v_cat.T.astype(BF16)
    for c in range(n_chunks):
        ot = [jnp.dot(vt_all[(2 * c + half) * ATT_HEAD_DIM:(2 * c + half + 1) * ATT_HEAD_DIM, :],
                      pt_scr[c * 2 + half], preferred_element_type=F32) for half in range(2)]
        for r in range(_GROUP):
            j = c * _GROUP + r
            qs = slice(r * blk, (r + 1) * blk)
            o_t = jnp.concatenate([ot[0][:, qs], ot[1][:, qs]], axis=0)
            o_ref[rows, j * LANES:(j + 1) * LANES] = o_t.T.astype(BF16)


def _inproj_kernel(blocks_per_seq, x_ref, g_ref, cq_ref, sq_ref, ck_ref, sk_ref, w_ref, sink_ref,
                   att_ref, qkm_ref, vm_ref, om_ref, ga_ref, gm_ref, gi_ref, gf_ref, q_scr, kv_scr, pt_scr):
    tm = x_ref.shape[0]
    blk = ATT_BLOCK
    n_blk = tm // blk
    step = pl.program_id(0)
    slot = step % 2
    q_new, kv_new = q_scr.at[slot], kv_scr.at[slot]
    q_old, kv_old = q_scr.at[1 - slot], kv_scr.at[1 - slot]

    @pl.when(step == 0)
    def _():
        q_scr[...] = jnp.zeros_like(q_scr)
        kv_scr[...] = jnp.zeros_like(kv_scr)

    h = _rms(x_ref[...], g_ref[...]).astype(BF16)
    kv_new[:blk, :] = kv_old[tm:, :]

    def proj(lo, width):
        return jnp.dot(h, w_ref[:, lo:lo + width], preferred_element_type=F32)

    def proj_q():
        z = proj(_C_QA, ATT_Q_W)
        cq, sq = cq_ref[...], sq_ref[...]
        for j in range(ATT_Q_W // LANES):
            sl = slice(j * LANES, (j + 1) * LANES)
            q_new[:, sl] = _rope(z[:, sl], cq, sq).astype(BF16)

    def proj_kv_vm():
        z = proj(_C_KA, 2 * ATT_KV_W)
        ck, sk = ck_ref[...], sk_ref[...]
        for j in range(ATT_KV_W // LANES):
            sl = slice(j * LANES, (j + 1) * LANES)
            kv_new[blk:, sl] = _rope(z[:, sl], ck, sk).astype(BF16)
        kv_new[blk:, ATT_KV_W:] = z[:, ATT_KV_W:].astype(BF16)
        vm_ref[...] = proj(_C_VM, ML_V_W).astype(BF16)

    def proj_om_ga():
        om_ref[...] = proj(_C_OM, ML_V_W).astype(BF16)
        ga_ref[...] = proj(_C_GA, D_MODEL).astype(BF16)

    def proj_gm_if_qkm():
        gm_ref[...] = proj(_C_GM, D_MODEL).astype(BF16)
        z = proj(_C_IF, 2 * LANES)
        gi_ref[...] = z[:, :LANES]
        gf_ref[...] = z[:, LANES:]
        qkm_ref[...] = proj(_C_QKM, 2 * ML_QK_W).astype(BF16)

    groups = (proj_q, proj_kv_vm, proj_om_ga, proj_gm_if_qkm)
    for k in range(n_blk):
        rows = slice(k * blk, (k + 1) * blk)
        kv_c = kv_old[blk + k * blk:2 * blk + k * blk, :]
        kv_p = kv_old[k * blk:(k + 1) * blk, :]
        st = _attn_scores(q_old[rows, :], kv_c[:, :ATT_KV_W], kv_p[:, :ATT_KV_W])
        for group in groups[k * len(groups) // n_blk:(k + 1) * len(groups) // n_blk]:
            group()
        has_prev = ((step - 1) * n_blk + k) % blocks_per_seq != 0
        _attn_finish(st, kv_c[:, ATT_KV_W:], kv_p[:, ATT_KV_W:], has_prev, sink_ref, pt_scr, att_ref, rows)


def _inproj(x1, g, cq, sq, ck, sk, w, sinks, tm, seq):
    t = x1.shape[0]
    n = t // tm
    tiles_per_seq = seq // tm

    def cur(i):
        return jnp.minimum(i, n - 1)

    def tok(width):
        return pl.BlockSpec((tm, width), lambda i: (cur(i), 0))

    pos = pl.BlockSpec((tm, LANES), lambda i: (cur(i) % tiles_per_seq, 0))
    lagged = pl.BlockSpec((tm, ATT_Q_W), lambda i: (jnp.maximum(i - 1, 0), 0))
    widths = (2 * ML_QK_W, ML_V_W, ML_V_W, D_MODEL, D_MODEL)
    out_shape = [jax.ShapeDtypeStruct((t, ATT_Q_W), BF16)]
    out_shape += [jax.ShapeDtypeStruct((t, w_), BF16) for w_ in widths]
    out_shape += [jax.ShapeDtypeStruct((t, LANES), F32)] * 2
    return pl.pallas_call(
        functools.partial(_inproj_kernel, seq // ATT_BLOCK),
        grid=(n + 1,),
        in_specs=[tok(D_MODEL), _resident((1, D_MODEL)), pos, pos, pos, pos, _resident((D_MODEL, IN_W)),
                  _resident((ATT_HEADS, LANES))],
        out_specs=[lagged] + [tok(w_) for w_ in widths] + [tok(LANES)] * 2,
        out_shape=out_shape,
        scratch_shapes=[pltpu.VMEM((2, tm, ATT_Q_W), BF16),
                        pltpu.VMEM((2, ATT_BLOCK + tm, 2 * ATT_KV_W), BF16),
                        pltpu.VMEM((ATT_KV_HEADS, 2 * ATT_BLOCK, _GROUP * ATT_BLOCK), BF16)],
        compiler_params=pltpu.CompilerParams(dimension_semantics=("arbitrary",),
                                             vmem_limit_bytes=VMEM_LIMIT_BYTES),
        name="in_proj_attn",
    )(x1, g, cq, sq, ck, sk, w, sinks)


def _scan_rows(x, op, fill):
    n = x.shape[0]
    row = lax.broadcasted_iota(jnp.int32, x.shape, 0)
    shift = 1
    while shift < n:
        prev = jnp.where(row >= shift, pltpu.roll(x, shift, axis=0), fill)
        x = op(x, prev)
        shift *= 2
    return x


def _mlstm_kernel(qk_ref, v_ref, o_ref, gi_ref, gf_ref, cw_ref, cb_ref, bi_ref, bf_ref, hn_ref,
                  out_ref, ct_scr, n_scr, m_scr, conv_scr):
    L = MLSTM_CHUNK
    tail = CONV_TAIL_ROWS

    @pl.when(pl.program_id(1) == 0)
    def _():
        ct_scr[...] = jnp.zeros_like(ct_scr)
        n_scr[...] = jnp.zeros_like(n_scr)
        m_scr[...] = jnp.zeros_like(m_scr)
        conv_scr[:tail, :] = jnp.zeros((tail, conv_scr.shape[1]), F32)

    x = qk_ref[...].astype(F32)
    conv_scr[tail:, :] = x
    y = cb_ref[...] + cw_ref[MLSTM_CONV - 1:MLSTM_CONV, :] * x
    for j in range(1, MLSTM_CONV):
        y = y + cw_ref[MLSTM_CONV - 1 - j:MLSTM_CONV - j, :] * conv_scr[tail - j:tail - j + L, :]
    conv_scr[:tail, :] = x[L - tail:, :]
    qk = y * _sigmoid(y)

    ig = gi_ref[...] + bi_ref[...]
    fpre = gf_ref[...] + bf_ref[...]
    logf = jnp.minimum(fpre, 0.0) - jnp.log1p(jnp.exp(-jnp.abs(fpre)))
    b = _scan_rows(logf, jnp.add, 0.0)
    g = ig - b
    cm = _scan_rows(g, jnp.maximum, -jnp.inf)
    m_prev = m_scr[0:1, :]
    u = jnp.maximum(cm, m_prev)
    g_max = cm[L - 1:L, :]
    u_last = u[L - 1:L, :]
    a_t = jnp.exp(g - g_max)
    inter_t = jnp.exp(m_prev - u)
    emt_t = jnp.exp(-(b + u))
    s_old = jnp.exp(m_prev - u_last)
    s_new = jnp.exp(g_max - u_last)
    m_scr[0:1, :] = b[L - 1:L, :] + u_last
    g_rows = g.T

    row = lax.broadcasted_iota(jnp.int32, (L, L), 0)
    col = lax.broadcasted_iota(jnp.int32, (L, L), 1)
    causal = col <= row
    k_scale = MLSTM_QK_DIM ** -0.5

    for h in range(MLSTM_HEADS):
        qs = slice(h * MLSTM_QK_DIM, (h + 1) * MLSTM_QK_DIM)
        ks = slice(ML_QK_W + h * MLSTM_QK_DIM, ML_QK_W + (h + 1) * MLSTM_QK_DIM)
        vs = slice(h * MLSTM_V_DIM, (h + 1) * MLSTM_V_DIM)
        qh = qk[:, qs]
        kh = qk[:, ks] * k_scale
        qb = qh.astype(BF16)
        kb = kh.astype(BF16)
        vb = v_ref[:, vs]
        u_col = u[:, h:h + 1]
        inter_col = inter_t[:, h:h + 1]
        emt_col = emt_t[:, h:h + 1]
        a_col = a_t[:, h:h + 1]

        decay = jnp.where(causal, jnp.exp(g_rows[h:h + 1, :] - u_col), 0.0)
        s = lax.dot_general(qb, kb, (((1,), (1,)), ((), ())), preferred_element_type=F32) * decay
        ct = ct_scr[h]
        n_row = n_scr[h:h + 1, :]
        num = (jnp.dot(s.astype(BF16), vb, preferred_element_type=F32)
               + inter_col * jnp.dot(qb, ct.astype(BF16), preferred_element_type=F32))
        den = jnp.sum(s + inter_col * (qh * n_row), axis=-1, keepdims=True)
        hh = num * (1.0 / jnp.maximum(jnp.abs(den), emt_col))
        hn = _rms(hh, hn_ref[:, vs])
        out_ref[:, vs] = (_sigmoid(o_ref[:, vs].astype(F32)) * hn).astype(BF16)

        av = (a_col * vb.astype(F32)).astype(BF16)
        d_ct = lax.dot_general(kb, av, (((0,), (0,)), ((), ())), preferred_element_type=F32)
        d_n = jnp.sum(a_col * kh, axis=0, keepdims=True)
        so = s_old[:, h:h + 1]
        sn = s_new[:, h:h + 1]
        ct_scr[h] = so * ct + sn * d_ct
        n_scr[h:h + 1, :] = so * n_row + sn * d_n


def _mlstm(qkm, vm, om, gi, gf, cw, cb, bi, bf, hn, batch, seq):
    t = qkm.shape[0]
    nc = seq // MLSTM_CHUNK

    def tok(width):
        return pl.BlockSpec((MLSTM_CHUNK, width), lambda b, c: (b * nc + c, 0))

    return pl.pallas_call(
        _mlstm_kernel,
        grid=(batch, nc),
        in_specs=[tok(2 * ML_QK_W), tok(ML_V_W), tok(ML_V_W), tok(LANES), tok(LANES),
                  _resident((MLSTM_CONV, 2 * ML_QK_W)), _resident((1, 2 * ML_QK_W)),
                  _resident((1, LANES)), _resident((1, LANES)), _resident((1, ML_V_W))],
        out_specs=tok(ML_V_W),
        out_shape=jax.ShapeDtypeStruct((t, ML_V_W), BF16),
        scratch_shapes=[pltpu.VMEM((MLSTM_HEADS, MLSTM_QK_DIM, MLSTM_V_DIM), F32),
                        pltpu.VMEM((SUBLANES, LANES), F32),
                        pltpu.VMEM((SUBLANES, LANES), F32),
                        pltpu.VMEM((CONV_TAIL_ROWS + MLSTM_CHUNK, 2 * ML_QK_W), F32)],
        compiler_params=pltpu.CompilerParams(dimension_semantics=("arbitrary", "arbitrary"),
                                             vmem_limit_bytes=VMEM_LIMIT_BYTES),
        name="mlstm",
    )(qkm, vm, om, gi, gf, cw, cb, bi, bf, hn)


def _out_kernel(att_ref, hm_ref, ga_ref, gm_ref, x1_ref, watt_ref, wml_ref, wout_ref,
                g2_ref, wg_ref, wu_ref, wd_ref, gfin_ref, o_ref):
    ya = jnp.dot(att_ref[...], watt_ref[...], preferred_element_type=F32)
    ym = jnp.dot(hm_ref[...], wml_ref[...], preferred_element_type=F32)
    y = _sigmoid(ga_ref[...].astype(F32)) * ya + _sigmoid(gm_ref[...].astype(F32)) * ym
    x2 = x1_ref[...] + jnp.dot(y.astype(BF16), wout_ref[...], preferred_element_type=F32)
    h = _rms(x2, g2_ref[...]).astype(BF16)
    x3 = x2 + 0.5 * _swiglu(h, wg_ref, wu_ref, wd_ref)
    o_ref[...] = _rms(x3, gfin_ref[...])


def _out(att, hm, ga, gm, x1, watt, wml, wout, g2, wg, wu, wd, gfin, tm):
    t = x1.shape[0]
    tok = pl.BlockSpec((tm, D_MODEL), lambda i: (i, 0))
    sq = _resident((D_MODEL, D_MODEL))
    vec = _resident((1, D_MODEL))
    return pl.pallas_call(
        _out_kernel,
        grid=(t // tm,),
        in_specs=[tok, tok, tok, tok, tok, sq, sq, sq, vec, _resident((D_MODEL, D_FF)),
                  _resident((D_MODEL, D_FF)), _resident((D_FF, D_MODEL)), vec],
        out_specs=tok,
        out_shape=jax.ShapeDtypeStruct((t, D_MODEL), F32),
        compiler_params=pltpu.CompilerParams(dimension_semantics=("arbitrary",),
                                             vmem_limit_bytes=VMEM_LIMIT_BYTES),
        name="merge_ffn2",
    )(att, hm, ga, gm, x1, watt, wml, wout, g2, wg, wu, wd, gfin)


def _rope_tables(seq):
    half = ATT_HEAD_DIM // 2
    pos = jnp.arange(seq, dtype=F32)
    inv_freq = ROPE_THETA ** (-jnp.arange(half, dtype=F32) / half)
    ang = pos[:, None] * inv_freq[None, :]
    cos, sin = jnp.cos(ang), jnp.sin(ang)
    reps = LANES // ATT_HEAD_DIM
    cos = jnp.tile(jnp.concatenate([cos, cos], axis=-1), (1, reps))
    sin = jnp.tile(jnp.concatenate([-sin, sin], axis=-1), (1, reps))
    return cos, sin


def _pad_lanes(v):
    return jnp.pad(v, ((0, 0), (0, LANES - v.shape[-1])))


def _layer(x2d, batch, seq, p):
    (ffn1_norm, ffn1_w_gate, ffn1_w_up, ffn1_w_down, mix_norm, w_in, b_i, b_f, attn_sinks, conv_w, conv_b,
     head_norm, w_att, w_mlstm, w_out, ffn2_norm, ffn2_w_gate, ffn2_w_up, ffn2_w_down, final_norm) = p
    tm = min(TOKEN_TILE, seq)
    assert seq % tm == 0 and seq % ATT_BLOCK == 0 and seq % MLSTM_CHUNK == 0

    perm = np.asarray(Q_HEAD_PERM)
    col_perm = (perm[:, None] * ATT_HEAD_DIM + np.arange(ATT_HEAD_DIM)[None, :]).reshape(-1)
    offs = np.cumsum((0, ATT_Q_W, ATT_KV_W, ATT_KV_W, ML_QK_W, ML_QK_W, ML_V_W, ML_V_W,
                      MLSTM_HEADS, MLSTM_HEADS, D_MODEL))
    w_q = w_in[:, offs[0]:offs[1]][:, col_perm]
    w_main = w_in[:, offs[1]:offs[7]]
    w_i = _pad_lanes(w_in[:, offs[7]:offs[8]])
    w_f = _pad_lanes(w_in[:, offs[8]:offs[9]])
    w_g = w_in[:, offs[9]:]
    w_in_r = jnp.concatenate([w_q, w_main, w_g, w_i, w_f], axis=1).astype(BF16)
    assert w_in_r.shape[1] == IN_W

    cos, sin = _rope_tables(seq)
    q_scale = ATT_HEAD_DIM ** -0.5
    sinks = jnp.broadcast_to(attn_sinks[perm][:, None], (ATT_HEADS, LANES)).astype(F32)

    x1 = _ffn1(x2d, ffn1_norm[None], ffn1_w_gate.astype(BF16), ffn1_w_up.astype(BF16),
               ffn1_w_down.astype(BF16), tm)
    att, qkm, vm, om, ga, gm, gi, gf = _inproj(
        x1, mix_norm[None], cos * q_scale, sin * q_scale, cos, sin, w_in_r, sinks, tm, seq)
    hm = _mlstm(qkm, vm, om, gi, gf, conv_w, conv_b[None], _pad_lanes(b_i[None]), _pad_lanes(b_f[None]),
                head_norm[None], batch, seq)
    return _out(att, hm, ga, gm, x1, w_att[col_perm, :].astype(BF16), w_mlstm.astype(BF16),
                w_out.astype(BF16), ffn2_norm[None], ffn2_w_gate.astype(BF16), ffn2_w_up.astype(BF16),
                ffn2_w_down.astype(BF16), final_norm[None], tm)


def kernel(x, ffn1_norm, ffn1_w_gate, ffn1_w_up, ffn1_w_down, mix_norm, w_in, b_i, b_f, attn_sinks, conv_w,
           conv_b, head_norm, w_att, w_mlstm, w_out, ffn2_norm, ffn2_w_gate, ffn2_w_up, ffn2_w_down, final_norm):
    batch, seq, d = x.shape
    assert d == D_MODEL and ffn1_norm.shape[0] == 1, "single-layer kernel"
    per_layer = (ffn1_norm, ffn1_w_gate, ffn1_w_up, ffn1_w_down, mix_norm, w_in, b_i, b_f, attn_sinks, conv_w,
                 conv_b, head_norm, w_att, w_mlstm, w_out, ffn2_norm, ffn2_w_gate, ffn2_w_up, ffn2_w_down)
    params = tuple(a[0] for a in per_layer) + (final_norm,)
    out = _layer(x.reshape(batch * seq, d), batch, seq, params)
    return out.reshape(batch, seq, d)
```

```python
import functools

import jax
import jax.numpy as jnp
import numpy as np
from jax import lax
from jax.experimental import pallas as pl
from jax.experimental.pallas import tpu as pltpu

F32 = jnp.float32
BF16 = jnp.bfloat16

D_MODEL = 1024
ATT_HEADS = 16
ATT_KV_HEADS = 4
ATT_HEAD_DIM = 64
ATT_BLOCK = 128
ROPE_THETA = 10000.0
MLSTM_HEADS = 4
MLSTM_V_DIM = D_MODEL // MLSTM_HEADS
MLSTM_QK_DIM = MLSTM_V_DIM // 2
MLSTM_CHUNK = 128
MLSTM_CONV = 4
D_FF = 2816
RMS_EPS = 1e-5

ATT_Q_W = ATT_HEADS * ATT_HEAD_DIM
ATT_KV_W = ATT_KV_HEADS * ATT_HEAD_DIM
ML_QK_W = MLSTM_HEADS * MLSTM_QK_DIM
ML_V_W = MLSTM_HEADS * MLSTM_V_DIM

LANES = 128
SUBLANES = 8
CONV_TAIL_ROWS = SUBLANES
VMEM_LIMIT_BYTES = 56 * 1024 * 1024

TOKEN_TILE = 512
FF_CHUNKS = ((0, 768), (768, 1536), (1536, 2304), (2304, D_FF))

_GROUP = ATT_HEADS // ATT_KV_HEADS
Q_HEAD_PERM = tuple(
    (2 * c + half) * _GROUP + r
    for c in range(ATT_KV_HEADS // 2) for r in range(_GROUP) for half in range(2))


def _resident(shape):
    return pl.BlockSpec(shape, lambda *_: (0,) * len(shape), pipeline_mode=pl.Buffered(1))


def _rms(x, g):
    ms = jnp.mean(x * x, axis=-1, keepdims=True)
    return x * lax.rsqrt(ms + RMS_EPS) * g


def _sigmoid(x):
    return 1.0 / (1.0 + jnp.exp(-x))


def _swiglu(h, wg_ref, wu_ref, wd_ref):
    acc = None
    for lo, hi in FF_CHUNKS:
        g = jnp.dot(h, wg_ref[:, lo:hi], preferred_element_type=F32)
        u = jnp.dot(h, wu_ref[:, lo:hi], preferred_element_type=F32)
        a = (g * _sigmoid(g) * u).astype(BF16)
        d = jnp.dot(a, wd_ref[lo:hi, :], preferred_element_type=F32)
        acc = d if acc is None else acc + d
    return acc


def _ffn1_kernel(x_ref, g_ref, wg_ref, wu_ref, wd_ref, o_ref):
    x = x_ref[...]
    h = _rms(x, g_ref[...]).astype(BF16)
    o_ref[...] = x + 0.5 * _swiglu(h, wg_ref, wu_ref, wd_ref)


def _ffn1(x, g, wg, wu, wd, tm):
    t = x.shape[0]
    tok = pl.BlockSpec((tm, D_MODEL), lambda i: (i, 0))
    return pl.pallas_call(
        _ffn1_kernel,
        grid=(t // tm,),
        in_specs=[tok, _resident((1, D_MODEL)), _resident((D_MODEL, D_FF)), _resident((D_MODEL, D_FF)),
                  _resident((D_FF, D_MODEL))],
        out_specs=tok,
        out_shape=jax.ShapeDtypeStruct((t, D_MODEL), F32),
        compiler_params=pltpu.CompilerParams(dimension_semantics=("arbitrary",),
                                             vmem_limit_bytes=VMEM_LIMIT_BYTES),
        name="ffn1",
    )(x, g, wg, wu, wd)


_C_QA = 0
_C_KA = _C_QA + ATT_Q_W
_C_VA = _C_KA + ATT_KV_W
_C_QKM = _C_VA + ATT_KV_W
_C_VM = _C_QKM + 2 * ML_QK_W
_C_OM = _C_VM + ML_V_W
_C_GA = _C_OM + ML_V_W
_C_GM = _C_GA + D_MODEL
_C_IF = _C_GM + D_MODEL
IN_W = _C_IF + 2 * LANES


def _rope(x, cos, sin_signed):
    lane = lax.broadcasted_iota(jnp.int32, x.shape, 1)
    half = ATT_HEAD_DIM // 2
    partner = jnp.where(lane % ATT_HEAD_DIM < half, pltpu.roll(x, LANES - half, axis=1),
                        pltpu.roll(x, half, axis=1))
    return x * cos + partner * sin_signed


def _attn_scores(q_blk, k_cur, k_prev):
    low = lax.broadcasted_iota(jnp.int32, (ATT_BLOCK, LANES), 1) < ATT_HEAD_DIM
    out = []
    for c in range(ATT_KV_W // LANES):
        ksl = slice(c * LANES, (c + 1) * LANES)
        kc = k_cur[:, ksl].astype(F32)
        kp = k_prev[:, ksl].astype(F32)
        zero = jnp.zeros_like(kc)
        kstack = jnp.concatenate(
            [jnp.where(low, kc, zero), jnp.where(low, kp, zero),
             jnp.where(low, zero, kc), jnp.where(low, zero, kp)], axis=0).astype(BF16)
        qstack = jnp.concatenate(
            [q_blk[:, (c * _GROUP + r) * LANES:(c * _GROUP + r + 1) * LANES] for r in range(_GROUP)], axis=0)
        out.append(lax.dot_general(kstack, qstack, (((1,), (1,)), ((), ())), preferred_element_type=F32))
    return out


def _attn_finish(st, v_cur, v_prev, has_prev, sink_ref, pt_scr, o_ref, rows):
    blk = ATT_BLOCK
    key = lax.broadcasted_iota(jnp.int32, (blk, blk), 0)
    qry = lax.broadcasted_iota(jnp.int32, (blk, blk), 1)
    in_cur = key <= qry
    neg_inf = jnp.float32(-jnp.inf)
    n_chunks = ATT_KV_W // LANES
    for c in range(n_chunks):
        for r in range(_GROUP):
            qs = slice(r * blk, (r + 1) * blk)
            for half in range(2):
                p_idx = (c * _GROUP + r) * 2 + half
                s_cur = st[c][(2 * half) * blk:(2 * half + 1) * blk, qs]
                s_prev = st[c][(2 * half + 1) * blk:(2 * half + 2) * blk, qs]
                s = jnp.where(in_cur, s_cur, jnp.where(has_prev, s_prev, neg_inf))
                sink = sink_ref[p_idx:p_idx + 1, :]
                m = jnp.maximum(jnp.max(s, axis=0, keepdims=True), sink)
                p = jnp.exp(s - m)
                denom = jnp.sum(p, axis=0, keepdims=True) + jnp.exp(sink - m)
                p = p * (1.0 / denom)
                zp = jnp.zeros_like(p)
                pt_scr[c * 2 + half, :blk, qs] = jnp.where(in_cur, p, zp).astype(BF16)
                pt_scr[c * 2 + half, blk:, qs] = jnp.where(in_cur, zp, p).astype(BF16)

    v_cat = jnp.concatenate([v_cur, v_prev], axis=0).astype(F32)
    vt_all = v_cat.T.astype(BF16)
    for c in range(n_chunks):
        ot = [jnp.dot(vt_all[(2 * c + half) * ATT_HEAD_DIM:(2 * c + half + 1) * ATT_HEAD_DIM, :],
                      pt_scr[c * 2 + half], preferred_element_type=F32) for half in range(2)]
        for r in range(_GROUP):
            j = c * _GROUP + r
            qs = slice(r * blk, (r + 1) * blk)
            o_t = jnp.concatenate([ot[0][:, qs], ot[1][:, qs]], axis=0)
            o_ref[rows, j * LANES:(j + 1) * LANES] = o_t.T.astype(BF16)


def _inproj_kernel(blocks_per_seq, x_ref, g_ref, cq_ref, sq_ref, ck_ref, sk_ref, w_ref, sink_ref,
                   att_ref, qkm_ref, vm_ref, om_ref, ga_ref, gm_ref, gi_ref, gf_ref, q_scr, kv_scr, pt_scr):
    tm = x_ref.shape[0]
    blk = ATT_BLOCK
    n_blk = tm // blk
    step = pl.program_id(0)
    slot = step % 2
    q_new, kv_new = q_scr.at[slot], kv_scr.at[slot]
    q_old, kv_old = q_scr.at[1 - slot], kv_scr.at[1 - slot]

    @pl.when(step == 0)
    def _():
        q_scr[...] = jnp.zeros_like(q_scr)
        kv_scr[...] = jnp.zeros_like(kv_scr)

    h = _rms(x_ref[...], g_ref[...]).astype(BF16)
    kv_new[:blk, :] = kv_old[tm:, :]

    def proj(lo, width):
        return jnp.dot(h, w_ref[:, lo:lo + width], preferred_element_type=F32)

    def proj_q():
        z = proj(_C_QA, ATT_Q_W)
        cq, sq = cq_ref[...], sq_ref[...]
        for j in range(ATT_Q_W // LANES):
            sl = slice(j * LANES, (j + 1) * LANES)
            q_new[:, sl] = _rope(z[:, sl], cq, sq).astype(BF16)

    def proj_kv_vm():
        z = proj(_C_KA, 2 * ATT_KV_W)
        ck, sk = ck_ref[...], sk_ref[...]
        for j in range(ATT_KV_W // LANES):
            sl = slice(j * LANES, (j + 1) * LANES)
            kv_new[blk:, sl] = _rope(z[:, sl], ck, sk).astype(BF16)
        kv_new[blk:, ATT_KV_W:] = z[:, ATT_KV_W:].astype(BF16)
        vm_ref[...] = proj(_C_VM, ML_V_W).astype(BF16)

    def proj_om_ga():
        om_ref[...] = proj(_C_OM, ML_V_W).astype(BF16)
        ga_ref[...] = proj(_C_GA, D_MODEL).astype(BF16)

    def proj_gm_if_qkm():
        gm_ref[...] = proj(_C_GM, D_MODEL).astype(BF16)
        z = proj(_C_IF, 2 * LANES)
        gi_ref[...] = z[:, :LANES]
        gf_ref[...] = z[:, LANES:]
        qkm_ref[...] = proj(_C_QKM, 2 * ML_QK_W).astype(BF16)

    groups = (proj_q, proj_kv_vm, proj_om_ga, proj_gm_if_qkm)
    for k in range(n_blk):
        rows = slice(k * blk, (k + 1) * blk)
        kv_c = kv_old[blk + k * blk:2 * blk + k * blk, :]
        kv_p = kv_old[k * blk:(k + 1) * blk, :]
        st = _attn_scores(q_old[rows, :], kv_c[:, :ATT_KV_W], kv_p[:, :ATT_KV_W])
        for group in groups[k * len(groups) // n_blk:(k + 1) * len(groups) // n_blk]:
            group()
        has_prev = ((step - 1) * n_blk + k) % blocks_per_seq != 0
        _attn_finish(st, kv_c[:, ATT_KV_W:], kv_p[:, ATT_KV_W:], has_prev, sink_ref, pt_scr, att_ref, rows)


def _inproj(x1, g, cq, sq, ck, sk, w, sinks, tm, seq):
    t = x1.shape[0]
    n = t // tm
    tiles_per_seq = seq // tm

    def cur(i):
        return jnp.minimum(i, n - 1)

    def tok(width):
        return pl.BlockSpec((tm, width), lambda i: (cur(i), 0))

    pos = pl.BlockSpec((tm, LANES), lambda i: (cur(i) % tiles_per_seq, 0))
    lagged = pl.BlockSpec((tm, ATT_Q_W), lambda i: (jnp.maximum(i - 1, 0), 0))
    widths = (2 * ML_QK_W, ML_V_W, ML_V_W, D_MODEL, D_MODEL)
    out_shape = [jax.ShapeDtypeStruct((t, ATT_Q_W), BF16)]
    out_shape += [jax.ShapeDtypeStruct((t, w_), BF16) for w_ in widths]
    out_shape += [jax.ShapeDtypeStruct((t, LANES), F32)] * 2
    return pl.pallas_call(
        functools.partial(_inproj_kernel, seq // ATT_BLOCK),
        grid=(n + 1,),
        in_specs=[tok(D_MODEL), _resident((1, D_MODEL)), pos, pos, pos, pos, _resident((D_MODEL, IN_W)),
                  _resident((ATT_HEADS, LANES))],
        out_specs=[lagged] + [tok(w_) for w_ in widths] + [tok(LANES)] * 2,
        out_shape=out_shape,
        scratch_shapes=[pltpu.VMEM((2, tm, ATT_Q_W), BF16),
                        pltpu.VMEM((2, ATT_BLOCK + tm, 2 * ATT_KV_W), BF16),
                        pltpu.VMEM((ATT_KV_HEADS, 2 * ATT_BLOCK, _GROUP * ATT_BLOCK), BF16)],
        compiler_params=pltpu.CompilerParams(dimension_semantics=("arbitrary",),
                                             vmem_limit_bytes=VMEM_LIMIT_BYTES),
        name="in_proj_attn",
    )(x1, g, cq, sq, ck, sk, w, sinks)


def _scan_rows(x, op, fill):
    n = x.shape[0]
    row = lax.broadcasted_iota(jnp.int32, x.shape, 0)
    shift = 1
    while shift < n:
        prev = jnp.where(row >= shift, pltpu.roll(x, shift, axis=0), fill)
        x = op(x, prev)
        shift *= 2
    return x


def _mlstm_prologue(rows, qk_ref, gi_ref, gf_ref, cw_ref, cb_ref, bi_ref, bf_ref, m_scr, conv_scr):
    L = MLSTM_CHUNK
    tail = CONV_TAIL_ROWS
    x = qk_ref[rows, :].astype(F32)
    conv_scr[tail:, :] = x
    y = cb_ref[...] + cw_ref[MLSTM_CONV - 1:MLSTM_CONV, :] * x
    for j in range(1, MLSTM_CONV):
        y = y + cw_ref[MLSTM_CONV - 1 - j:MLSTM_CONV - j, :] * conv_scr[tail - j:tail - j + L, :]
    conv_scr[:tail, :] = x[L - tail:, :]
    qk = y * _sigmoid(y)

    ig = gi_ref[rows, :] + bi_ref[...]
    fpre = gf_ref[rows, :] + bf_ref[...]
    logf = jnp.minimum(fpre, 0.0) - jnp.log1p(jnp.exp(-jnp.abs(fpre)))
    b = _scan_rows(logf, jnp.add, 0.0)
    g = ig - b
    cm = _scan_rows(g, jnp.maximum, -jnp.inf)
    m_prev = m_scr[0:1, :]
    u = jnp.maximum(cm, m_prev)
    g_max = cm[L - 1:L, :]
    u_last = u[L - 1:L, :]
    gates = dict(u=u, a=jnp.exp(g - g_max), inter=jnp.exp(m_prev - u), emt=jnp.exp(-(b + u)),
                 s_old=jnp.exp(m_prev - u_last), s_new=jnp.exp(g_max - u_last),
                 g_rows=g.T)
    m_scr[0:1, :] = b[L - 1:L, :] + u_last
    return qk, gates


def _mlstm_head(h, rows, qk, gates, v_ref, o_ref, hn_ref, out_ref, ct_scr, n_scr):
    L = MLSTM_CHUNK
    row = lax.broadcasted_iota(jnp.int32, (L, L), 0)
    col = lax.broadcasted_iota(jnp.int32, (L, L), 1)
    causal = col <= row
    qs = slice(h * MLSTM_QK_DIM, (h + 1) * MLSTM_QK_DIM)
    ks = slice(ML_QK_W + h * MLSTM_QK_DIM, ML_QK_W + (h + 1) * MLSTM_QK_DIM)
    vs = slice(h * MLSTM_V_DIM, (h + 1) * MLSTM_V_DIM)
    qh = qk[:, qs]
    kh = qk[:, ks] * (MLSTM_QK_DIM ** -0.5)
    qb = qh.astype(BF16)
    kb = kh.astype(BF16)
    vb = v_ref[rows, vs]
    inter_col = gates["inter"][:, h:h + 1]
    a_col = gates["a"][:, h:h + 1]

    decay = jnp.where(causal, jnp.exp(gates["g_rows"][h:h + 1, :] - gates["u"][:, h:h + 1]), 0.0)
    s = lax.dot_general(qb, kb, (((1,), (1,)), ((), ())), preferred_element_type=F32) * decay
    ct = ct_scr[h]
    n_row = n_scr[h:h + 1, :]
    num = (jnp.dot(s.astype(BF16), vb, preferred_element_type=F32)
           + inter_col * jnp.dot(qb, ct.astype(BF16), preferred_element_type=F32))
    den = jnp.sum(s + inter_col * (qh * n_row), axis=-1, keepdims=True)
    hh = num * (1.0 / jnp.maximum(jnp.abs(den), gates["emt"][:, h:h + 1]))
    hn = _rms(hh, hn_ref[:, vs])
    out_ref[rows, vs] = (_sigmoid(o_ref[rows, vs].astype(F32)) * hn).astype(BF16)

    av = (a_col * vb.astype(F32)).astype(BF16)
    d_ct = lax.dot_general(kb, av, (((0,), (0,)), ((), ())), preferred_element_type=F32)
    d_n = jnp.sum(a_col * kh, axis=0, keepdims=True)
    so = gates["s_old"][:, h:h + 1]
    sn = gates["s_new"][:, h:h + 1]
    ct_scr[h] = so * ct + sn * d_ct
    n_scr[h:h + 1, :] = so * n_row + sn * d_n


def _out_kernel(tiles_per_seq, n_tiles, att_ref, ga_ref, gm_ref, x1_ref, qk_ref, v_ref, o_ref, gi_ref, gf_ref,
                cw_ref, cb_ref, bi_ref, bf_ref, hn_ref, watt_ref, wml_ref, wout_ref, g2_ref, wg_ref, wu_ref,
                wd_ref, gfin_ref, out_ref, hm_scr, ct_scr, n_scr, m_scr, conv_scr):
    tm = x1_ref.shape[0]
    L = MLSTM_CHUNK
    step = pl.program_id(0)
    slot = step % 2
    hm_new, hm_old = hm_scr.at[slot], hm_scr.at[1 - slot]

    @pl.when(step == 0)
    def _():
        hm_scr[...] = jnp.zeros_like(hm_scr)

    @pl.when(jnp.minimum(step, n_tiles - 1) % tiles_per_seq == 0)
    def _():
        ct_scr[...] = jnp.zeros_like(ct_scr)
        n_scr[...] = jnp.zeros_like(n_scr)
        m_scr[...] = jnp.zeros_like(m_scr)
        conv_scr[:CONV_TAIL_ROWS, :] = jnp.zeros((CONV_TAIL_ROWS, conv_scr.shape[1]), F32)

    d = {}

    def p_att():
        d["ya"] = jnp.dot(att_ref[...], watt_ref[...], preferred_element_type=F32)

    def p_mlstm():
        d["ym"] = jnp.dot(hm_old[...], wml_ref[...], preferred_element_type=F32)

    def p_out():
        y = _sigmoid(ga_ref[...].astype(F32)) * d.pop("ya") + _sigmoid(gm_ref[...].astype(F32)) * d.pop("ym")
        d["x2"] = x1_ref[...] + jnp.dot(y.astype(BF16), wout_ref[...], preferred_element_type=F32)
        d["h2"] = _rms(d["x2"], g2_ref[...]).astype(BF16)

    def p_gate(lo, hi):
        d["g"] = jnp.dot(d["h2"], wg_ref[:, lo:hi], preferred_element_type=F32)

    def p_up(lo, hi):
        u = jnp.dot(d["h2"], wu_ref[:, lo:hi], preferred_element_type=F32)
        g = d.pop("g")
        d["a"] = (g * _sigmoid(g) * u).astype(BF16)

    def p_down(lo, hi):
        dn = jnp.dot(d.pop("a"), wd_ref[lo:hi, :], preferred_element_type=F32)
        d["acc"] = dn if "acc" not in d else d["acc"] + dn

    def p_final():
        x3 = d.pop("x2") + 0.5 * d.pop("acc")
        out_ref[...] = _rms(x3, gfin_ref[...])

    pieces = [p_att, p_mlstm, p_out]
    for lo, hi in FF_CHUNKS:
        pieces += [functools.partial(p_gate, lo, hi), functools.partial(p_up, lo, hi),
                   functools.partial(p_down, lo, hi)]
    pieces.append(p_final)
    pieces.reverse()

    def next_piece():
        if pieces:
            pieces.pop()()

    for c in range(tm // L):
        rows = slice(c * L, (c + 1) * L)
        qk, gates = _mlstm_prologue(rows, qk_ref, gi_ref, gf_ref, cw_ref, cb_ref, bi_ref, bf_ref, m_scr, conv_scr)
        next_piece()
        for h in range(MLSTM_HEADS):
            _mlstm_head(h, rows, qk, gates, v_ref, o_ref, hn_ref, hm_new, ct_scr, n_scr)
            if h % 2 == 1:
                next_piece()
    while pieces:
        next_piece()


def _out(att, ga, gm, x1, qkm, vm, om, gi, gf, cw, cb, bi, bf, hn, watt, wml, wout, g2, wg, wu, wd, gfin, tm, seq):
    t = x1.shape[0]
    n = t // tm

    def cur(width):
        return pl.BlockSpec((tm, width), lambda i: (jnp.minimum(i, n - 1), 0))

    lagged = pl.BlockSpec((tm, D_MODEL), lambda i: (jnp.maximum(i - 1, 0), 0))
    sq = _resident((D_MODEL, D_MODEL))
    vec = _resident((1, D_MODEL))
    return pl.pallas_call(
        functools.partial(_out_kernel, seq // tm, n),
        grid=(n + 1,),
        in_specs=[lagged, lagged, lagged, lagged,
                  cur(2 * ML_QK_W), cur(ML_V_W), cur(ML_V_W), cur(LANES), cur(LANES),
                  _resident((MLSTM_CONV, 2 * ML_QK_W)), _resident((1, 2 * ML_QK_W)),
                  _resident((1, LANES)), _resident((1, LANES)), vec,
                  sq, sq, sq, vec, _resident((D_MODEL, D_FF)), _resident((D_MODEL, D_FF)),
                  _resident((D_FF, D_MODEL)), vec],
        out_specs=lagged,
        out_shape=jax.ShapeDtypeStruct((t, D_MODEL), F32),
        scratch_shapes=[pltpu.VMEM((2, tm, ML_V_W), BF16),
                        pltpu.VMEM((MLSTM_HEADS, MLSTM_QK_DIM, MLSTM_V_DIM), F32),
                        pltpu.VMEM((SUBLANES, LANES), F32),
                        pltpu.VMEM((SUBLANES, LANES), F32),
                        pltpu.VMEM((CONV_TAIL_ROWS + MLSTM_CHUNK, 2 * ML_QK_W), F32)],
        compiler_params=pltpu.CompilerParams(dimension_semantics=("arbitrary",),
                                             vmem_limit_bytes=VMEM_LIMIT_BYTES),
        name="mlstm_merge_ffn2",
    )(att, ga, gm, x1, qkm, vm, om, gi, gf, cw, cb, bi, bf, hn, watt, wml, wout, g2, wg, wu, wd, gfin)


def _rope_tables(seq):
    half = ATT_HEAD_DIM // 2
    pos = jnp.arange(seq, dtype=F32)
    inv_freq = ROPE_THETA ** (-jnp.arange(half, dtype=F32) / half)
    ang = pos[:, None] * inv_freq[None, :]
    cos, sin = jnp.cos(ang), jnp.sin(ang)
    reps = LANES // ATT_HEAD_DIM
    cos = jnp.tile(jnp.concatenate([cos, cos], axis=-1), (1, reps))
    sin = jnp.tile(jnp.concatenate([-sin, sin], axis=-1), (1, reps))
    return cos, sin


def _pad_lanes(v):
    return jnp.pad(v, ((0, 0), (0, LANES - v.shape[-1])))


def _layer(x2d, batch, seq, p):
    (ffn1_norm, ffn1_w_gate, ffn1_w_up, ffn1_w_down, mix_norm, w_in, b_i, b_f, attn_sinks, conv_w, conv_b,
     head_norm, w_att, w_mlstm, w_out, ffn2_norm, ffn2_w_gate, ffn2_w_up, ffn2_w_down, final_norm) = p
    tm = min(TOKEN_TILE, seq)
    assert seq % tm == 0 and seq % ATT_BLOCK == 0 and seq % MLSTM_CHUNK == 0

    perm = np.asarray(Q_HEAD_PERM)
    col_perm = (perm[:, None] * ATT_HEAD_DIM + np.arange(ATT_HEAD_DIM)[None, :]).reshape(-1)
    offs = np.cumsum((0, ATT_Q_W, ATT_KV_W, ATT_KV_W, ML_QK_W, ML_QK_W, ML_V_W, ML_V_W,
                      MLSTM_HEADS, MLSTM_HEADS, D_MODEL))
    w_q = w_in[:, offs[0]:offs[1]][:, col_perm]
    w_main = w_in[:, offs[1]:offs[7]]
    w_i = _pad_lanes(w_in[:, offs[7]:offs[8]])
    w_f = _pad_lanes(w_in[:, offs[8]:offs[9]])
    w_g = w_in[:, offs[9]:]
    w_in_r = jnp.concatenate([w_q, w_main, w_g, w_i, w_f], axis=1).astype(BF16)
    assert w_in_r.shape[1] == IN_W

    cos, sin = _rope_tables(seq)
    q_scale = ATT_HEAD_DIM ** -0.5
    sinks = jnp.broadcast_to(attn_sinks[perm][:, None], (ATT_HEADS, LANES)).astype(F32)

    x1 = _ffn1(x2d, ffn1_norm[None], ffn1_w_gate.astype(BF16), ffn1_w_up.astype(BF16),
               ffn1_w_down.astype(BF16), tm)
    att, qkm, vm, om, ga, gm, gi, gf = _inproj(
        x1, mix_norm[None], cos * q_scale, sin * q_scale, cos, sin, w_in_r, sinks, tm, seq)
    return _out(att, ga, gm, x1, qkm, vm, om, gi, gf, conv_w, conv_b[None], _pad_lanes(b_i[None]),
                _pad_lanes(b_f[None]), head_norm[None], w_att[col_perm, :].astype(BF16), w_mlstm.astype(BF16),
                w_out.astype(BF16), ffn2_norm[None], ffn2_w_gate.astype(BF16), ffn2_w_up.astype(BF16),
                ffn2_w_down.astype(BF16), final_norm[None], tm, seq)


def kernel(x, ffn1_norm, ffn1_w_gate, ffn1_w_up, ffn1_w_down, mix_norm, w_in, b_i, b_f, attn_sinks, conv_w,
           conv_b, head_norm, w_att, w_mlstm, w_out, ffn2_norm, ffn2_w_gate, ffn2_w_up, ffn2_w_down, final_norm):
    batch, seq, d = x.shape
    assert d == D_MODEL and ffn1_norm.shape[0] == 1, "single-layer kernel"
    per_layer = (ffn1_norm, ffn1_w_gate, ffn1_w_up, ffn1_w_down, mix_norm, w_in, b_i, b_f, attn_sinks, conv_w,
                 conv_b, head_norm, w_att, w_mlstm, w_out, ffn2_norm, ffn2_w_gate, ffn2_w_up, ffn2_w_down)
    params = tuple(a[0] for a in per_layer) + (final_norm,)
    out = _layer(x.reshape(batch * seq, d), batch, seq, params)
    return out.reshape(batch, seq, d)
```

```python
import functools

import jax
import jax.numpy as jnp
import numpy as np
from jax import lax
from jax.experimental import pallas as pl
from jax.experimental.pallas import tpu as pltpu

F32 = jnp.float32
BF16 = jnp.bfloat16

D_MODEL = 1024
ATT_HEADS = 16
ATT_KV_HEADS = 4
ATT_HEAD_DIM = 64
ATT_BLOCK = 128
ROPE_THETA = 10000.0
MLSTM_HEADS = 4
MLSTM_V_DIM = D_MODEL // MLSTM_HEADS
MLSTM_QK_DIM = MLSTM_V_DIM // 2
MLSTM_CHUNK = 128
MLSTM_CONV = 4
D_FF = 2816
RMS_EPS = 1e-5

ATT_Q_W = ATT_HEADS * ATT_HEAD_DIM
ATT_KV_W = ATT_KV_HEADS * ATT_HEAD_DIM
ML_QK_W = MLSTM_HEADS * MLSTM_QK_DIM
ML_V_W = MLSTM_HEADS * MLSTM_V_DIM

LANES = 128
MXU_COLS = 256
SUBLANES = 8
CONV_TAIL_ROWS = SUBLANES
VMEM_LIMIT_BYTES = 60 * 1024 * 1024

TOKEN_TILE = 512
FF_CHUNKS = ((0, 768), (768, 1536), (1536, 2304), (2304, D_FF))

_GROUP = ATT_HEADS // ATT_KV_HEADS
Q_HEAD_PERM = tuple(
    (2 * c + half) * _GROUP + r
    for c in range(ATT_KV_HEADS // 2) for r in range(_GROUP) for half in range(2))


def _resident(shape):
    return pl.BlockSpec(shape, lambda *_: (0,) * len(shape), pipeline_mode=pl.Buffered(1))


def _col_groups(lo, hi):
    return [(c, min(c + MXU_COLS, hi)) for c in range(lo, hi, MXU_COLS)]


def _rms(x, g):
    ms = jnp.mean(x * x, axis=-1, keepdims=True)
    return x * lax.rsqrt(ms + RMS_EPS) * g


def _sigmoid(x):
    return 1.0 / (1.0 + jnp.exp(-x))


def _swiglu(h, wg_ref, wu_ref, wd_ref):
    acc = None
    for lo, hi in FF_CHUNKS:
        g = jnp.dot(h, wg_ref[:, lo:hi], preferred_element_type=F32)
        u = jnp.dot(h, wu_ref[:, lo:hi], preferred_element_type=F32)
        a = (g * _sigmoid(g) * u).astype(BF16)
        d = jnp.dot(a, wd_ref[lo:hi, :], preferred_element_type=F32)
        acc = d if acc is None else acc + d
    return acc


def _ffn1_kernel(x_ref, g_ref, wg_ref, wu_ref, wd_ref, o_ref):
    x = x_ref[...]
    h = _rms(x, g_ref[...]).astype(BF16)
    o_ref[...] = x + 0.5 * _swiglu(h, wg_ref, wu_ref, wd_ref)


def _ffn1(x, g, wg, wu, wd, tm):
    t = x.shape[0]
    tok = pl.BlockSpec((tm, D_MODEL), lambda i: (i, 0))
    return pl.pallas_call(
        _ffn1_kernel,
        grid=(t // tm,),
        in_specs=[tok, _resident((1, D_MODEL)), _resident((D_MODEL, D_FF)), _resident((D_MODEL, D_FF)),
                  _resident((D_FF, D_MODEL))],
        out_specs=tok,
        out_shape=jax.ShapeDtypeStruct((t, D_MODEL), F32),
        compiler_params=pltpu.CompilerParams(dimension_semantics=("arbitrary",),
                                             vmem_limit_bytes=VMEM_LIMIT_BYTES),
        name="ffn1",
    )(x, g, wg, wu, wd)


_C_KA = 0
_C_VA = _C_KA + ATT_KV_W
_C_QKM = _C_VA + ATT_KV_W
_C_VM = _C_QKM + 2 * ML_QK_W
_C_OM = _C_VM + ML_V_W
W_MID = _C_OM + ML_V_W


def _rope(x, cos, sin_signed):
    lane = lax.broadcasted_iota(jnp.int32, x.shape, 1)
    half = ATT_HEAD_DIM // 2
    partner = jnp.where(lane % ATT_HEAD_DIM < half, pltpu.roll(x, LANES - half, axis=1),
                        pltpu.roll(x, half, axis=1))
    return x * cos + partner * sin_signed


def _attn_scores(c, q_blk, k_cur, k_prev):
    low = lax.broadcasted_iota(jnp.int32, (ATT_BLOCK, LANES), 1) < ATT_HEAD_DIM
    ksl = slice(c * LANES, (c + 1) * LANES)
    kc = k_cur[:, ksl].astype(F32)
    kp = k_prev[:, ksl].astype(F32)
    zero = jnp.zeros_like(kc)
    kstack = jnp.concatenate(
        [jnp.where(low, kc, zero), jnp.where(low, kp, zero),
         jnp.where(low, zero, kc), jnp.where(low, zero, kp)], axis=0).astype(BF16)
    qstack = jnp.concatenate(
        [q_blk[:, (c * _GROUP + r) * LANES:(c * _GROUP + r + 1) * LANES] for r in range(_GROUP)], axis=0)
    return lax.dot_general(kstack, qstack, (((1,), (1,)), ((), ())), preferred_element_type=F32)


def _attn_block(q_blk, kv_cur, kv_prev, has_prev, sink_ref, pt_scr, o_ref, rows):
    blk = ATT_BLOCK
    key = lax.broadcasted_iota(jnp.int32, (blk, blk), 0)
    qry = lax.broadcasted_iota(jnp.int32, (blk, blk), 1)
    in_cur = key <= qry
    neg_inf = jnp.float32(-jnp.inf)
    n_chunks = ATT_KV_W // LANES
    st = []
    for c in range(n_chunks):
        st.append(_attn_scores(c, q_blk, kv_cur[:, :ATT_KV_W], kv_prev[:, :ATT_KV_W]))
        yield
    v_cat = jnp.concatenate([kv_cur[:, ATT_KV_W:], kv_prev[:, ATT_KV_W:]], axis=0).astype(F32)
    vt_all = v_cat.T.astype(BF16)
    for c in range(n_chunks):
        for r in range(_GROUP):
            qs = slice(r * blk, (r + 1) * blk)
            for half in range(2):
                p_idx = (c * _GROUP + r) * 2 + half
                s_cur = st[c][(2 * half) * blk:(2 * half + 1) * blk, qs]
                s_prev = st[c][(2 * half + 1) * blk:(2 * half + 2) * blk, qs]
                s = jnp.where(in_cur, s_cur, jnp.where(has_prev, s_prev, neg_inf))
                sink = sink_ref[p_idx:p_idx + 1, :]
                m = jnp.maximum(jnp.max(s, axis=0, keepdims=True), sink)
                p = jnp.exp(s - m)
                denom = jnp.sum(p, axis=0, keepdims=True) + jnp.exp(sink - m)
                p = p * (1.0 / denom)
                zp = jnp.zeros_like(p)
                pt_scr[c * 2 + half, :blk, qs] = jnp.where(in_cur, p, zp).astype(BF16)
                pt_scr[c * 2 + half, blk:, qs] = jnp.where(in_cur, zp, p).astype(BF16)
            if r % 2 == 1:
                yield
        ot = [jnp.dot(vt_all[(2 * c + half) * ATT_HEAD_DIM:(2 * c + half + 1) * ATT_HEAD_DIM, :],
                      pt_scr[c * 2 + half], preferred_element_type=F32) for half in range(2)]
        for r in range(_GROUP):
            j = c * _GROUP + r
            qs = slice(r * blk, (r + 1) * blk)
            o_t = jnp.concatenate([ot[0][:, qs], ot[1][:, qs]], axis=0)
            o_ref[rows, j * LANES:(j + 1) * LANES] = o_t.T.astype(BF16)
        yield


def _inproj_kernel(blocks_per_seq, x_ref, g_ref, cq_ref, sq_ref, ck_ref, sk_ref, wq_ref, wmid_ref, wgate_ref, wif_ref,
                   sink_ref,
                   att_ref, qkm_ref, vm_ref, om_ref, ga_ref, gm_ref, gi_ref, gf_ref, q_scr, kv_scr, pt_scr):
    tm = x_ref.shape[0]
    blk = ATT_BLOCK
    step = pl.program_id(0)
    slot = step % 2
    q_new, kv_new = q_scr.at[slot], kv_scr.at[slot]
    q_old, kv_old = q_scr.at[1 - slot], kv_scr.at[1 - slot]

    @pl.when(step == 0)
    def _():
        q_scr[...] = jnp.zeros_like(q_scr)
        kv_scr[...] = jnp.zeros_like(kv_scr)

    h = _rms(x_ref[...], g_ref[...]).astype(BF16)
    kv_new[:blk, :] = kv_old[tm:, :]

    def mixer():
        for k in range(tm // blk):
            rows = slice(k * blk, (k + 1) * blk)
            has_prev = ((step - 1) * (tm // blk) + k) % blocks_per_seq != 0
            yield from _attn_block(q_old[rows, :], kv_old[blk + k * blk:2 * blk + k * blk, :],
                                   kv_old[k * blk:(k + 1) * blk, :], has_prev, sink_ref, pt_scr, att_ref, rows)

    def dense():
        def proj(w_ref, lo, hi):
            return jnp.dot(h, w_ref[:, lo:hi], preferred_element_type=F32)

        def roped(z, cos, sin):
            return [_rope(z[:, j * LANES:(j + 1) * LANES], cos, sin).astype(BF16) for j in range(MXU_COLS // LANES)]

        cq, sq = cq_ref[...], sq_ref[...]
        for lo, hi in _col_groups(0, ATT_Q_W):
            q_new[:, lo:hi] = jnp.concatenate(roped(proj(wq_ref, lo, hi), cq, sq), axis=1)
            yield
        for lo, hi in _col_groups(0, ATT_KV_W):
            kv_new[blk:, lo:hi] = jnp.concatenate(roped(proj(wmid_ref, _C_KA + lo, _C_KA + hi), ck_ref[...], sk_ref[...]), axis=1)
            yield
        for lo, hi in _col_groups(0, ATT_KV_W):
            kv_new[blk:, ATT_KV_W + lo:ATT_KV_W + hi] = proj(wmid_ref, _C_VA + lo, _C_VA + hi).astype(BF16)
            yield
        for w_ref, base, ref, width in ((wmid_ref, _C_VM, vm_ref, ML_V_W), (wmid_ref, _C_OM, om_ref, ML_V_W),
                                        (wgate_ref, 0, ga_ref, D_MODEL), (wgate_ref, D_MODEL, gm_ref, D_MODEL),
                                        (wmid_ref, _C_QKM, qkm_ref, 2 * ML_QK_W)):
            for lo, hi in _col_groups(0, width):
                ref[:, lo:hi] = proj(w_ref, base + lo, base + hi).astype(BF16)
                yield
        z = proj(wif_ref, 0, 2 * LANES)
        gi_ref[...] = z[:, :LANES]
        gf_ref[...] = z[:, LANES:]
        yield

    streams = [mixer(), dense()]
    while streams:
        streams = [s for s in streams if next(s, True) is None]


def _inproj(x1, g, cq, sq, ck, sk, wq, wmid, wgate, wif, sinks, tm, seq):
    t = x1.shape[0]
    n = t // tm
    tiles_per_seq = seq // tm

    def cur(i):
        return jnp.minimum(i, n - 1)

    def tok(width):
        return pl.BlockSpec((tm, width), lambda i: (cur(i), 0))

    pos = pl.BlockSpec((tm, LANES), lambda i: (cur(i) % tiles_per_seq, 0))
    lagged = pl.BlockSpec((tm, ATT_Q_W), lambda i: (jnp.maximum(i - 1, 0), 0))
    widths = (2 * ML_QK_W, ML_V_W, ML_V_W, D_MODEL, D_MODEL)
    out_shape = [jax.ShapeDtypeStruct((t, ATT_Q_W), BF16)]
    out_shape += [jax.ShapeDtypeStruct((t, w_), BF16) for w_ in widths]
    out_shape += [jax.ShapeDtypeStruct((t, LANES), F32)] * 2
    return pl.pallas_call(
        functools.partial(_inproj_kernel, seq // ATT_BLOCK),
        grid=(n + 1,),
        in_specs=[tok(D_MODEL), _resident((1, D_MODEL)), pos, pos, pos, pos, _resident((D_MODEL, ATT_Q_W)),
                  _resident((D_MODEL, W_MID)), _resident((D_MODEL, 2 * D_MODEL)), _resident((D_MODEL, 2 * LANES)),
                  _resident((ATT_HEADS, LANES))],
        out_specs=[lagged] + [tok(w_) for w_ in widths] + [tok(LANES)] * 2,
        out_shape=out_shape,
        scratch_shapes=[pltpu.VMEM((2, tm, ATT_Q_W), BF16),
                        pltpu.VMEM((2, ATT_BLOCK + tm, 2 * ATT_KV_W), BF16),
                        pltpu.VMEM((ATT_KV_HEADS, 2 * ATT_BLOCK, _GROUP * ATT_BLOCK), BF16)],
        compiler_params=pltpu.CompilerParams(dimension_semantics=("arbitrary",),
                                             vmem_limit_bytes=VMEM_LIMIT_BYTES),
        name="in_proj_attn",
    )(x1, g, cq, sq, ck, sk, wq, wmid, wgate, wif, sinks)


def _scan_rows(x, op, fill):
    n = x.shape[0]
    row = lax.broadcasted_iota(jnp.int32, x.shape, 0)
    shift = 1
    while shift < n:
        prev = jnp.where(row >= shift, pltpu.roll(x, shift, axis=0), fill)
        x = op(x, prev)
        shift *= 2
    return x


def _mlstm_prologue(rows, qk_ref, gi_ref, gf_ref, cw_ref, cb_ref, bi_ref, bf_ref, m_scr, conv_scr):
    L = MLSTM_CHUNK
    tail = CONV_TAIL_ROWS
    x = qk_ref[rows, :].astype(F32)
    conv_scr[tail:, :] = x
    y = cb_ref[...] + cw_ref[MLSTM_CONV - 1:MLSTM_CONV, :] * x
    for j in range(1, MLSTM_CONV):
        y = y + cw_ref[MLSTM_CONV - 1 - j:MLSTM_CONV - j, :] * conv_scr[tail - j:tail - j + L, :]
    conv_scr[:tail, :] = x[L - tail:, :]
    qk = y * _sigmoid(y)

    ig = gi_ref[rows, :] + bi_ref[...]
    fpre = gf_ref[rows, :] + bf_ref[...]
    logf = jnp.minimum(fpre, 0.0) - jnp.log1p(jnp.exp(-jnp.abs(fpre)))
    b = _scan_rows(logf, jnp.add, 0.0)
    g = ig - b
    cm = _scan_rows(g, jnp.maximum, -jnp.inf)
    m_prev = m_scr[0:1, :]
    u = jnp.maximum(cm, m_prev)
    g_max = cm[L - 1:L, :]
    u_last = u[L - 1:L, :]
    gates = dict(u=u, a=jnp.exp(g - g_max), inter=jnp.exp(m_prev - u), emt=jnp.exp(-(b + u)),
                 s_old=jnp.exp(m_prev - u_last), s_new=jnp.exp(g_max - u_last),
                 g_rows=g.T)
    m_scr[0:1, :] = b[L - 1:L, :] + u_last
    return qk, gates


def _mlstm_head(h, rows, qk, gates, v_ref, o_ref, hn_ref, out_ref, ct_scr, n_scr):
    L = MLSTM_CHUNK
    row = lax.broadcasted_iota(jnp.int32, (L, L), 0)
    col = lax.broadcasted_iota(jnp.int32, (L, L), 1)
    causal = col <= row
    qs = slice(h * MLSTM_QK_DIM, (h + 1) * MLSTM_QK_DIM)
    ks = slice(ML_QK_W + h * MLSTM_QK_DIM, ML_QK_W + (h + 1) * MLSTM_QK_DIM)
    vs = slice(h * MLSTM_V_DIM, (h + 1) * MLSTM_V_DIM)
    qh = qk[:, qs]
    kh = qk[:, ks] * (MLSTM_QK_DIM ** -0.5)
    qb = qh.astype(BF16)
    kb = kh.astype(BF16)
    vb = v_ref[rows, vs]
    inter_col = gates["inter"][:, h:h + 1]
    a_col = gates["a"][:, h:h + 1]

    decay = jnp.where(causal, jnp.exp(gates["g_rows"][h:h + 1, :] - gates["u"][:, h:h + 1]), 0.0)
    s = lax.dot_general(qb, kb, (((1,), (1,)), ((), ())), preferred_element_type=F32) * decay
    ct = ct_scr[h]
    n_row = n_scr[h:h + 1, :]
    yield
    num = (jnp.dot(s.astype(BF16), vb, preferred_element_type=F32)
           + inter_col * jnp.dot(qb, ct.astype(BF16), preferred_element_type=F32))
    den = jnp.sum(s + inter_col * (qh * n_row), axis=-1, keepdims=True)
    hh = num * (1.0 / jnp.maximum(jnp.abs(den), gates["emt"][:, h:h + 1]))
    hn = _rms(hh, hn_ref[:, vs])
    out_ref[rows, vs] = (_sigmoid(o_ref[rows, vs].astype(F32)) * hn).astype(BF16)
    yield
    av = (a_col * vb.astype(F32)).astype(BF16)
    d_ct = lax.dot_general(kb, av, (((0,), (0,)), ((), ())), preferred_element_type=F32)
    d_n = jnp.sum(a_col * kh, axis=0, keepdims=True)
    so = gates["s_old"][:, h:h + 1]
    sn = gates["s_new"][:, h:h + 1]
    ct_scr[h] = so * ct + sn * d_ct
    n_scr[h:h + 1, :] = so * n_row + sn * d_n
    yield


def _out_kernel(tiles_per_seq, n_tiles, att_ref, ga_ref, gm_ref, x1_ref, qk_ref, v_ref, o_ref, gi_ref, gf_ref,
                cw_ref, cb_ref, bi_ref, bf_ref, hn_ref, watt_ref, wml_ref, wout_ref, g2_ref, wg_ref, wu_ref,
                wd_ref, gfin_ref, out_ref, hm_scr, ct_scr, n_scr, m_scr, conv_scr):
    tm = x1_ref.shape[0]
    L = MLSTM_CHUNK
    step = pl.program_id(0)
    slot = step % 2
    hm_new, hm_old = hm_scr.at[slot], hm_scr.at[1 - slot]

    @pl.when(step == 0)
    def _():
        hm_scr[...] = jnp.zeros_like(hm_scr)

    @pl.when(jnp.minimum(step, n_tiles - 1) % tiles_per_seq == 0)
    def _():
        ct_scr[...] = jnp.zeros_like(ct_scr)
        n_scr[...] = jnp.zeros_like(n_scr)
        m_scr[...] = jnp.zeros_like(m_scr)
        conv_scr[:CONV_TAIL_ROWS, :] = jnp.zeros((CONV_TAIL_ROWS, conv_scr.shape[1]), F32)

    def mixer():
        for c in range(tm // L):
            rows = slice(c * L, (c + 1) * L)
            qk, gates = _mlstm_prologue(rows, qk_ref, gi_ref, gf_ref, cw_ref, cb_ref, bi_ref, bf_ref, m_scr,
                                        conv_scr)
            yield
            for h in range(MLSTM_HEADS):
                yield from _mlstm_head(h, rows, qk, gates, v_ref, o_ref, hn_ref, hm_new, ct_scr, n_scr)

    def dense():
        def dot(x, w):
            return jnp.dot(x, w, preferred_element_type=F32)

        cols = _col_groups(0, D_MODEL)
        ya, ym, x2p = [], [], []
        for lo, hi in cols:
            ya.append(dot(att_ref[...], watt_ref[:, lo:hi]))
            yield
        for lo, hi in cols:
            ym.append(dot(hm_old[...], wml_ref[:, lo:hi]))
            yield
        y = jnp.concatenate(
            [_sigmoid(ga_ref[:, lo:hi].astype(F32)) * ya[j] + _sigmoid(gm_ref[:, lo:hi].astype(F32)) * ym[j]
             for j, (lo, hi) in enumerate(cols)], axis=1).astype(BF16)
        for lo, hi in cols:
            x2p.append(dot(y, wout_ref[:, lo:hi]))
            yield
        x2 = x1_ref[...] + jnp.concatenate(x2p, axis=1)
        h2 = _rms(x2, g2_ref[...]).astype(BF16)
        acc = None
        for clo, chi in FF_CHUNKS:
            groups = _col_groups(clo, chi)
            g = []
            for lo, hi in groups:
                g.append(dot(h2, wg_ref[:, lo:hi]))
                yield
            a = []
            for j, (lo, hi) in enumerate(groups):
                u = dot(h2, wu_ref[:, lo:hi])
                a.append((g[j] * _sigmoid(g[j]) * u).astype(BF16))
                yield
            a = jnp.concatenate(a, axis=1)
            part = []
            for lo, hi in cols:
                part.append(dot(a, wd_ref[clo:chi, lo:hi]))
                yield
            acc = part if acc is None else [p + q for p, q in zip(acc, part)]
        x3 = x2 + 0.5 * jnp.concatenate(acc, axis=1)
        out_ref[...] = _rms(x3, gfin_ref[...])
        yield

    streams = [mixer(), dense()]
    while streams:
        streams = [s for s in streams if next(s, True) is None]


def _out(att, ga, gm, x1, qkm, vm, om, gi, gf, cw, cb, bi, bf, hn, watt, wml, wout, g2, wg, wu, wd, gfin, tm, seq):
    t = x1.shape[0]
    n = t // tm

    def cur(width):
        return pl.BlockSpec((tm, width), lambda i: (jnp.minimum(i, n - 1), 0))

    lagged = pl.BlockSpec((tm, D_MODEL), lambda i: (jnp.maximum(i - 1, 0), 0))
    sq = _resident((D_MODEL, D_MODEL))
    vec = _resident((1, D_MODEL))
    return pl.pallas_call(
        functools.partial(_out_kernel, seq // tm, n),
        grid=(n + 1,),
        in_specs=[lagged, lagged, lagged, lagged,
                  cur(2 * ML_QK_W), cur(ML_V_W), cur(ML_V_W), cur(LANES), cur(LANES),
                  _resident((MLSTM_CONV, 2 * ML_QK_W)), _resident((1, 2 * ML_QK_W)),
                  _resident((1, LANES)), _resident((1, LANES)), vec,
                  sq, sq, sq, vec, _resident((D_MODEL, D_FF)), _resident((D_MODEL, D_FF)),
                  _resident((D_FF, D_MODEL)), vec],
        out_specs=lagged,
        out_shape=jax.ShapeDtypeStruct((t, D_MODEL), F32),
        scratch_shapes=[pltpu.VMEM((2, tm, ML_V_W), BF16),
                        pltpu.VMEM((MLSTM_HEADS, MLSTM_QK_DIM, MLSTM_V_DIM), F32),
                        pltpu.VMEM((SUBLANES, LANES), F32),
                        pltpu.VMEM((SUBLANES, LANES), F32),
                        pltpu.VMEM((CONV_TAIL_ROWS + MLSTM_CHUNK, 2 * ML_QK_W), F32)],
        compiler_params=pltpu.CompilerParams(dimension_semantics=("arbitrary",),
                                             vmem_limit_bytes=VMEM_LIMIT_BYTES),
        name="mlstm_merge_ffn2",
    )(att, ga, gm, x1, qkm, vm, om, gi, gf, cw, cb, bi, bf, hn, watt, wml, wout, g2, wg, wu, wd, gfin)


def _rope_tables(seq):
    half = ATT_HEAD_DIM // 2
    pos = jnp.arange(seq, dtype=F32)
    inv_freq = ROPE_THETA ** (-jnp.arange(half, dtype=F32) / half)
    ang = pos[:, None] * inv_freq[None, :]
    cos, sin = jnp.cos(ang), jnp.sin(ang)
    reps = LANES // ATT_HEAD_DIM
    cos = jnp.tile(jnp.concatenate([cos, cos], axis=-1), (1, reps))
    sin = jnp.tile(jnp.concatenate([-sin, sin], axis=-1), (1, reps))
    return cos, sin


def _pad_lanes(v):
    return jnp.pad(v, ((0, 0), (0, LANES - v.shape[-1])))


def _layer(x2d, batch, seq, p):
    (ffn1_norm, ffn1_w_gate, ffn1_w_up, ffn1_w_down, mix_norm, w_in, b_i, b_f, attn_sinks, conv_w, conv_b,
     head_norm, w_att, w_mlstm, w_out, ffn2_norm, ffn2_w_gate, ffn2_w_up, ffn2_w_down, final_norm) = p
    tm = min(TOKEN_TILE, seq)
    assert seq % tm == 0 and seq % ATT_BLOCK == 0 and seq % MLSTM_CHUNK == 0

    perm = np.asarray(Q_HEAD_PERM)
    col_perm = (perm[:, None] * ATT_HEAD_DIM + np.arange(ATT_HEAD_DIM)[None, :]).reshape(-1)
    offs = np.cumsum((0, ATT_Q_W, ATT_KV_W, ATT_KV_W, ML_QK_W, ML_QK_W, ML_V_W, ML_V_W,
                      MLSTM_HEADS, MLSTM_HEADS, D_MODEL))
    w_q = w_in[:, offs[0]:offs[1]][:, col_perm].astype(BF16)
    w_mid = w_in[:, offs[1]:offs[7]].astype(BF16)
    w_gate = w_in[:, offs[9]:].astype(BF16)
    w_if = jnp.concatenate([_pad_lanes(w_in[:, offs[7]:offs[8]]), _pad_lanes(w_in[:, offs[8]:offs[9]])],
                           axis=1).astype(BF16)
    assert w_mid.shape[1] == W_MID

    cos, sin = _rope_tables(seq)
    q_scale = ATT_HEAD_DIM ** -0.5
    sinks = jnp.broadcast_to(attn_sinks[perm][:, None], (ATT_HEADS, LANES)).astype(F32)

    x1 = _ffn1(x2d, ffn1_norm[None], ffn1_w_gate.astype(BF16), ffn1_w_up.astype(BF16),
               ffn1_w_down.astype(BF16), tm)
    att, qkm, vm, om, ga, gm, gi, gf = _inproj(
        x1, mix_norm[None], cos * q_scale, sin * q_scale, cos, sin, w_q, w_mid, w_gate, w_if, sinks, tm, seq)
    return _out(att, ga, gm, x1, qkm, vm, om, gi, gf, conv_w, conv_b[None], _pad_lanes(b_i[None]),
                _pad_lanes(b_f[None]), head_norm[None], w_att[col_perm, :].astype(BF16), w_mlstm.astype(BF16),
                w_out.astype(BF16), ffn2_norm[None], ffn2_w_gate.astype(BF16), ffn2_w_up.astype(BF16),
                ffn2_w_down.astype(BF16), final_norm[None], tm, seq)


def kernel(x, ffn1_norm, ffn1_w_gate, ffn1_w_up, ffn1_w_down, mix_norm, w_in, b_i, b_f, attn_sinks, conv_w,
           conv_b, head_norm, w_att, w_mlstm, w_out, ffn2_norm, ffn2_w_gate, ffn2_w_up, ffn2_w_down, final_norm):
    batch, seq, d = x.shape
    assert d == D_MODEL and ffn1_norm.shape[0] == 1, "single-layer kernel"
    per_layer = (ffn1_norm, ffn1_w_gate, ffn1_w_up, ffn1_w_down, mix_norm, w_in, b_i, b_f, attn_sinks, conv_w,
                 conv_b, head_norm, w_att, w_mlstm, w_out, ffn2_norm, ffn2_w_gate, ffn2_w_up, ffn2_w_down)
    params = tuple(a[0] for a in per_layer) + (final_norm,)
    out = _layer(x.reshape(batch * seq, d), batch, seq, params)
    return out.reshape(batch, seq, d)
```

```python
import functools

import jax
import jax.numpy as jnp
import numpy as np
from jax import lax
from jax.experimental import pallas as pl
from jax.experimental.pallas import tpu as pltpu

F32 = jnp.float32
BF16 = jnp.bfloat16

D_MODEL = 1024
ATT_HEADS = 16
ATT_KV_HEADS = 4
ATT_HEAD_DIM = 64
ATT_BLOCK = 128
ROPE_THETA = 10000.0
MLSTM_HEADS = 4
MLSTM_V_DIM = D_MODEL // MLSTM_HEADS
MLSTM_QK_DIM = MLSTM_V_DIM // 2
MLSTM_CHUNK = 128
MLSTM_CONV = 4
D_FF = 2816
RMS_EPS = 1e-5
NEG_LOG2E = -1.4426950408889634

ATT_Q_W = ATT_HEADS * ATT_HEAD_DIM
ATT_KV_W = ATT_KV_HEADS * ATT_HEAD_DIM
ML_QK_W = MLSTM_HEADS * MLSTM_QK_DIM
ML_V_W = MLSTM_HEADS * MLSTM_V_DIM

LANES = 128
MXU_COLS = 256
SUBLANES = 8
CONV_TAIL_ROWS = SUBLANES
VMEM_LIMIT_BYTES = 60 * 1024 * 1024

TOKEN_TILE = 512
ZERO_FILL_ROWS = 64
FF_CHUNKS = ((0, 768), (768, 1536), (1536, 2304), (2304, D_FF))

_GROUP = ATT_HEADS // ATT_KV_HEADS
Q_HEAD_PERM = tuple(
    (2 * c + half) * _GROUP + r
    for c in range(ATT_KV_HEADS // 2) for r in range(_GROUP) for half in range(2))


def _resident(shape):
    return pl.BlockSpec(shape, lambda *_: (0,) * len(shape), pipeline_mode=pl.Buffered(1))


def _col_groups(lo, hi):
    return [(c, min(c + MXU_COLS, hi)) for c in range(lo, hi, MXU_COLS)]


def _zero_fill(ref):
    slots, rows, _ = ref.shape
    assert rows % ZERO_FILL_ROWS == 0

    def body(i, carry):
        start = pl.multiple_of(i * ZERO_FILL_ROWS, ZERO_FILL_ROWS)
        for s in range(slots):
            ref[s, pl.ds(start, ZERO_FILL_ROWS), :] = jnp.zeros((ZERO_FILL_ROWS, ref.shape[2]), ref.dtype)
        return carry

    lax.fori_loop(0, rows // ZERO_FILL_ROWS, body, 0)


def _interleave(mixer, dense):
    for n_dense in mixer:
        for _ in range(n_dense):
            next(dense, None)
    for _ in dense:
        pass


def _rms(x, g):
    ms = jnp.mean(x * x, axis=-1, keepdims=True)
    return x * lax.rsqrt(ms + RMS_EPS) * g


def _sigmoid(x):
    return 1.0 / (1.0 + jnp.exp2(x * NEG_LOG2E))


def _swiglu(h, wg_ref, wu_ref, wd_ref):
    acc = None
    for lo, hi in FF_CHUNKS:
        g = jnp.dot(h, wg_ref[:, lo:hi], preferred_element_type=F32)
        u = jnp.dot(h, wu_ref[:, lo:hi], preferred_element_type=F32)
        a = (g * _sigmoid(g) * u).astype(BF16)
        d = jnp.dot(a, wd_ref[lo:hi, :], preferred_element_type=F32)
        acc = d if acc is None else acc + d
    return acc


def _ffn1_kernel(x_ref, g_ref, wg_ref, wu_ref, wd_ref, o_ref):
    x = x_ref[...]
    h = _rms(x, g_ref[...]).astype(BF16)
    o_ref[...] = x + 0.5 * _swiglu(h, wg_ref, wu_ref, wd_ref)


def _ffn1(x, g, wg, wu, wd, tm):
    t = x.shape[0]
    tok = pl.BlockSpec((tm, D_MODEL), lambda i: (i, 0))
    return pl.pallas_call(
        _ffn1_kernel,
        grid=(t // tm,),
        in_specs=[tok, _resident((1, D_MODEL)), _resident((D_MODEL, D_FF)), _resident((D_MODEL, D_FF)),
                  _resident((D_FF, D_MODEL))],
        out_specs=tok,
        out_shape=jax.ShapeDtypeStruct((t, D_MODEL), F32),
        compiler_params=pltpu.CompilerParams(dimension_semantics=("arbitrary",),
                                             vmem_limit_bytes=VMEM_LIMIT_BYTES),
        name="ffn1",
    )(x, g, wg, wu, wd)


_C_KA = 0
_C_VA = _C_KA + ATT_KV_W
_C_QKM = _C_VA + ATT_KV_W
_C_VM = _C_QKM + 2 * ML_QK_W
_C_OM = _C_VM + ML_V_W
W_MID = _C_OM + ML_V_W


def _rope(x, cos, sin_signed):
    lane = lax.broadcasted_iota(jnp.int32, x.shape, 1)
    half = ATT_HEAD_DIM // 2
    partner = jnp.where(lane % ATT_HEAD_DIM < half, pltpu.roll(x, LANES - half, axis=1),
                        pltpu.roll(x, half, axis=1))
    return x * cos + partner * sin_signed


def _attn_scores(c, q_blk, k_cur, k_prev):
    low = lax.broadcasted_iota(jnp.int32, (ATT_BLOCK, LANES), 1) < ATT_HEAD_DIM
    ksl = slice(c * LANES, (c + 1) * LANES)
    kc = k_cur[:, ksl].astype(F32)
    kp = k_prev[:, ksl].astype(F32)
    zero = jnp.zeros_like(kc)
    kstack = jnp.concatenate(
        [jnp.where(low, kc, zero), jnp.where(low, kp, zero),
         jnp.where(low, zero, kc), jnp.where(low, zero, kp)], axis=0).astype(BF16)
    qstack = jnp.concatenate(
        [q_blk[:, (c * _GROUP + r) * LANES:(c * _GROUP + r + 1) * LANES] for r in range(_GROUP)], axis=0)
    return lax.dot_general(kstack, qstack, (((1,), (1,)), ((), ())), preferred_element_type=F32)


def _attn_block(q_blk, kv_cur, kv_prev, has_prev, sink_ref, pt_scr, o_ref, rows):
    blk = ATT_BLOCK
    key = lax.broadcasted_iota(jnp.int32, (blk, blk), 0)
    qry = lax.broadcasted_iota(jnp.int32, (blk, blk), 1)
    in_cur = key <= qry
    neg_inf = jnp.float32(-jnp.inf)
    n_chunks = ATT_KV_W // LANES
    st = []
    for c in range(n_chunks):
        st.append(_attn_scores(c, q_blk, kv_cur[:, :ATT_KV_W], kv_prev[:, :ATT_KV_W]))
    yield 1
    v_cat = jnp.concatenate([kv_cur[:, ATT_KV_W:], kv_prev[:, ATT_KV_W:]], axis=0).astype(F32)
    vt_all = v_cat.T.astype(BF16)
    for c in range(n_chunks):
        for r in range(_GROUP):
            qs = slice(r * blk, (r + 1) * blk)
            for half in range(2):
                p_idx = (c * _GROUP + r) * 2 + half
                s_cur = st[c][(2 * half) * blk:(2 * half + 1) * blk, qs]
                s_prev = st[c][(2 * half + 1) * blk:(2 * half + 2) * blk, qs]
                s = jnp.where(in_cur, s_cur, jnp.where(has_prev, s_prev, neg_inf))
                sink = sink_ref[p_idx:p_idx + 1, :]
                m = jnp.maximum(jnp.max(s, axis=0, keepdims=True), sink)
                p = jnp.exp(s - m)
                denom = jnp.sum(p, axis=0, keepdims=True) + jnp.exp(sink - m)
                p = p * (1.0 / denom)
                zp = jnp.zeros_like(p)
                pt_scr[c * 2 + half, :blk, qs] = jnp.where(in_cur, p, zp).astype(BF16)
                pt_scr[c * 2 + half, blk:, qs] = jnp.where(in_cur, zp, p).astype(BF16)
        ot = [jnp.dot(vt_all[(2 * c + half) * ATT_HEAD_DIM:(2 * c + half + 1) * ATT_HEAD_DIM, :],
                      pt_scr[c * 2 + half], preferred_element_type=F32) for half in range(2)]
        for r in range(_GROUP):
            j = c * _GROUP + r
            qs = slice(r * blk, (r + 1) * blk)
            o_t = jnp.concatenate([ot[0][:, qs], ot[1][:, qs]], axis=0)
            o_ref[rows, j * LANES:(j + 1) * LANES] = o_t.T.astype(BF16)
    yield 0


def _inproj_kernel(blocks_per_seq, x_ref, g_ref, cq_ref, sq_ref, ck_ref, sk_ref, wq_ref, wmid_ref, wgate_ref, wif_ref,
                   sink_ref,
                   att_ref, qkm_ref, vm_ref, om_ref, ga_ref, gm_ref, gi_ref, gf_ref, q_scr, kv_scr, pt_scr):
    tm = x_ref.shape[0]
    blk = ATT_BLOCK
    step = pl.program_id(0)
    slot = step % 2
    q_new, kv_new = q_scr.at[slot], kv_scr.at[slot]
    q_old, kv_old = q_scr.at[1 - slot], kv_scr.at[1 - slot]

    @pl.when(step == 0)
    def _():
        _zero_fill(q_scr)
        _zero_fill(kv_scr)

    h = _rms(x_ref[...], g_ref[...]).astype(BF16)
    kv_new[:blk, :] = kv_old[tm:, :]

    def mixer():
        for k in range(tm // blk):
            rows = slice(k * blk, (k + 1) * blk)
            has_prev = ((step - 1) * (tm // blk) + k) % blocks_per_seq != 0
            yield from _attn_block(q_old[rows, :], kv_old[blk + k * blk:2 * blk + k * blk, :],
                                   kv_old[k * blk:(k + 1) * blk, :], has_prev, sink_ref, pt_scr, att_ref, rows)

    def dense():
        def proj(w_ref, lo, width):
            return jnp.dot(h, w_ref[:, lo:lo + width], preferred_element_type=F32)

        z = proj(wq_ref, 0, ATT_Q_W)
        cq, sq = cq_ref[...], sq_ref[...]
        for j in range(ATT_Q_W // LANES):
            sl = slice(j * LANES, (j + 1) * LANES)
            q_new[:, sl] = _rope(z[:, sl], cq, sq).astype(BF16)
        yield
        z = proj(wmid_ref, _C_KA, 2 * ATT_KV_W)
        ck, sk = ck_ref[...], sk_ref[...]
        for j in range(ATT_KV_W // LANES):
            sl = slice(j * LANES, (j + 1) * LANES)
            kv_new[blk:, sl] = _rope(z[:, sl], ck, sk).astype(BF16)
        kv_new[blk:, ATT_KV_W:] = z[:, ATT_KV_W:].astype(BF16)
        vm_ref[...] = proj(wmid_ref, _C_VM, ML_V_W).astype(BF16)
        yield
        om_ref[...] = proj(wmid_ref, _C_OM, ML_V_W).astype(BF16)
        ga_ref[...] = proj(wgate_ref, 0, D_MODEL).astype(BF16)
        yield
        gm_ref[...] = proj(wgate_ref, D_MODEL, D_MODEL).astype(BF16)
        z = proj(wif_ref, 0, 2 * LANES)
        gi_ref[...] = z[:, :LANES]
        gf_ref[...] = z[:, LANES:]
        qkm_ref[...] = proj(wmid_ref, _C_QKM, 2 * ML_QK_W).astype(BF16)
        yield

    _interleave(mixer(), dense())


def _inproj(x1, g, cq, sq, ck, sk, wq, wmid, wgate, wif, sinks, tm, seq):
    t = x1.shape[0]
    n = t // tm
    tiles_per_seq = seq // tm

    def cur(i):
        return jnp.minimum(i, n - 1)

    def tok(width):
        return pl.BlockSpec((tm, width), lambda i: (cur(i), 0))

    pos = pl.BlockSpec((tm, LANES), lambda i: (cur(i) % tiles_per_seq, 0))
    lagged = pl.BlockSpec((tm, ATT_Q_W), lambda i: (jnp.maximum(i - 1, 0), 0))
    widths = (2 * ML_QK_W, ML_V_W, ML_V_W, D_MODEL, D_MODEL)
    out_shape = [jax.ShapeDtypeStruct((t, ATT_Q_W), BF16)]
    out_shape += [jax.ShapeDtypeStruct((t, w_), BF16) for w_ in widths]
    out_shape += [jax.ShapeDtypeStruct((t, LANES), F32)] * 2
    return pl.pallas_call(
        functools.partial(_inproj_kernel, seq // ATT_BLOCK),
        grid=(n + 1,),
        in_specs=[tok(D_MODEL), _resident((1, D_MODEL)), pos, pos, pos, pos, _resident((D_MODEL, ATT_Q_W)),
                  _resident((D_MODEL, W_MID)), _resident((D_MODEL, 2 * D_MODEL)), _resident((D_MODEL, 2 * LANES)),
                  _resident((ATT_HEADS, LANES))],
        out_specs=[lagged] + [tok(w_) for w_ in widths] + [tok(LANES)] * 2,
        out_shape=out_shape,
        scratch_shapes=[pltpu.VMEM((2, tm, ATT_Q_W), BF16),
                        pltpu.VMEM((2, ATT_BLOCK + tm, 2 * ATT_KV_W), BF16),
                        pltpu.VMEM((ATT_KV_HEADS, 2 * ATT_BLOCK, _GROUP * ATT_BLOCK), BF16)],
        compiler_params=pltpu.CompilerParams(dimension_semantics=("arbitrary",),
                                             vmem_limit_bytes=VMEM_LIMIT_BYTES),
        name="in_proj_attn",
    )(x1, g, cq, sq, ck, sk, wq, wmid, wgate, wif, sinks)


def _scan_rows(x, op, fill):
    n = x.shape[0]
    row = lax.broadcasted_iota(jnp.int32, x.shape, 0)
    shift = 1
    while shift < n:
        prev = jnp.where(row >= shift, pltpu.roll(x, shift, axis=0), fill)
        x = op(x, prev)
        shift *= 2
    return x


def _mlstm_prologue(rows, qk_ref, gi_ref, gf_ref, cw_ref, cb_ref, bi_ref, bf_ref, m_scr, conv_scr):
    L = MLSTM_CHUNK
    tail = CONV_TAIL_ROWS
    x = qk_ref[rows, :].astype(F32)
    conv_scr[tail:, :] = x
    y = cb_ref[...] + cw_ref[MLSTM_CONV - 1:MLSTM_CONV, :] * x
    for j in range(1, MLSTM_CONV):
        y = y + cw_ref[MLSTM_CONV - 1 - j:MLSTM_CONV - j, :] * conv_scr[tail - j:tail - j + L, :]
    conv_scr[:tail, :] = x[L - tail:, :]
    qk = y * _sigmoid(y)

    ig = gi_ref[rows, :] + bi_ref[...]
    fpre = gf_ref[rows, :] + bf_ref[...]
    logf = jnp.minimum(fpre, 0.0) - jnp.log1p(jnp.exp(-jnp.abs(fpre)))
    b = _scan_rows(logf, jnp.add, 0.0)
    g = ig - b
    cm = _scan_rows(g, jnp.maximum, -jnp.inf)
    m_prev = m_scr[0:1, :]
    u = jnp.maximum(cm, m_prev)
    g_max = cm[L - 1:L, :]
    u_last = u[L - 1:L, :]
    gates = dict(u=u, a=jnp.exp(g - g_max), inter=jnp.exp(m_prev - u), emt=jnp.exp(-(b + u)),
                 s_old=jnp.exp(m_prev - u_last), s_new=jnp.exp(g_max - u_last),
                 g_rows=g.T)
    m_scr[0:1, :] = b[L - 1:L, :] + u_last
    return qk, gates


def _mlstm_head(h, rows, qk, gates, v_ref, o_ref, hn_ref, out_ref, ct_scr, n_scr):
    L = MLSTM_CHUNK
    row = lax.broadcasted_iota(jnp.int32, (L, L), 0)
    col = lax.broadcasted_iota(jnp.int32, (L, L), 1)
    causal = col <= row
    qs = slice(h * MLSTM_QK_DIM, (h + 1) * MLSTM_QK_DIM)
    ks = slice(ML_QK_W + h * MLSTM_QK_DIM, ML_QK_W + (h + 1) * MLSTM_QK_DIM)
    vs = slice(h * MLSTM_V_DIM, (h + 1) * MLSTM_V_DIM)
    qh = qk[:, qs]
    kh = qk[:, ks] * (MLSTM_QK_DIM ** -0.5)
    qb = qh.astype(BF16)
    kb = kh.astype(BF16)
    vb = v_ref[rows, vs]
    inter_col = gates["inter"][:, h:h + 1]
    a_col = gates["a"][:, h:h + 1]

    decay = jnp.where(causal, jnp.exp(gates["g_rows"][h:h + 1, :] - gates["u"][:, h:h + 1]), 0.0)
    s = lax.dot_general(qb, kb, (((1,), (1,)), ((), ())), preferred_element_type=F32) * decay
    ct = ct_scr[h]
    n_row = n_scr[h:h + 1, :]
    yield 0
    num = (jnp.dot(s.astype(BF16), vb, preferred_element_type=F32)
           + inter_col * jnp.dot(qb, ct.astype(BF16), preferred_element_type=F32))
    den = jnp.sum(s + inter_col * (qh * n_row), axis=-1, keepdims=True)
    hh = num * (1.0 / jnp.maximum(jnp.abs(den), gates["emt"][:, h:h + 1]))
    hn = _rms(hh, hn_ref[:, vs])
    out_ref[rows, vs] = (_sigmoid(o_ref[rows, vs].astype(F32)) * hn).astype(BF16)
    yield 1 + h % 2
    av = (a_col * vb.astype(F32)).astype(BF16)
    d_ct = lax.dot_general(kb, av, (((0,), (0,)), ((), ())), preferred_element_type=F32)
    d_n = jnp.sum(a_col * kh, axis=0, keepdims=True)
    so = gates["s_old"][:, h:h + 1]
    sn = gates["s_new"][:, h:h + 1]
    ct_scr[h] = so * ct + sn * d_ct
    n_scr[h:h + 1, :] = so * n_row + sn * d_n
    yield 1


def _out_kernel(tiles_per_seq, n_tiles, att_ref, ga_ref, gm_ref, x1_ref, qk_ref, v_ref, o_ref, gi_ref, gf_ref,
                cw_ref, cb_ref, bi_ref, bf_ref, hn_ref, watt_ref, wml_ref, wout_ref, g2_ref, wg_ref, wu_ref,
                wd_ref, gfin_ref, out_ref, hm_scr, ct_scr, n_scr, m_scr, conv_scr):
    tm = x1_ref.shape[0]
    L = MLSTM_CHUNK
    step = pl.program_id(0)
    slot = step % 2
    hm_new, hm_old = hm_scr.at[slot], hm_scr.at[1 - slot]

    @pl.when(step == 0)
    def _():
        _zero_fill(hm_scr)

    @pl.when(jnp.minimum(step, n_tiles - 1) % tiles_per_seq == 0)
    def _():
        _zero_fill(ct_scr)
        n_scr[...] = jnp.zeros_like(n_scr)
        m_scr[...] = jnp.zeros_like(m_scr)
        conv_scr[:CONV_TAIL_ROWS, :] = jnp.zeros((CONV_TAIL_ROWS, conv_scr.shape[1]), F32)

    def mixer():
        for c in range(tm // L):
            rows = slice(c * L, (c + 1) * L)
            qk, gates = _mlstm_prologue(rows, qk_ref, gi_ref, gf_ref, cw_ref, cb_ref, bi_ref, bf_ref, m_scr,
                                        conv_scr)
            yield 3
            for h in range(MLSTM_HEADS):
                yield from _mlstm_head(h, rows, qk, gates, v_ref, o_ref, hn_ref, hm_new, ct_scr, n_scr)

    def dense():
        def dot(x, w):
            return jnp.dot(x, w, preferred_element_type=F32)

        cols = _col_groups(0, D_MODEL)
        ya, ym, x2p = [], [], []
        for lo, hi in cols:
            ya.append(dot(att_ref[...], watt_ref[:, lo:hi]))
            yield
        for lo, hi in cols:
            ym.append(dot(hm_old[...], wml_ref[:, lo:hi]))
            yield
        y = jnp.concatenate(
            [_sigmoid(ga_ref[:, lo:hi].astype(F32)) * ya[j] + _sigmoid(gm_ref[:, lo:hi].astype(F32)) * ym[j]
             for j, (lo, hi) in enumerate(cols)], axis=1).astype(BF16)
        for lo, hi in cols:
            x2p.append(dot(y, wout_ref[:, lo:hi]))
            yield
        x2 = x1_ref[...] + jnp.concatenate(x2p, axis=1)
        h2 = _rms(x2, g2_ref[...]).astype(BF16)
        acc = None
        for clo, chi in FF_CHUNKS:
            groups = _col_groups(clo, chi)
            g = []
            for lo, hi in groups:
                g.append(dot(h2, wg_ref[:, lo:hi]))
                yield
            a = []
            for j, (lo, hi) in enumerate(groups):
                u = dot(h2, wu_ref[:, lo:hi])
                a.append((g[j] * _sigmoid(g[j]) * u).astype(BF16))
                yield
            a = jnp.concatenate(a, axis=1)
            part = []
            for lo, hi in cols:
                part.append(dot(a, wd_ref[clo:chi, lo:hi]))
                yield
            acc = part if acc is None else [p + q for p, q in zip(acc, part)]
        x3 = x2 + 0.5 * jnp.concatenate(acc, axis=1)
        out_ref[...] = _rms(x3, gfin_ref[...])
        yield

    _interleave(mixer(), dense())


def _out(att, ga, gm, x1, qkm, vm, om, gi, gf, cw, cb, bi, bf, hn, watt, wml, wout, g2, wg, wu, wd, gfin, tm, seq):
    t = x1.shape[0]
    n = t // tm

    def cur(width):
        return pl.BlockSpec((tm, width), lambda i: (jnp.minimum(i, n - 1), 0))

    lagged = pl.BlockSpec((tm, D_MODEL), lambda i: (jnp.maximum(i - 1, 0), 0))
    sq = _resident((D_MODEL, D_MODEL))
    vec = _resident((1, D_MODEL))
    return pl.pallas_call(
        functools.partial(_out_kernel, seq // tm, n),
        grid=(n + 1,),
        in_specs=[lagged, lagged, lagged, lagged,
                  cur(2 * ML_QK_W), cur(ML_V_W), cur(ML_V_W), cur(LANES), cur(LANES),
                  _resident((MLSTM_CONV, 2 * ML_QK_W)), _resident((1, 2 * ML_QK_W)),
                  _resident((1, LANES)), _resident((1, LANES)), vec,
                  sq, sq, sq, vec, _resident((D_MODEL, D_FF)), _resident((D_MODEL, D_FF)),
                  _resident((D_FF, D_MODEL)), vec],
        out_specs=lagged,
        out_shape=jax.ShapeDtypeStruct((t, D_MODEL), F32),
        scratch_shapes=[pltpu.VMEM((2, tm, ML_V_W), BF16),
                        pltpu.VMEM((MLSTM_HEADS, MLSTM_QK_DIM, MLSTM_V_DIM), F32),
                        pltpu.VMEM((SUBLANES, LANES), F32),
                        pltpu.VMEM((SUBLANES, LANES), F32),
                        pltpu.VMEM((CONV_TAIL_ROWS + MLSTM_CHUNK, 2 * ML_QK_W), F32)],
        compiler_params=pltpu.CompilerParams(dimension_semantics=("arbitrary",),
                                             vmem_limit_bytes=VMEM_LIMIT_BYTES),
        name="mlstm_merge_ffn2",
    )(att, ga, gm, x1, qkm, vm, om, gi, gf, cw, cb, bi, bf, hn, watt, wml, wout, g2, wg, wu, wd, gfin)


def _rope_tables(seq):
    half = ATT_HEAD_DIM // 2
    pos = jnp.arange(seq, dtype=F32)
    inv_freq = ROPE_THETA ** (-jnp.arange(half, dtype=F32) / half)
    ang = pos[:, None] * inv_freq[None, :]
    cos, sin = jnp.cos(ang), jnp.sin(ang)
    reps = LANES // ATT_HEAD_DIM
    cos = jnp.tile(jnp.concatenate([cos, cos], axis=-1), (1, reps))
    sin = jnp.tile(jnp.concatenate([-sin, sin], axis=-1), (1, reps))
    return cos, sin


def _pad_lanes(v):
    return jnp.pad(v, ((0, 0), (0, LANES - v.shape[-1])))


def _layer(x2d, batch, seq, p):
    (ffn1_norm, ffn1_w_gate, ffn1_w_up, ffn1_w_down, mix_norm, w_in, b_i, b_f, attn_sinks, conv_w, conv_b,
     head_norm, w_att, w_mlstm, w_out, ffn2_norm, ffn2_w_gate, ffn2_w_up, ffn2_w_down, final_norm) = p
    tm = min(TOKEN_TILE, seq)
    assert seq % tm == 0 and seq % ATT_BLOCK == 0 and seq % MLSTM_CHUNK == 0

    perm = np.asarray(Q_HEAD_PERM)
    col_perm = (perm[:, None] * ATT_HEAD_DIM + np.arange(ATT_HEAD_DIM)[None, :]).reshape(-1)
    offs = np.cumsum((0, ATT_Q_W, ATT_KV_W, ATT_KV_W, ML_QK_W, ML_QK_W, ML_V_W, ML_V_W,
                      MLSTM_HEADS, MLSTM_HEADS, D_MODEL))
    w_q = w_in[:, offs[0]:offs[1]][:, col_perm].astype(BF16)
    w_mid = w_in[:, offs[1]:offs[7]].astype(BF16)
    w_gate = w_in[:, offs[9]:].astype(BF16)
    w_if = jnp.concatenate([_pad_lanes(w_in[:, offs[7]:offs[8]]), _pad_lanes(w_in[:, offs[8]:offs[9]])],
                           axis=1).astype(BF16)
    assert w_mid.shape[1] == W_MID

    cos, sin = _rope_tables(seq)
    q_scale = ATT_HEAD_DIM ** -0.5
    sinks = jnp.broadcast_to(attn_sinks[perm][:, None], (ATT_HEADS, LANES)).astype(F32)

    x1 = _ffn1(x2d, ffn1_norm[None], ffn1_w_gate.astype(BF16), ffn1_w_up.astype(BF16),
               ffn1_w_down.astype(BF16), tm)
    att, qkm, vm, om, ga, gm, gi, gf = _inproj(
        x1, mix_norm[None], cos * q_scale, sin * q_scale, cos, sin, w_q, w_mid, w_gate, w_if, sinks, tm, seq)
    return _out(att, ga, gm, x1, qkm, vm, om, gi, gf, conv_w, conv_b[None], _pad_lanes(b_i[None]),
                _pad_lanes(b_f[None]), head_norm[None], w_att[col_perm, :].astype(BF16), w_mlstm.astype(BF16),
                w_out.astype(BF16), ffn2_norm[None], ffn2_w_gate.astype(BF16), ffn2_w_up.astype(BF16),
                ffn2_w_down.astype(BF16), final_norm[None], tm, seq)


def kernel(x, ffn1_norm, ffn1_w_gate, ffn1_w_up, ffn1_w_down, mix_norm, w_in, b_i, b_f, attn_sinks, conv_w,
           conv_b, head_norm, w_att, w_mlstm, w_out, ffn2_norm, ffn2_w_gate, ffn2_w_up, ffn2_w_down, final_norm):
    batch, seq, d = x.shape
    assert d == D_MODEL and ffn1_norm.shape[0] == 1, "single-layer kernel"
    per_layer = (ffn1_norm, ffn1_w_gate, ffn1_w_up, ffn1_w_down, mix_norm, w_in, b_i, b_f, attn_sinks, conv_w,
                 conv_b, head_norm, w_att, w_mlstm, w_out, ffn2_norm, ffn2_w_gate, ffn2_w_up, ffn2_w_down)
    params = tuple(a[0] for a in per_layer) + (final_norm,)
    out = _layer(x.reshape(batch * seq, d), batch, seq, params)
    return out.reshape(batch, seq, d)
```

```python
import functools

import jax
import jax.numpy as jnp
import numpy as np
from jax import lax
from jax.experimental import pallas as pl
from jax.experimental.pallas import tpu as pltpu

F32 = jnp.float32
BF16 = jnp.bfloat16

D_MODEL = 1024
ATT_HEADS = 16
ATT_KV_HEADS = 4
ATT_HEAD_DIM = 64
ATT_BLOCK = 128
ROPE_THETA = 10000.0
MLSTM_HEADS = 4
MLSTM_V_DIM = D_MODEL // MLSTM_HEADS
MLSTM_QK_DIM = MLSTM_V_DIM // 2
MLSTM_CHUNK = 128
MLSTM_CONV = 4
D_FF = 2816
RMS_EPS = 1e-5
NEG_LOG2E = -1.4426950408889634

ATT_Q_W = ATT_HEADS * ATT_HEAD_DIM
ATT_KV_W = ATT_KV_HEADS * ATT_HEAD_DIM
ML_QK_W = MLSTM_HEADS * MLSTM_QK_DIM
ML_V_W = MLSTM_HEADS * MLSTM_V_DIM

LANES = 128
MXU_COLS = 256
SUBLANES = 8
BF16_SUBLANES = 16
CONV_TAIL_ROWS = SUBLANES
VMEM_LIMIT_BYTES = 60 * 1024 * 1024

TOKEN_TILE = 512
ZERO_FILL_ROWS = 64
FF_CHUNKS = ((0, 768), (768, 1536), (1536, 2304), (2304, D_FF))

Q_CHUNKS_PER_GROUP = ATT_HEADS // ATT_KV_HEADS * ATT_HEAD_DIM // LANES

_C_KA = 0
_C_VA = _C_KA + ATT_KV_W
_C_QKM = _C_VA + ATT_KV_W
_C_VM = _C_QKM + 2 * ML_QK_W
_C_OM = _C_VM + ML_V_W
W_MID = _C_OM + ML_V_W

_IN_OFFS = tuple(int(v) for v in np.cumsum((0, ATT_Q_W, ATT_KV_W, ATT_KV_W, ML_QK_W, ML_QK_W, ML_V_W, ML_V_W,
                                            MLSTM_HEADS, MLSTM_HEADS, D_MODEL, D_MODEL)))
IN_PROJ_W = _IN_OFFS[-1]


def _resident(shape):
    return pl.BlockSpec(shape, lambda *_: (0,) * len(shape), pipeline_mode=pl.Buffered(1))


def _col_groups(lo, hi):
    return [(c, min(c + MXU_COLS, hi)) for c in range(lo, hi, MXU_COLS)]


def _zero_fill(ref):
    slots, rows, _ = ref.shape
    assert rows % ZERO_FILL_ROWS == 0

    def body(i, carry):
        start = pl.multiple_of(i * ZERO_FILL_ROWS, ZERO_FILL_ROWS)
        for s in range(slots):
            ref[s, pl.ds(start, ZERO_FILL_ROWS), :] = jnp.zeros((ZERO_FILL_ROWS, ref.shape[2]), ref.dtype)
        return carry

    lax.fori_loop(0, rows // ZERO_FILL_ROWS, body, 0)


def _interleave(mixer, dense):
    for n_dense in mixer:
        for _ in range(n_dense):
            next(dense, None)
    for _ in dense:
        pass


def _rms(x, g):
    ms = jnp.mean(x * x, axis=-1, keepdims=True)
    return x * lax.rsqrt(ms + RMS_EPS) * g


def _sigmoid(x):
    return 1.0 / (1.0 + jnp.exp2(x * NEG_LOG2E))


def _swiglu(h, wg_ref, wu_ref, wd_ref):
    acc = None
    for lo, hi in FF_CHUNKS:
        g = jnp.dot(h, wg_ref[:, lo:hi], preferred_element_type=F32)
        u = jnp.dot(h, wu_ref[:, lo:hi], preferred_element_type=F32)
        a = (g * _sigmoid(g) * u).astype(BF16)
        d = jnp.dot(a, wd_ref[lo:hi, :], preferred_element_type=F32)
        acc = d if acc is None else acc + d
    return acc


def _ffn1_kernel(x_ref, g_ref, wg_ref, wu_ref, wd_ref, win_ref, *refs):
    n_plain = (len(refs) - 5) // 2
    plain_in, (o_ref, wq_ref, wmid_ref, wgate_ref, wif_ref), plain_out = (
        refs[:n_plain], refs[n_plain:n_plain + 5], refs[n_plain + 5:])
    x = x_ref[...]
    h = _rms(x, g_ref[...]).astype(BF16)
    o_ref[...] = x + 0.5 * _swiglu(h, wg_ref, wu_ref, wd_ref)

    for src, dst in zip(plain_in, plain_out):
        dst[...] = src[...].astype(BF16)
    w = win_ref[...]
    wq_ref[...] = w[:, _IN_OFFS[0]:_IN_OFFS[1]].astype(BF16)
    wmid_ref[...] = w[:, _IN_OFFS[1]:_IN_OFFS[7]].astype(BF16)
    wgate_ref[...] = w[:, _IN_OFFS[9]:].astype(BF16)
    pad = jnp.zeros((w.shape[0], LANES - MLSTM_HEADS), F32)
    wif_ref[...] = jnp.concatenate([w[:, _IN_OFFS[7]:_IN_OFFS[8]], pad, w[:, _IN_OFFS[8]:_IN_OFFS[9]], pad],
                                   axis=1).astype(BF16)


def _row_block(rows, n_steps):
    return next(r for r in range(BF16_SUBLANES, rows + 1, BF16_SUBLANES)
                if rows % r == 0 and r * n_steps >= rows)


def _ffn1(x, g, wg, wu, wd, w_in, later_weights, tm):
    t = x.shape[0]
    n = t // tm
    tok = pl.BlockSpec((tm, D_MODEL), lambda i: (i, 0))

    def streamed(w):
        r = _row_block(w.shape[0], n)
        last = w.shape[0] // r - 1
        return lambda width: pl.BlockSpec((r, width), lambda i: (jnp.minimum(i, last), 0))

    in_spec = streamed(w_in)
    plain = [streamed(w) for w in later_weights]
    in_widths = (ATT_Q_W, W_MID, 2 * D_MODEL, 2 * LANES)
    out = pl.pallas_call(
        _ffn1_kernel,
        grid=(n,),
        in_specs=[tok, _resident((1, D_MODEL)), _resident((D_MODEL, D_FF)), _resident((D_MODEL, D_FF)),
                  _resident((D_FF, D_MODEL)), in_spec(IN_PROJ_W)]
        + [spec(w.shape[1]) for spec, w in zip(plain, later_weights)],
        out_specs=[tok] + [in_spec(w_) for w_ in in_widths]
        + [spec(w.shape[1]) for spec, w in zip(plain, later_weights)],
        out_shape=[jax.ShapeDtypeStruct((t, D_MODEL), F32)]
        + [jax.ShapeDtypeStruct((D_MODEL, w_), BF16) for w_ in in_widths]
        + [jax.ShapeDtypeStruct(w.shape, BF16) for w in later_weights],
        compiler_params=pltpu.CompilerParams(dimension_semantics=("arbitrary",),
                                             vmem_limit_bytes=VMEM_LIMIT_BYTES),
        name="ffn1",
    )(x, g, wg, wu, wd, w_in, *later_weights)
    return out[0], out[1:5], out[5:]


def _rope(x, cos, sin_signed):
    lane = lax.broadcasted_iota(jnp.int32, x.shape, 1)
    half = ATT_HEAD_DIM // 2
    partner = jnp.where(lane % ATT_HEAD_DIM < half, pltpu.roll(x, LANES - half, axis=1),
                        pltpu.roll(x, half, axis=1))
    return x * cos + partner * sin_signed


def _attn_scores(g, q_blk, k_cur, k_prev):
    low = lax.broadcasted_iota(jnp.int32, (ATT_BLOCK, LANES), 1) < ATT_HEAD_DIM
    c, in_high = divmod(g, 2)
    ksl = slice(c * LANES, (c + 1) * LANES)
    parts = []
    for k in (k_cur, k_prev):
        x = k[:, ksl].astype(F32)
        other = pltpu.roll(x, ATT_HEAD_DIM, axis=1)
        in_low_lanes, in_high_lanes = (other, x) if in_high else (x, other)
        parts.append((jnp.where(low, in_low_lanes, 0.0), jnp.where(low, 0.0, in_high_lanes)))
    kstack = jnp.concatenate([parts[0][0], parts[1][0], parts[0][1], parts[1][1]], axis=0).astype(BF16)
    qstack = jnp.concatenate(
        [q_blk[:, (g * Q_CHUNKS_PER_GROUP + r) * LANES:(g * Q_CHUNKS_PER_GROUP + r + 1) * LANES]
         for r in range(Q_CHUNKS_PER_GROUP)], axis=0)
    return lax.dot_general(kstack, qstack, (((1,), (1,)), ((), ())), preferred_element_type=F32)


def _attn_block(q_blk, kv_cur, kv_prev, has_prev, sink_ref, pt_scr, o_ref, rows):
    blk = ATT_BLOCK
    key = lax.broadcasted_iota(jnp.int32, (blk, blk), 0)
    qry = lax.broadcasted_iota(jnp.int32, (blk, blk), 1)
    in_cur = key <= qry
    neg_inf = jnp.float32(-jnp.inf)
    st = [_attn_scores(g, q_blk, kv_cur[:, :ATT_KV_W], kv_prev[:, :ATT_KV_W]) for g in range(ATT_KV_HEADS)]
    yield 1
    v_cat = jnp.concatenate([kv_cur[:, ATT_KV_W:], kv_prev[:, ATT_KV_W:]], axis=0).astype(F32)
    vt_all = v_cat.T.astype(BF16)
    for g in range(ATT_KV_HEADS):
        for r in range(Q_CHUNKS_PER_GROUP):
            qs = slice(r * blk, (r + 1) * blk)
            for half in range(2):
                head = (g * Q_CHUNKS_PER_GROUP + r) * 2 + half
                s_cur = st[g][(2 * half) * blk:(2 * half + 1) * blk, qs]
                s_prev = st[g][(2 * half + 1) * blk:(2 * half + 2) * blk, qs]
                s = jnp.where(in_cur, s_cur, jnp.where(has_prev, s_prev, neg_inf))
                sink = sink_ref[head:head + 1, :]
                m = jnp.maximum(jnp.max(s, axis=0, keepdims=True), sink)
                p = jnp.exp(s - m)
                denom = jnp.sum(p, axis=0, keepdims=True) + jnp.exp(sink - m)
                p = p * (1.0 / denom)
                zp = jnp.zeros_like(p)
                pt_scr[g * 2 + half, :blk, qs] = jnp.where(in_cur, p, zp).astype(BF16)
                pt_scr[g * 2 + half, blk:, qs] = jnp.where(in_cur, zp, p).astype(BF16)
        vt = vt_all[g * ATT_HEAD_DIM:(g + 1) * ATT_HEAD_DIM, :]
        ot = [jnp.dot(vt, pt_scr[g * 2 + half], preferred_element_type=F32) for half in range(2)]
        for r in range(Q_CHUNKS_PER_GROUP):
            j = g * Q_CHUNKS_PER_GROUP + r
            qs = slice(r * blk, (r + 1) * blk)
            o_t = jnp.concatenate([ot[0][:, qs], ot[1][:, qs]], axis=0)
            o_ref[rows, j * LANES:(j + 1) * LANES] = o_t.T.astype(BF16)
    yield 0


def _inproj_kernel(blocks_per_seq, x_ref, g_ref, cq_ref, sq_ref, ck_ref, sk_ref, wq_ref, wmid_ref, wgate_ref, wif_ref,
                   sink_ref,
                   att_ref, qkm_ref, vm_ref, om_ref, ga_ref, gm_ref, gi_ref, gf_ref, q_scr, kv_scr, pt_scr):
    tm = x_ref.shape[0]
    blk = ATT_BLOCK
    step = pl.program_id(0)
    slot = step % 2
    q_new, kv_new = q_scr.at[slot], kv_scr.at[slot]
    q_old, kv_old = q_scr.at[1 - slot], kv_scr.at[1 - slot]

    @pl.when(step == 0)
    def _():
        _zero_fill(q_scr)
        _zero_fill(kv_scr)

    h = _rms(x_ref[...], g_ref[...]).astype(BF16)
    kv_new[:blk, :] = kv_old[tm:, :]

    def mixer():
        for k in range(tm // blk):
            rows = slice(k * blk, (k + 1) * blk)
            has_prev = ((step - 1) * (tm // blk) + k) % blocks_per_seq != 0
            yield from _attn_block(q_old[rows, :], kv_old[blk + k * blk:2 * blk + k * blk, :],
                                   kv_old[k * blk:(k + 1) * blk, :], has_prev, sink_ref, pt_scr, att_ref, rows)

    def dense():
        def proj(w_ref, lo, width):
            return jnp.dot(h, w_ref[:, lo:lo + width], preferred_element_type=F32)

        z = proj(wq_ref, 0, ATT_Q_W)
        cq, sq = cq_ref[...], sq_ref[...]
        for j in range(ATT_Q_W // LANES):
            sl = slice(j * LANES, (j + 1) * LANES)
            q_new[:, sl] = _rope(z[:, sl], cq, sq).astype(BF16)
        yield
        z = proj(wmid_ref, _C_KA, 2 * ATT_KV_W)
        ck, sk = ck_ref[...], sk_ref[...]
        for j in range(ATT_KV_W // LANES):
            sl = slice(j * LANES, (j + 1) * LANES)
            kv_new[blk:, sl] = _rope(z[:, sl], ck, sk).astype(BF16)
        kv_new[blk:, ATT_KV_W:] = z[:, ATT_KV_W:].astype(BF16)
        vm_ref[...] = proj(wmid_ref, _C_VM, ML_V_W).astype(BF16)
        yield
        om_ref[...] = proj(wmid_ref, _C_OM, ML_V_W).astype(BF16)
        ga_ref[...] = proj(wgate_ref, 0, D_MODEL).astype(BF16)
        yield
        gm_ref[...] = proj(wgate_ref, D_MODEL, D_MODEL).astype(BF16)
        z = proj(wif_ref, 0, 2 * LANES)
        gi_ref[...] = z[:, :LANES]
        gf_ref[...] = z[:, LANES:]
        qkm_ref[...] = proj(wmid_ref, _C_QKM, 2 * ML_QK_W).astype(BF16)
        yield

    _interleave(mixer(), dense())


def _inproj(x1, g, cq, sq, ck, sk, wq, wmid, wgate, wif, sinks, tm, seq):
    t = x1.shape[0]
    n = t // tm
    tiles_per_seq = seq // tm

    def cur(i):
        return jnp.minimum(i, n - 1)

    def tok(width):
        return pl.BlockSpec((tm, width), lambda i: (cur(i), 0))

    pos = pl.BlockSpec((tm, LANES), lambda i: (cur(i) % tiles_per_seq, 0))
    lagged = pl.BlockSpec((tm, ATT_Q_W), lambda i: (jnp.maximum(i - 1, 0), 0))
    widths = (2 * ML_QK_W, ML_V_W, ML_V_W, D_MODEL, D_MODEL)
    out_shape = [jax.ShapeDtypeStruct((t, ATT_Q_W), BF16)]
    out_shape += [jax.ShapeDtypeStruct((t, w_), BF16) for w_ in widths]
    out_shape += [jax.ShapeDtypeStruct((t, LANES), F32)] * 2
    return pl.pallas_call(
        functools.partial(_inproj_kernel, seq // ATT_BLOCK),
        grid=(n + 1,),
        in_specs=[tok(D_MODEL), _resident((1, D_MODEL)), pos, pos, pos, pos, _resident((D_MODEL, ATT_Q_W)),
                  _resident((D_MODEL, W_MID)), _resident((D_MODEL, 2 * D_MODEL)), _resident((D_MODEL, 2 * LANES)),
                  _resident((ATT_HEADS, LANES))],
        out_specs=[lagged] + [tok(w_) for w_ in widths] + [tok(LANES)] * 2,
        out_shape=out_shape,
        scratch_shapes=[pltpu.VMEM((2, tm, ATT_Q_W), BF16),
                        pltpu.VMEM((2, ATT_BLOCK + tm, 2 * ATT_KV_W), BF16),
                        pltpu.VMEM((2 * ATT_KV_HEADS, 2 * ATT_BLOCK, Q_CHUNKS_PER_GROUP * ATT_BLOCK), BF16)],
        compiler_params=pltpu.CompilerParams(dimension_semantics=("arbitrary",),
                                             vmem_limit_bytes=VMEM_LIMIT_BYTES),
        name="in_proj_attn",
    )(x1, g, cq, sq, ck, sk, wq, wmid, wgate, wif, sinks)


def _scan_rows(x, op, fill):
    n = x.shape[0]
    row = lax.broadcasted_iota(jnp.int32, x.shape, 0)
    shift = 1
    while shift < n:
        prev = jnp.where(row >= shift, pltpu.roll(x, shift, axis=0), fill)
        x = op(x, prev)
        shift *= 2
    return x


def _mlstm_prologue(rows, qk_ref, gi_ref, gf_ref, cw_ref, cb_ref, bi_ref, bf_ref, m_scr, conv_scr):
    L = MLSTM_CHUNK
    tail = CONV_TAIL_ROWS
    x = qk_ref[rows, :].astype(F32)
    conv_scr[tail:, :] = x
    y = cb_ref[...] + cw_ref[MLSTM_CONV - 1:MLSTM_CONV, :] * x
    for j in range(1, MLSTM_CONV):
        y = y + cw_ref[MLSTM_CONV - 1 - j:MLSTM_CONV - j, :] * conv_scr[tail - j:tail - j + L, :]
    conv_scr[:tail, :] = x[L - tail:, :]
    qk = y * _sigmoid(y)

    ig = gi_ref[rows, :] + bi_ref[...]
    fpre = gf_ref[rows, :] + bf_ref[...]
    logf = jnp.minimum(fpre, 0.0) - jnp.log1p(jnp.exp(-jnp.abs(fpre)))
    b = _scan_rows(logf, jnp.add, 0.0)
    g = ig - b
    cm = _scan_rows(g, jnp.maximum, -jnp.inf)
    m_prev = m_scr[0:1, :]
    u = jnp.maximum(cm, m_prev)
    g_max = cm[L - 1:L, :]
    u_last = u[L - 1:L, :]
    gates = dict(u=u, a=jnp.exp(g - g_max), inter=jnp.exp(m_prev - u), emt=jnp.exp(-(b + u)),
                 s_old=jnp.exp(m_prev - u_last), s_new=jnp.exp(g_max - u_last),
                 g_rows=g.T)
    m_scr[0:1, :] = b[L - 1:L, :] + u_last
    return qk, gates


def _mlstm_head(h, rows, qk, gates, v_ref, o_ref, hn_ref, out_ref, ct_scr, n_scr):
    L = MLSTM_CHUNK
    row = lax.broadcasted_iota(jnp.int32, (L, L), 0)
    col = lax.broadcasted_iota(jnp.int32, (L, L), 1)
    causal = col <= row
    qs = slice(h * MLSTM_QK_DIM, (h + 1) * MLSTM_QK_DIM)
    ks = slice(ML_QK_W + h * MLSTM_QK_DIM, ML_QK_W + (h + 1) * MLSTM_QK_DIM)
    vs = slice(h * MLSTM_V_DIM, (h + 1) * MLSTM_V_DIM)
    qh = qk[:, qs]
    kh = qk[:, ks] * (MLSTM_QK_DIM ** -0.5)
    qb = qh.astype(BF16)
    kb = kh.astype(BF16)
    vb = v_ref[rows, vs]
    inter_col = gates["inter"][:, h:h + 1]
    a_col = gates["a"][:, h:h + 1]

    decay = jnp.where(causal, jnp.exp(gates["g_rows"][h:h + 1, :] - gates["u"][:, h:h + 1]), 0.0)
    s = lax.dot_general(qb, kb, (((1,), (1,)), ((), ())), preferred_element_type=F32) * decay
    ct = ct_scr[h]
    n_row = n_scr[h:h + 1, :]
    yield 0
    num = (jnp.dot(s.astype(BF16), vb, preferred_element_type=F32)
           + inter_col * jnp.dot(qb, ct.astype(BF16), preferred_element_type=F32))
    den = jnp.sum(s + inter_col * (qh * n_row), axis=-1, keepdims=True)
    hh = num * (1.0 / jnp.maximum(jnp.abs(den), gates["emt"][:, h:h + 1]))
    hn = _rms(hh, hn_ref[:, vs])
    out_ref[rows, vs] = (_sigmoid(o_ref[rows, vs].astype(F32)) * hn).astype(BF16)
    yield 1 + h % 2
    av = (a_col * vb.astype(F32)).astype(BF16)
    d_ct = lax.dot_general(kb, av, (((0,), (0,)), ((), ())), preferred_element_type=F32)
    d_n = jnp.sum(a_col * kh, axis=0, keepdims=True)
    so = gates["s_old"][:, h:h + 1]
    sn = gates["s_new"][:, h:h + 1]
    ct_scr[h] = so * ct + sn * d_ct
    n_scr[h:h + 1, :] = so * n_row + sn * d_n
    yield 1


def _out_kernel(tiles_per_seq, n_tiles, att_ref, ga_ref, gm_ref, x1_ref, qk_ref, v_ref, o_ref, gi_ref, gf_ref,
                cw_ref, cb_ref, bi_ref, bf_ref, hn_ref, watt_ref, wml_ref, wout_ref, g2_ref, wg_ref, wu_ref,
                wd_ref, gfin_ref, out_ref, hm_scr, ct_scr, n_scr, m_scr, conv_scr):
    tm = x1_ref.shape[0]
    L = MLSTM_CHUNK
    step = pl.program_id(0)
    slot = step % 2
    hm_new, hm_old = hm_scr.at[slot], hm_scr.at[1 - slot]

    @pl.when(step == 0)
    def _():
        _zero_fill(hm_scr)

    @pl.when(jnp.minimum(step, n_tiles - 1) % tiles_per_seq == 0)
    def _():
        _zero_fill(ct_scr)
        n_scr[...] = jnp.zeros_like(n_scr)
        m_scr[...] = jnp.zeros_like(m_scr)
        conv_scr[:CONV_TAIL_ROWS, :] = jnp.zeros((CONV_TAIL_ROWS, conv_scr.shape[1]), F32)

    def mixer():
        for c in range(tm // L):
            rows = slice(c * L, (c + 1) * L)
            qk, gates = _mlstm_prologue(rows, qk_ref, gi_ref, gf_ref, cw_ref, cb_ref, bi_ref, bf_ref, m_scr,
                                        conv_scr)
            yield 3
            for h in range(MLSTM_HEADS):
                yield from _mlstm_head(h, rows, qk, gates, v_ref, o_ref, hn_ref, hm_new, ct_scr, n_scr)

    def dense():
        def dot(x, w):
            return jnp.dot(x, w, preferred_element_type=F32)

        cols = _col_groups(0, D_MODEL)
        ya, ym, x2p = [], [], []
        for lo, hi in cols:
            ya.append(dot(att_ref[...], watt_ref[:, lo:hi]))
            yield
        for lo, hi in cols:
            ym.append(dot(hm_old[...], wml_ref[:, lo:hi]))
            yield
        y = jnp.concatenate(
            [_sigmoid(ga_ref[:, lo:hi].astype(F32)) * ya[j] + _sigmoid(gm_ref[:, lo:hi].astype(F32)) * ym[j]
             for j, (lo, hi) in enumerate(cols)], axis=1).astype(BF16)
        for lo, hi in cols:
            x2p.append(dot(y, wout_ref[:, lo:hi]))
            yield
        x2 = x1_ref[...] + jnp.concatenate(x2p, axis=1)
        h2 = _rms(x2, g2_ref[...]).astype(BF16)
        acc = None
        for clo, chi in FF_CHUNKS:
            groups = _col_groups(clo, chi)
            g = []
            for lo, hi in groups:
                g.append(dot(h2, wg_ref[:, lo:hi]))
                yield
            a = []
            for j, (lo, hi) in enumerate(groups):
                u = dot(h2, wu_ref[:, lo:hi])
                a.append((g[j] * _sigmoid(g[j]) * u).astype(BF16))
                yield
            a = jnp.concatenate(a, axis=1)
            part = []
            for lo, hi in cols:
                part.append(dot(a, wd_ref[clo:chi, lo:hi]))
                yield
            acc = part if acc is None else [p + q for p, q in zip(acc, part)]
        x3 = x2 + 0.5 * jnp.concatenate(acc, axis=1)
        out_ref[...] = _rms(x3, gfin_ref[...])
        yield

    _interleave(mixer(), dense())


def _out(att, ga, gm, x1, qkm, vm, om, gi, gf, cw, cb, bi, bf, hn, watt, wml, wout, g2, wg, wu, wd, gfin, tm, seq):
    t = x1.shape[0]
    n = t // tm

    def cur(width):
        return pl.BlockSpec((tm, width), lambda i: (jnp.minimum(i, n - 1), 0))

    lagged = pl.BlockSpec((tm, D_MODEL), lambda i: (jnp.maximum(i - 1, 0), 0))
    sq = _resident((D_MODEL, D_MODEL))
    vec = _resident((1, D_MODEL))
    return pl.pallas_call(
        functools.partial(_out_kernel, seq // tm, n),
        grid=(n + 1,),
        in_specs=[lagged, lagged, lagged, lagged,
                  cur(2 * ML_QK_W), cur(ML_V_W), cur(ML_V_W), cur(LANES), cur(LANES),
                  _resident((MLSTM_CONV, 2 * ML_QK_W)), _resident((1, 2 * ML_QK_W)),
                  _resident((1, LANES)), _resident((1, LANES)), vec,
                  sq, sq, sq, vec, _resident((D_MODEL, D_FF)), _resident((D_MODEL, D_FF)),
                  _resident((D_FF, D_MODEL)), vec],
        out_specs=lagged,
        out_shape=jax.ShapeDtypeStruct((t, D_MODEL), F32),
        scratch_shapes=[pltpu.VMEM((2, tm, ML_V_W), BF16),
                        pltpu.VMEM((MLSTM_HEADS, MLSTM_QK_DIM, MLSTM_V_DIM), F32),
                        pltpu.VMEM((SUBLANES, LANES), F32),
                        pltpu.VMEM((SUBLANES, LANES), F32),
                        pltpu.VMEM((CONV_TAIL_ROWS + MLSTM_CHUNK, 2 * ML_QK_W), F32)],
        compiler_params=pltpu.CompilerParams(dimension_semantics=("arbitrary",),
                                             vmem_limit_bytes=VMEM_LIMIT_BYTES),
        name="mlstm_merge_ffn2",
    )(att, ga, gm, x1, qkm, vm, om, gi, gf, cw, cb, bi, bf, hn, watt, wml, wout, g2, wg, wu, wd, gfin)


def _rope_tables(seq):
    half = ATT_HEAD_DIM // 2
    pos = jnp.arange(seq, dtype=F32)
    inv_freq = ROPE_THETA ** (-jnp.arange(half, dtype=F32) / half)
    ang = pos[:, None] * inv_freq[None, :]
    cos, sin = jnp.cos(ang), jnp.sin(ang)
    reps = LANES // ATT_HEAD_DIM
    cos = jnp.tile(jnp.concatenate([cos, cos], axis=-1), (1, reps))
    sin = jnp.tile(jnp.concatenate([-sin, sin], axis=-1), (1, reps))
    return cos, sin


def _pad_lanes(v):
    return jnp.pad(v, ((0, 0), (0, LANES - v.shape[-1])))


def _layer(x2d, batch, seq, p):
    (ffn1_norm, ffn1_w_gate, ffn1_w_up, ffn1_w_down, mix_norm, w_in, b_i, b_f, attn_sinks, conv_w, conv_b,
     head_norm, w_att, w_mlstm, w_out, ffn2_norm, ffn2_w_gate, ffn2_w_up, ffn2_w_down, final_norm) = p
    tm = min(TOKEN_TILE, seq)
    assert seq % tm == 0 and seq % ATT_BLOCK == 0 and seq % MLSTM_CHUNK == 0

    cos, sin = _rope_tables(seq)
    q_scale = ATT_HEAD_DIM ** -0.5
    sinks = jnp.broadcast_to(attn_sinks[:, None], (ATT_HEADS, LANES)).astype(F32)

    x1, (w_q, w_mid, w_gate, w_if), (watt, wml, wout, wg2, wu2, wd2) = _ffn1(
        x2d, ffn1_norm[None], ffn1_w_gate.astype(BF16), ffn1_w_up.astype(BF16), ffn1_w_down.astype(BF16),
        w_in, (w_att, w_mlstm, w_out, ffn2_w_gate, ffn2_w_up, ffn2_w_down), tm)
    att, qkm, vm, om, ga, gm, gi, gf = _inproj(
        x1, mix_norm[None], cos * q_scale, sin * q_scale, cos, sin, w_q, w_mid, w_gate, w_if, sinks, tm, seq)
    return _out(att, ga, gm, x1, qkm, vm, om, gi, gf, conv_w, conv_b[None], _pad_lanes(b_i[None]),
                _pad_lanes(b_f[None]), head_norm[None], watt, wml, wout, ffn2_norm[None], wg2, wu2, wd2,
                final_norm[None], tm, seq)


def kernel(x, ffn1_norm, ffn1_w_gate, ffn1_w_up, ffn1_w_down, mix_norm, w_in, b_i, b_f, attn_sinks, conv_w,
           conv_b, head_norm, w_att, w_mlstm, w_out, ffn2_norm, ffn2_w_gate, ffn2_w_up, ffn2_w_down, final_norm):
    batch, seq, d = x.shape
    assert d == D_MODEL and ffn1_norm.shape[0] == 1, "single-layer kernel"
    per_layer = (ffn1_norm, ffn1_w_gate, ffn1_w_up, ffn1_w_down, mix_norm, w_in, b_i, b_f, attn_sinks, conv_w,
                 conv_b, head_norm, w_att, w_mlstm, w_out, ffn2_norm, ffn2_w_gate, ffn2_w_up, ffn2_w_down)
    params = tuple(a[0] for a in per_layer) + (final_norm,)
    out = _layer(x.reshape(batch * seq, d), batch, seq, params)
    return out.reshape(batch, seq, d)
```

```python
import functools

import jax
import jax.numpy as jnp
import numpy as np
from jax import lax
from jax.experimental import pallas as pl
from jax.experimental.pallas import tpu as pltpu

F32 = jnp.float32
BF16 = jnp.bfloat16

D_MODEL = 1024
ATT_HEADS = 16
ATT_KV_HEADS = 4
ATT_HEAD_DIM = 64
ATT_BLOCK = 128
ROPE_THETA = 10000.0
MLSTM_HEADS = 4
MLSTM_V_DIM = D_MODEL // MLSTM_HEADS
MLSTM_QK_DIM = MLSTM_V_DIM // 2
MLSTM_CHUNK = 128
MLSTM_CONV = 4
D_FF = 2816
RMS_EPS = 1e-5
NEG_LOG2E = -1.4426950408889634

ATT_Q_W = ATT_HEADS * ATT_HEAD_DIM
ATT_KV_W = ATT_KV_HEADS * ATT_HEAD_DIM
ML_QK_W = MLSTM_HEADS * MLSTM_QK_DIM
ML_V_W = MLSTM_HEADS * MLSTM_V_DIM

LANES = 128
MXU_COLS = 256
SUBLANES = 8
BF16_SUBLANES = 16
CONV_TAIL_ROWS = SUBLANES
VMEM_LIMIT_BYTES = 60 * 1024 * 1024

TOKEN_TILE = 512
FF_CHUNKS = ((0, 768), (768, 1536), (1536, 2304), (2304, D_FF))

Q_CHUNKS_PER_GROUP = ATT_HEADS // ATT_KV_HEADS * ATT_HEAD_DIM // LANES

_C_KA = 0
_C_VA = _C_KA + ATT_KV_W
_C_QKM = _C_VA + ATT_KV_W
_C_VM = _C_QKM + 2 * ML_QK_W
_C_OM = _C_VM + ML_V_W
W_MID = _C_OM + ML_V_W

_IN_OFFS = tuple(int(v) for v in np.cumsum((0, ATT_Q_W, ATT_KV_W, ATT_KV_W, ML_QK_W, ML_QK_W, ML_V_W, ML_V_W,
                                            MLSTM_HEADS, MLSTM_HEADS, D_MODEL, D_MODEL)))
IN_PROJ_W = _IN_OFFS[-1]


def _resident(shape):
    return pl.BlockSpec(shape, lambda *_: (0,) * len(shape), pipeline_mode=pl.Buffered(1))


def _col_groups(lo, hi):
    return [(c, min(c + MXU_COLS, hi)) for c in range(lo, hi, MXU_COLS)]


def _interleave(mixer, dense):
    for n_dense in mixer:
        for _ in range(n_dense):
            next(dense, None)
    for _ in dense:
        pass


def _rms(x, g):
    ms = jnp.mean(x * x, axis=-1, keepdims=True)
    return x * lax.rsqrt(ms + RMS_EPS) * g


def _sigmoid(x):
    return 1.0 / (1.0 + jnp.exp2(x * NEG_LOG2E))


def _swiglu(h, wg_ref, wu_ref, wd_ref):
    acc = None
    for lo, hi in FF_CHUNKS:
        g = jnp.dot(h, wg_ref[:, lo:hi], preferred_element_type=F32)
        u = jnp.dot(h, wu_ref[:, lo:hi], preferred_element_type=F32)
        a = (g * _sigmoid(g) * u).astype(BF16)
        d = jnp.dot(a, wd_ref[lo:hi, :], preferred_element_type=F32)
        acc = d if acc is None else acc + d
    return acc


def _ffn1_kernel(x_ref, g_ref, wg_ref, wu_ref, wd_ref, win_ref, *refs):
    n_plain = (len(refs) - 5) // 2
    plain_in, (o_ref, wq_ref, wmid_ref, wgate_ref, wif_ref), plain_out = (
        refs[:n_plain], refs[n_plain:n_plain + 5], refs[n_plain + 5:])
    x = x_ref[...]
    h = _rms(x, g_ref[...]).astype(BF16)
    o_ref[...] = x + 0.5 * _swiglu(h, wg_ref, wu_ref, wd_ref)

    for src, dst in zip(plain_in, plain_out):
        dst[...] = src[...].astype(BF16)
    w = win_ref[...]
    wq_ref[...] = w[:, _IN_OFFS[0]:_IN_OFFS[1]].astype(BF16)
    wmid_ref[...] = w[:, _IN_OFFS[1]:_IN_OFFS[7]].astype(BF16)
    wgate_ref[...] = w[:, _IN_OFFS[9]:].astype(BF16)
    pad = jnp.zeros((w.shape[0], LANES - MLSTM_HEADS), F32)
    wif_ref[...] = jnp.concatenate([w[:, _IN_OFFS[7]:_IN_OFFS[8]], pad, w[:, _IN_OFFS[8]:_IN_OFFS[9]], pad],
                                   axis=1).astype(BF16)


def _row_block(rows, n_steps):
    return next(r for r in range(BF16_SUBLANES, rows + 1, BF16_SUBLANES)
                if rows % r == 0 and r * n_steps >= rows)


def _ffn1(x, g, wg, wu, wd, w_in, later_weights, tm):
    t = x.shape[0]
    n = t // tm
    tok = pl.BlockSpec((tm, D_MODEL), lambda i: (i, 0))

    def streamed(w):
        r = _row_block(w.shape[0], n)
        last = w.shape[0] // r - 1
        return lambda width: pl.BlockSpec((r, width), lambda i: (jnp.minimum(i, last), 0))

    in_spec = streamed(w_in)
    plain = [streamed(w) for w in later_weights]
    in_widths = (ATT_Q_W, W_MID, 2 * D_MODEL, 2 * LANES)
    out = pl.pallas_call(
        _ffn1_kernel,
        grid=(n,),
        in_specs=[tok, _resident((1, D_MODEL)), _resident((D_MODEL, D_FF)), _resident((D_MODEL, D_FF)),
                  _resident((D_FF, D_MODEL)), in_spec(IN_PROJ_W)]
        + [spec(w.shape[1]) for spec, w in zip(plain, later_weights)],
        out_specs=[tok] + [in_spec(w_) for w_ in in_widths]
        + [spec(w.shape[1]) for spec, w in zip(plain, later_weights)],
        out_shape=[jax.ShapeDtypeStruct((t, D_MODEL), F32)]
        + [jax.ShapeDtypeStruct((D_MODEL, w_), BF16) for w_ in in_widths]
        + [jax.ShapeDtypeStruct(w.shape, BF16) for w in later_weights],
        compiler_params=pltpu.CompilerParams(dimension_semantics=("arbitrary",),
                                             vmem_limit_bytes=VMEM_LIMIT_BYTES),
        name="ffn1",
    )(x, g, wg, wu, wd, w_in, *later_weights)
    return out[0], out[1:5], out[5:]


def _rope(x, cos, sin_signed):
    lane = lax.broadcasted_iota(jnp.int32, x.shape, 1)
    half = ATT_HEAD_DIM // 2
    partner = jnp.where(lane % ATT_HEAD_DIM < half, pltpu.roll(x, LANES - half, axis=1),
                        pltpu.roll(x, half, axis=1))
    return x * cos + partner * sin_signed


def _attn_scores(g, q_blk, k_cur, k_prev):
    low = lax.broadcasted_iota(jnp.int32, (ATT_BLOCK, LANES), 1) < ATT_HEAD_DIM
    c, in_high = divmod(g, 2)
    ksl = slice(c * LANES, (c + 1) * LANES)
    parts = []
    for k in (k_cur, k_prev):
        x = k[:, ksl].astype(F32)
        other = pltpu.roll(x, ATT_HEAD_DIM, axis=1)
        in_low_lanes, in_high_lanes = (other, x) if in_high else (x, other)
        parts.append((jnp.where(low, in_low_lanes, 0.0), jnp.where(low, 0.0, in_high_lanes)))
    kstack = jnp.concatenate([parts[0][0], parts[1][0], parts[0][1], parts[1][1]], axis=0).astype(BF16)
    qstack = jnp.concatenate(
        [q_blk[:, (g * Q_CHUNKS_PER_GROUP + r) * LANES:(g * Q_CHUNKS_PER_GROUP + r + 1) * LANES]
         for r in range(Q_CHUNKS_PER_GROUP)], axis=0)
    return lax.dot_general(kstack, qstack, (((1,), (1,)), ((), ())), preferred_element_type=F32)


def _attn_block(q_blk, kv_cur, kv_prev, has_prev, sink_ref, pt_scr, o_ref, rows):
    blk = ATT_BLOCK
    key = lax.broadcasted_iota(jnp.int32, (blk, blk), 0)
    qry = lax.broadcasted_iota(jnp.int32, (blk, blk), 1)
    in_cur = key <= qry
    neg_inf = jnp.float32(-jnp.inf)
    st = [_attn_scores(g, q_blk, kv_cur[:, :ATT_KV_W], kv_prev[:, :ATT_KV_W]) for g in range(ATT_KV_HEADS)]
    yield 1
    v_cat = jnp.concatenate([kv_cur[:, ATT_KV_W:], kv_prev[:, ATT_KV_W:]], axis=0).astype(F32)
    vt_all = v_cat.T.astype(BF16)
    for g in range(ATT_KV_HEADS):
        for r in range(Q_CHUNKS_PER_GROUP):
            qs = slice(r * blk, (r + 1) * blk)
            for half in range(2):
                head = (g * Q_CHUNKS_PER_GROUP + r) * 2 + half
                s_cur = st[g][(2 * half) * blk:(2 * half + 1) * blk, qs]
                s_prev = st[g][(2 * half + 1) * blk:(2 * half + 2) * blk, qs]
                s = jnp.where(in_cur, s_cur, jnp.where(has_prev, s_prev, neg_inf))
                sink = sink_ref[head:head + 1, :]
                m = jnp.maximum(jnp.max(s, axis=0, keepdims=True), sink)
                p = jnp.exp(s - m)
                denom = jnp.sum(p, axis=0, keepdims=True) + jnp.exp(sink - m)
                p = p * (1.0 / denom)
                zp = jnp.zeros_like(p)
                pt_scr[g * 2 + half, :blk, qs] = jnp.where(in_cur, p, zp).astype(BF16)
                pt_scr[g * 2 + half, blk:, qs] = jnp.where(in_cur, zp, p).astype(BF16)
        vt = vt_all[g * ATT_HEAD_DIM:(g + 1) * ATT_HEAD_DIM, :]
        ot = [jnp.dot(vt, pt_scr[g * 2 + half], preferred_element_type=F32) for half in range(2)]
        for r in range(Q_CHUNKS_PER_GROUP):
            j = g * Q_CHUNKS_PER_GROUP + r
            qs = slice(r * blk, (r + 1) * blk)
            o_t = jnp.concatenate([ot[0][:, qs], ot[1][:, qs]], axis=0)
            o_ref[rows, j * LANES:(j + 1) * LANES] = o_t.T.astype(BF16)
    yield 0


def _inproj_kernel(blocks_per_seq, x_ref, g_ref, cq_ref, sq_ref, ck_ref, sk_ref, wq_ref, wmid_ref, wgate_ref, wif_ref,
                   sink_ref,
                   att_ref, qkm_ref, vm_ref, om_ref, ga_ref, gm_ref, gi_ref, gf_ref, q_scr, kv_scr, pt_scr):
    tm = x_ref.shape[0]
    blk = ATT_BLOCK
    step = pl.program_id(0)
    slot = step % 2
    q_new, kv_new = q_scr.at[slot], kv_scr.at[slot]
    q_old, kv_old = q_scr.at[1 - slot], kv_scr.at[1 - slot]

    @pl.when(step == 0)
    def _():
        q_scr[...] = jnp.zeros_like(q_scr)
        kv_scr[...] = jnp.zeros_like(kv_scr)

    h = _rms(x_ref[...], g_ref[...]).astype(BF16)
    kv_new[:blk, :] = kv_old[tm:, :]

    def mixer():
        for k in range(tm // blk):
            rows = slice(k * blk, (k + 1) * blk)
            has_prev = ((step - 1) * (tm // blk) + k) % blocks_per_seq != 0
            yield from _attn_block(q_old[rows, :], kv_old[blk + k * blk:2 * blk + k * blk, :],
                                   kv_old[k * blk:(k + 1) * blk, :], has_prev, sink_ref, pt_scr, att_ref, rows)

    def dense():
        def proj(w_ref, lo, width):
            return jnp.dot(h, w_ref[:, lo:lo + width], preferred_element_type=F32)

        z = proj(wq_ref, 0, ATT_Q_W)
        cq, sq = cq_ref[...], sq_ref[...]
        for j in range(ATT_Q_W // LANES):
            sl = slice(j * LANES, (j + 1) * LANES)
            q_new[:, sl] = _rope(z[:, sl], cq, sq).astype(BF16)
        yield
        z = proj(wmid_ref, _C_KA, 2 * ATT_KV_W)
        ck, sk = ck_ref[...], sk_ref[...]
        for j in range(ATT_KV_W // LANES):
            sl = slice(j * LANES, (j + 1) * LANES)
            kv_new[blk:, sl] = _rope(z[:, sl], ck, sk).astype(BF16)
        kv_new[blk:, ATT_KV_W:] = z[:, ATT_KV_W:].astype(BF16)
        vm_ref[...] = proj(wmid_ref, _C_VM, ML_V_W).astype(BF16)
        yield
        om_ref[...] = proj(wmid_ref, _C_OM, ML_V_W).astype(BF16)
        ga_ref[...] = proj(wgate_ref, 0, D_MODEL).astype(BF16)
        yield
        gm_ref[...] = proj(wgate_ref, D_MODEL, D_MODEL).astype(BF16)
        z = proj(wif_ref, 0, 2 * LANES)
        gi_ref[...] = z[:, :LANES]
        gf_ref[...] = z[:, LANES:]
        qkm_ref[...] = proj(wmid_ref, _C_QKM, 2 * ML_QK_W).astype(BF16)
        yield

    _interleave(mixer(), dense())


def _inproj(x1, g, cq, sq, ck, sk, wq, wmid, wgate, wif, sinks, tm, seq):
    t = x1.shape[0]
    n = t // tm
    tiles_per_seq = seq // tm

    def cur(i):
        return jnp.minimum(i, n - 1)

    def tok(width):
        return pl.BlockSpec((tm, width), lambda i: (cur(i), 0))

    pos = pl.BlockSpec((tm, LANES), lambda i: (cur(i) % tiles_per_seq, 0))
    lagged = pl.BlockSpec((tm, ATT_Q_W), lambda i: (jnp.maximum(i - 1, 0), 0))
    widths = (2 * ML_QK_W, ML_V_W, ML_V_W, D_MODEL, D_MODEL)
    out_shape = [jax.ShapeDtypeStruct((t, ATT_Q_W), BF16)]
    out_shape += [jax.ShapeDtypeStruct((t, w_), BF16) for w_ in widths]
    out_shape += [jax.ShapeDtypeStruct((t, LANES), F32)] * 2
    return pl.pallas_call(
        functools.partial(_inproj_kernel, seq // ATT_BLOCK),
        grid=(n + 1,),
        in_specs=[tok(D_MODEL), _resident((1, D_MODEL)), pos, pos, pos, pos, _resident((D_MODEL, ATT_Q_W)),
                  _resident((D_MODEL, W_MID)), _resident((D_MODEL, 2 * D_MODEL)), _resident((D_MODEL, 2 * LANES)),
                  _resident((ATT_HEADS, LANES))],
        out_specs=[lagged] + [tok(w_) for w_ in widths] + [tok(LANES)] * 2,
        out_shape=out_shape,
        scratch_shapes=[pltpu.VMEM((2, tm, ATT_Q_W), BF16),
                        pltpu.VMEM((2, ATT_BLOCK + tm, 2 * ATT_KV_W), BF16),
                        pltpu.VMEM((2 * ATT_KV_HEADS, 2 * ATT_BLOCK, Q_CHUNKS_PER_GROUP * ATT_BLOCK), BF16)],
        compiler_params=pltpu.CompilerParams(dimension_semantics=("arbitrary",),
                                             vmem_limit_bytes=VMEM_LIMIT_BYTES),
        name="in_proj_attn",
    )(x1, g, cq, sq, ck, sk, wq, wmid, wgate, wif, sinks)


def _scan_rows(x, op, fill):
    n = x.shape[0]
    row = lax.broadcasted_iota(jnp.int32, x.shape, 0)
    shift = 1
    while shift < n:
        prev = jnp.where(row >= shift, pltpu.roll(x, shift, axis=0), fill)
        x = op(x, prev)
        shift *= 2
    return x


def _mlstm_prologue(rows, qk_ref, gi_ref, gf_ref, cw_ref, cb_ref, bi_ref, bf_ref, m_scr, conv_scr):
    L = MLSTM_CHUNK
    tail = CONV_TAIL_ROWS
    x = qk_ref[rows, :].astype(F32)
    conv_scr[tail:, :] = x
    y = cb_ref[...] + cw_ref[MLSTM_CONV - 1:MLSTM_CONV, :] * x
    for j in range(1, MLSTM_CONV):
        y = y + cw_ref[MLSTM_CONV - 1 - j:MLSTM_CONV - j, :] * conv_scr[tail - j:tail - j + L, :]
    conv_scr[:tail, :] = x[L - tail:, :]
    qk = y * _sigmoid(y)

    ig = gi_ref[rows, :] + bi_ref[...]
    fpre = gf_ref[rows, :] + bf_ref[...]
    logf = jnp.minimum(fpre, 0.0) - jnp.log1p(jnp.exp(-jnp.abs(fpre)))
    b = _scan_rows(logf, jnp.add, 0.0)
    g = ig - b
    cm = _scan_rows(g, jnp.maximum, -jnp.inf)
    m_prev = m_scr[0:1, :]
    u = jnp.maximum(cm, m_prev)
    g_max = cm[L - 1:L, :]
    u_last = u[L - 1:L, :]
    gates = dict(u=u, a=jnp.exp(g - g_max), inter=jnp.exp(m_prev - u), emt=jnp.exp(-(b + u)),
                 s_old=jnp.exp(m_prev - u_last), s_new=jnp.exp(g_max - u_last),
                 g_rows=g.T)
    m_scr[0:1, :] = b[L - 1:L, :] + u_last
    return qk, gates


def _mlstm_head(h, rows, qk, gates, v_ref, o_ref, hn_ref, out_ref, ct_scr, n_scr):
    L = MLSTM_CHUNK
    row = lax.broadcasted_iota(jnp.int32, (L, L), 0)
    col = lax.broadcasted_iota(jnp.int32, (L, L), 1)
    causal = col <= row
    qs = slice(h * MLSTM_QK_DIM, (h + 1) * MLSTM_QK_DIM)
    ks = slice(ML_QK_W + h * MLSTM_QK_DIM, ML_QK_W + (h + 1) * MLSTM_QK_DIM)
    vs = slice(h * MLSTM_V_DIM, (h + 1) * MLSTM_V_DIM)
    qh = qk[:, qs]
    kh = qk[:, ks] * (MLSTM_QK_DIM ** -0.5)
    qb = qh.astype(BF16)
    kb = kh.astype(BF16)
    vb = v_ref[rows, vs]
    inter_col = gates["inter"][:, h:h + 1]
    a_col = gates["a"][:, h:h + 1]

    decay = jnp.where(causal, jnp.exp(gates["g_rows"][h:h + 1, :] - gates["u"][:, h:h + 1]), 0.0)
    s = lax.dot_general(qb, kb, (((1,), (1,)), ((), ())), preferred_element_type=F32) * decay
    ct = ct_scr[h]
    n_row = n_scr[h:h + 1, :]
    yield 1
    num = (jnp.dot(s.astype(BF16), vb, preferred_element_type=F32)
           + inter_col * jnp.dot(qb, ct.astype(BF16), preferred_element_type=F32))
    den = jnp.sum(s + inter_col * (qh * n_row), axis=-1, keepdims=True)
    hh = num * (1.0 / jnp.maximum(jnp.abs(den), gates["emt"][:, h:h + 1]))
    hn = _rms(hh, hn_ref[:, vs])
    out_ref[rows, vs] = (_sigmoid(o_ref[rows, vs].astype(F32)) * hn).astype(BF16)
    yield 1
    av = (a_col * vb.astype(F32)).astype(BF16)
    d_ct = lax.dot_general(kb, av, (((0,), (0,)), ((), ())), preferred_element_type=F32)
    d_n = jnp.sum(a_col * kh, axis=0, keepdims=True)
    so = gates["s_old"][:, h:h + 1]
    sn = gates["s_new"][:, h:h + 1]
    ct_scr[h] = so * ct + sn * d_ct
    n_scr[h:h + 1, :] = so * n_row + sn * d_n
    yield 1


def _out_kernel(tiles_per_seq, n_tiles, att_ref, ga_ref, gm_ref, x1_ref, qk_ref, v_ref, o_ref, gi_ref, gf_ref,
                cw_ref, cb_ref, bi_ref, bf_ref, hn_ref, watt_ref, wml_ref, wout_ref, g2_ref, wg_ref, wu_ref,
                wd_ref, gfin_ref, out_ref, hm_scr, ct_scr, n_scr, m_scr, conv_scr):
    tm = x1_ref.shape[0]
    L = MLSTM_CHUNK
    step = pl.program_id(0)
    slot = step % 2
    hm_new, hm_old = hm_scr.at[slot], hm_scr.at[1 - slot]

    @pl.when(step == 0)
    def _():
        hm_scr[...] = jnp.zeros_like(hm_scr)

    @pl.when(jnp.minimum(step, n_tiles - 1) % tiles_per_seq == 0)
    def _():
        ct_scr[...] = jnp.zeros_like(ct_scr)
        n_scr[...] = jnp.zeros_like(n_scr)
        m_scr[...] = jnp.zeros_like(m_scr)
        conv_scr[:CONV_TAIL_ROWS, :] = jnp.zeros((CONV_TAIL_ROWS, conv_scr.shape[1]), F32)

    def mixer():
        for c in range(tm // L):
            rows = slice(c * L, (c + 1) * L)
            qk, gates = _mlstm_prologue(rows, qk_ref, gi_ref, gf_ref, cw_ref, cb_ref, bi_ref, bf_ref, m_scr,
                                        conv_scr)
            yield 1
            for h in range(MLSTM_HEADS):
                yield from _mlstm_head(h, rows, qk, gates, v_ref, o_ref, hn_ref, hm_new, ct_scr, n_scr)

    def dense():
        def dot(x, w):
            return jnp.dot(x, w, preferred_element_type=F32)

        cols = _col_groups(0, D_MODEL)
        ya, ym, x2p = [], [], []
        for lo, hi in cols:
            ya.append(dot(att_ref[...], watt_ref[:, lo:hi]))
            yield
        for lo, hi in cols:
            ym.append(dot(hm_old[...], wml_ref[:, lo:hi]))
            yield
        y = jnp.concatenate(
            [_sigmoid(ga_ref[:, lo:hi].astype(F32)) * ya[j] + _sigmoid(gm_ref[:, lo:hi].astype(F32)) * ym[j]
             for j, (lo, hi) in enumerate(cols)], axis=1).astype(BF16)
        for lo, hi in cols:
            x2p.append(dot(y, wout_ref[:, lo:hi]))
            yield
        x2 = x1_ref[...] + jnp.concatenate(x2p, axis=1)
        h2 = _rms(x2, g2_ref[...]).astype(BF16)
        acc = None
        for clo, chi in FF_CHUNKS:
            groups = _col_groups(clo, chi)
            g = []
            for lo, hi in groups:
                g.append(dot(h2, wg_ref[:, lo:hi]))
                yield
            a = []
            for j, (lo, hi) in enumerate(groups):
                u = dot(h2, wu_ref[:, lo:hi])
                a.append((g[j] * _sigmoid(g[j]) * u).astype(BF16))
                yield
            a = jnp.concatenate(a, axis=1)
            part = []
            for lo, hi in cols:
                part.append(dot(a, wd_ref[clo:chi, lo:hi]))
                yield
            acc = part if acc is None else [p + q for p, q in zip(acc, part)]
        x3 = x2 + 0.5 * jnp.concatenate(acc, axis=1)
        out_ref[...] = _rms(x3, gfin_ref[...])
        yield

    _interleave(mixer(), dense())


def _out(att, ga, gm, x1, qkm, vm, om, gi, gf, cw, cb, bi, bf, hn, watt, wml, wout, g2, wg, wu, wd, gfin, tm, seq):
    t = x1.shape[0]
    n = t // tm

    def cur(width):
        return pl.BlockSpec((tm, width), lambda i: (jnp.minimum(i, n - 1), 0))

    lagged = pl.BlockSpec((tm, D_MODEL), lambda i: (jnp.maximum(i - 1, 0), 0))
    sq = _resident((D_MODEL, D_MODEL))
    vec = _resident((1, D_MODEL))
    return pl.pallas_call(
        functools.partial(_out_kernel, seq // tm, n),
        grid=(n + 1,),
        in_specs=[lagged, lagged, lagged, lagged,
                  cur(2 * ML_QK_W), cur(ML_V_W), cur(ML_V_W), cur(LANES), cur(LANES),
                  _resident((MLSTM_CONV, 2 * ML_QK_W)), _resident((1, 2 * ML_QK_W)),
                  _resident((1, LANES)), _resident((1, LANES)), vec,
                  sq, sq, sq, vec, _resident((D_MODEL, D_FF)), _resident((D_MODEL, D_FF)),
                  _resident((D_FF, D_MODEL)), vec],
        out_specs=lagged,
        out_shape=jax.ShapeDtypeStruct((t, D_MODEL), F32),
        scratch_shapes=[pltpu.VMEM((2, tm, ML_V_W), BF16),
                        pltpu.VMEM((MLSTM_HEADS, MLSTM_QK_DIM, MLSTM_V_DIM), F32),
                        pltpu.VMEM((SUBLANES, LANES), F32),
                        pltpu.VMEM((SUBLANES, LANES), F32),
                        pltpu.VMEM((CONV_TAIL_ROWS + MLSTM_CHUNK, 2 * ML_QK_W), F32)],
        compiler_params=pltpu.CompilerParams(dimension_semantics=("arbitrary",),
                                             vmem_limit_bytes=VMEM_LIMIT_BYTES),
        name="mlstm_merge_ffn2",
    )(att, ga, gm, x1, qkm, vm, om, gi, gf, cw, cb, bi, bf, hn, watt, wml, wout, g2, wg, wu, wd, gfin)


def _rope_tables(seq):
    half = ATT_HEAD_DIM // 2
    pos = jnp.arange(seq, dtype=F32)
    inv_freq = ROPE_THETA ** (-jnp.arange(half, dtype=F32) / half)
    ang = pos[:, None] * inv_freq[None, :]
    cos, sin = jnp.cos(ang), jnp.sin(ang)
    reps = LANES // ATT_HEAD_DIM
    cos = jnp.tile(jnp.concatenate([cos, cos], axis=-1), (1, reps))
    sin = jnp.tile(jnp.concatenate([-sin, sin], axis=-1), (1, reps))
    return cos, sin


def _pad_lanes(v):
    return jnp.pad(v, ((0, 0), (0, LANES - v.shape[-1])))


def _layer(x2d, batch, seq, p):
    (ffn1_norm, ffn1_w_gate, ffn1_w_up, ffn1_w_down, mix_norm, w_in, b_i, b_f, attn_sinks, conv_w, conv_b,
     head_norm, w_att, w_mlstm, w_out, ffn2_norm, ffn2_w_gate, ffn2_w_up, ffn2_w_down, final_norm) = p
    tm = min(TOKEN_TILE, seq)
    assert seq % tm == 0 and seq % ATT_BLOCK == 0 and seq % MLSTM_CHUNK == 0

    cos, sin = _rope_tables(seq)
    q_scale = ATT_HEAD_DIM ** -0.5
    sinks = jnp.broadcast_to(attn_sinks[:, None], (ATT_HEADS, LANES)).astype(F32)

    x1, (w_q, w_mid, w_gate, w_if), (watt, wml, wout, wg2, wu2, wd2) = _ffn1(
        x2d, ffn1_norm[None], ffn1_w_gate.astype(BF16), ffn1_w_up.astype(BF16), ffn1_w_down.astype(BF16),
        w_in, (w_att, w_mlstm, w_out, ffn2_w_gate, ffn2_w_up, ffn2_w_down), tm)
    att, qkm, vm, om, ga, gm, gi, gf = _inproj(
        x1, mix_norm[None], cos * q_scale, sin * q_scale, cos, sin, w_q, w_mid, w_gate, w_if, sinks, tm, seq)
    return _out(att, ga, gm, x1, qkm, vm, om, gi, gf, conv_w, conv_b[None], _pad_lanes(b_i[None]),
                _pad_lanes(b_f[None]), head_norm[None], watt, wml, wout, ffn2_norm[None], wg2, wu2, wd2,
                final_norm[None], tm, seq)


def kernel(x, ffn1_norm, ffn1_w_gate, ffn1_w_up, ffn1_w_down, mix_norm, w_in, b_i, b_f, attn_sinks, conv_w,
           conv_b, head_norm, w_att, w_mlstm, w_out, ffn2_norm, ffn2_w_gate, ffn2_w_up, ffn2_w_down, final_norm):
    batch, seq, d = x.shape
    assert d == D_MODEL and ffn1_norm.shape[0] == 1, "single-layer kernel"
    per_layer = (ffn1_norm, ffn1_w_gate, ffn1_w_up, ffn1_w_down, mix_norm, w_in, b_i, b_f, attn_sinks, conv_w,
                 conv_b, head_norm, w_att, w_mlstm, w_out, ffn2_norm, ffn2_w_gate, ffn2_w_up, ffn2_w_down)
    params = tuple(a[0] for a in per_layer) + (final_norm,)
    out = _layer(x.reshape(batch * seq, d), batch, seq, params)
    return out.reshape(batch, seq, d)
```

```python
import functools

import jax
import jax.numpy as jnp
import numpy as np
from jax import lax
from jax.experimental import pallas as pl
from jax.experimental.pallas import tpu as pltpu

F32 = jnp.float32
BF16 = jnp.bfloat16

D_MODEL = 1024
ATT_HEADS = 16
ATT_KV_HEADS = 4
ATT_HEAD_DIM = 64
ATT_BLOCK = 128
ROPE_THETA = 10000.0
MLSTM_HEADS = 4
MLSTM_V_DIM = D_MODEL // MLSTM_HEADS
MLSTM_QK_DIM = MLSTM_V_DIM // 2
MLSTM_CHUNK = 128
MLSTM_CONV = 4
D_FF = 2816
RMS_EPS = 1e-5
NEG_LOG2E = -1.4426950408889634

ATT_Q_W = ATT_HEADS * ATT_HEAD_DIM
ATT_KV_W = ATT_KV_HEADS * ATT_HEAD_DIM
ML_QK_W = MLSTM_HEADS * MLSTM_QK_DIM
ML_V_W = MLSTM_HEADS * MLSTM_V_DIM

LANES = 128
MXU_COLS = 256
SUBLANES = 8
BF16_SUBLANES = 16
CONV_TAIL_ROWS = SUBLANES
VMEM_LIMIT_BYTES = 60 * 1024 * 1024

TOKEN_TILE = 512
FF_CHUNKS = ((0, 768), (768, 1536), (1536, 2304), (2304, D_FF))

Q_CHUNKS_PER_GROUP = ATT_HEADS // ATT_KV_HEADS * ATT_HEAD_DIM // LANES

_C_KA = 0
_C_VA = _C_KA + ATT_KV_W
_C_QKM = _C_VA + ATT_KV_W
_C_VM = _C_QKM + 2 * ML_QK_W
_C_OM = _C_VM + ML_V_W
W_MID = _C_OM + ML_V_W

_IN_OFFS = tuple(int(v) for v in np.cumsum((0, ATT_Q_W, ATT_KV_W, ATT_KV_W, ML_QK_W, ML_QK_W, ML_V_W, ML_V_W,
                                            MLSTM_HEADS, MLSTM_HEADS, D_MODEL, D_MODEL)))
IN_PROJ_W = _IN_OFFS[-1]


def _resident(shape):
    return pl.BlockSpec(shape, lambda *_: (0,) * len(shape), pipeline_mode=pl.Buffered(1))


def _col_groups(lo, hi):
    return [(c, min(c + MXU_COLS, hi)) for c in range(lo, hi, MXU_COLS)]


def _interleave(mixer, dense):
    for n_dense in mixer:
        for _ in range(n_dense):
            next(dense, None)
    for _ in dense:
        pass


def _rms(x, g):
    ms = jnp.mean(x * x, axis=-1, keepdims=True)
    return x * lax.rsqrt(ms + RMS_EPS) * g


def _sigmoid(x):
    return 1.0 / (1.0 + jnp.exp2(x * NEG_LOG2E))


def _swiglu(h, wg_ref, wu_ref, wd_ref):
    acc = None
    for lo, hi in FF_CHUNKS:
        g = jnp.dot(h, wg_ref[:, lo:hi], preferred_element_type=F32)
        u = jnp.dot(h, wu_ref[:, lo:hi], preferred_element_type=F32)
        a = (g * _sigmoid(g) * u).astype(BF16)
        d = jnp.dot(a, wd_ref[lo:hi, :], preferred_element_type=F32)
        acc = d if acc is None else acc + d
    return acc


def _ffn1_kernel(x_ref, g_ref, wg_ref, wu_ref, wd_ref, win_ref, *refs):
    n_plain = (len(refs) - 5) // 2
    plain_in, (o_ref, wq_ref, wmid_ref, wgate_ref, wif_ref), plain_out = (
        refs[:n_plain], refs[n_plain:n_plain + 5], refs[n_plain + 5:])
    x = x_ref[...]
    h = _rms(x, g_ref[...]).astype(BF16)
    o_ref[...] = x + 0.5 * _swiglu(h, wg_ref, wu_ref, wd_ref)

    for src, dst in zip(plain_in, plain_out):
        dst[...] = src[...].astype(BF16)
    w = win_ref[...]
    wq_ref[...] = w[:, _IN_OFFS[0]:_IN_OFFS[1]].astype(BF16)
    wmid_ref[...] = w[:, _IN_OFFS[1]:_IN_OFFS[7]].astype(BF16)
    wgate_ref[...] = w[:, _IN_OFFS[9]:].astype(BF16)
    pad = jnp.zeros((w.shape[0], LANES - MLSTM_HEADS), F32)
    wif_ref[...] = jnp.concatenate([w[:, _IN_OFFS[7]:_IN_OFFS[8]], pad, w[:, _IN_OFFS[8]:_IN_OFFS[9]], pad],
                                   axis=1).astype(BF16)


def _row_block(rows, n_steps):
    return next(r for r in range(BF16_SUBLANES, rows + 1, BF16_SUBLANES)
                if rows % r == 0 and r * n_steps >= rows)


def _ffn1(x, g, wg, wu, wd, w_in, later_weights, tm):
    t = x.shape[0]
    n = t // tm
    tok = pl.BlockSpec((tm, D_MODEL), lambda i: (i, 0))

    def streamed(w):
        r = _row_block(w.shape[0], n)
        last = w.shape[0] // r - 1
        return lambda width: pl.BlockSpec((r, width), lambda i: (jnp.minimum(i, last), 0))

    in_spec = streamed(w_in)
    plain = [streamed(w) for w in later_weights]
    in_widths = (ATT_Q_W, W_MID, 2 * D_MODEL, 2 * LANES)
    out = pl.pallas_call(
        _ffn1_kernel,
        grid=(n,),
        in_specs=[tok, _resident((1, D_MODEL)), _resident((D_MODEL, D_FF)), _resident((D_MODEL, D_FF)),
                  _resident((D_FF, D_MODEL)), in_spec(IN_PROJ_W)]
        + [spec(w.shape[1]) for spec, w in zip(plain, later_weights)],
        out_specs=[tok] + [in_spec(w_) for w_ in in_widths]
        + [spec(w.shape[1]) for spec, w in zip(plain, later_weights)],
        out_shape=[jax.ShapeDtypeStruct((t, D_MODEL), F32)]
        + [jax.ShapeDtypeStruct((D_MODEL, w_), BF16) for w_ in in_widths]
        + [jax.ShapeDtypeStruct(w.shape, BF16) for w in later_weights],
        compiler_params=pltpu.CompilerParams(dimension_semantics=("arbitrary",),
                                             vmem_limit_bytes=VMEM_LIMIT_BYTES),
        name="ffn1",
    )(x, g, wg, wu, wd, w_in, *later_weights)
    return out[0], out[1:5], out[5:]


def _rope(x, cos, sin_signed):
    lane = lax.broadcasted_iota(jnp.int32, x.shape, 1)
    half = ATT_HEAD_DIM // 2
    partner = jnp.where(lane % ATT_HEAD_DIM < half, pltpu.roll(x, LANES - half, axis=1),
                        pltpu.roll(x, half, axis=1))
    return x * cos + partner * sin_signed


def _attn_scores(g, q_blk, k_cur, k_prev):
    low = lax.broadcasted_iota(jnp.int32, (ATT_BLOCK, LANES), 1) < ATT_HEAD_DIM
    c, in_high = divmod(g, 2)
    ksl = slice(c * LANES, (c + 1) * LANES)
    parts = []
    for k in (k_cur, k_prev):
        x = k[:, ksl].astype(F32)
        other = pltpu.roll(x, ATT_HEAD_DIM, axis=1)
        in_low_lanes, in_high_lanes = (other, x) if in_high else (x, other)
        parts.append((jnp.where(low, in_low_lanes, 0.0), jnp.where(low, 0.0, in_high_lanes)))
    kstack = jnp.concatenate([parts[0][0], parts[1][0], parts[0][1], parts[1][1]], axis=0).astype(BF16)
    qstack = jnp.concatenate(
        [q_blk[:, (g * Q_CHUNKS_PER_GROUP + r) * LANES:(g * Q_CHUNKS_PER_GROUP + r + 1) * LANES]
         for r in range(Q_CHUNKS_PER_GROUP)], axis=0)
    return lax.dot_general(kstack, qstack, (((1,), (1,)), ((), ())), preferred_element_type=F32)


def _attn_block(q_blk, kv_cur, kv_prev, has_prev, sink_ref, pt_scr, o_ref, rows):
    blk = ATT_BLOCK
    key = lax.broadcasted_iota(jnp.int32, (blk, blk), 0)
    qry = lax.broadcasted_iota(jnp.int32, (blk, blk), 1)
    in_cur = key <= qry
    neg_inf = jnp.float32(-jnp.inf)
    st = [_attn_scores(g, q_blk, kv_cur[:, :ATT_KV_W], kv_prev[:, :ATT_KV_W]) for g in range(ATT_KV_HEADS)]
    yield 1
    v_cat = jnp.concatenate([kv_cur[:, ATT_KV_W:], kv_prev[:, ATT_KV_W:]], axis=0).astype(F32)
    vt_all = v_cat.T.astype(BF16)
    for g in range(ATT_KV_HEADS):
        for r in range(Q_CHUNKS_PER_GROUP):
            qs = slice(r * blk, (r + 1) * blk)
            for half in range(2):
                head = (g * Q_CHUNKS_PER_GROUP + r) * 2 + half
                s_cur = st[g][(2 * half) * blk:(2 * half + 1) * blk, qs]
                s_prev = st[g][(2 * half + 1) * blk:(2 * half + 2) * blk, qs]
                s = jnp.where(in_cur, s_cur, jnp.where(has_prev, s_prev, neg_inf))
                sink = sink_ref[head:head + 1, :]
                m = jnp.maximum(jnp.max(s, axis=0, keepdims=True), sink)
                p = jnp.exp(s - m)
                denom = jnp.sum(p, axis=0, keepdims=True) + jnp.exp(sink - m)
                p = p * (1.0 / denom)
                zp = jnp.zeros_like(p)
                pt_scr[g * 2 + half, :blk, qs] = jnp.where(in_cur, p, zp).astype(BF16)
                pt_scr[g * 2 + half, blk:, qs] = jnp.where(in_cur, zp, p).astype(BF16)
        vt = vt_all[g * ATT_HEAD_DIM:(g + 1) * ATT_HEAD_DIM, :]
        ot = [jnp.dot(vt, pt_scr[g * 2 + half], preferred_element_type=F32) for half in range(2)]
        for r in range(Q_CHUNKS_PER_GROUP):
            j = g * Q_CHUNKS_PER_GROUP + r
            qs = slice(r * blk, (r + 1) * blk)
            o_t = jnp.concatenate([ot[0][:, qs], ot[1][:, qs]], axis=0)
            o_ref[rows, j * LANES:(j + 1) * LANES] = o_t.T.astype(BF16)
    yield 0


def _inproj_kernel(blocks_per_seq, x_ref, g_ref, pos_ref, wq_ref, wmid_ref, wgate_ref, wif_ref, sink_ref,
                   att_ref, qkm_ref, vo_ref, gates_ref, gif_ref, q_scr, kv_scr, pt_scr):
    tm = x_ref.shape[0]
    blk = ATT_BLOCK
    step = pl.program_id(0)
    slot = step % 2
    q_new, kv_new = q_scr.at[slot], kv_scr.at[slot]
    q_old, kv_old = q_scr.at[1 - slot], kv_scr.at[1 - slot]

    @pl.when(step == 0)
    def _():
        q_scr[...] = jnp.zeros_like(q_scr)
        kv_scr[...] = jnp.zeros_like(kv_scr)

    h = _rms(x_ref[...], g_ref[...]).astype(BF16)
    kv_new[:blk, :] = kv_old[tm:, :]

    def mixer():
        for k in range(tm // blk):
            rows = slice(k * blk, (k + 1) * blk)
            has_prev = ((step - 1) * (tm // blk) + k) % blocks_per_seq != 0
            yield from _attn_block(q_old[rows, :], kv_old[blk + k * blk:2 * blk + k * blk, :],
                                   kv_old[k * blk:(k + 1) * blk, :], has_prev, sink_ref, pt_scr, att_ref, rows)

    def dense():
        def proj(w_ref, lo, width):
            return jnp.dot(h, w_ref[:, lo:lo + width], preferred_element_type=F32)

        z = proj(wq_ref, 0, ATT_Q_W)
        cq, sq, ck, sk = (pos_ref[:, j * LANES:(j + 1) * LANES] for j in range(4))
        for j in range(ATT_Q_W // LANES):
            sl = slice(j * LANES, (j + 1) * LANES)
            q_new[:, sl] = _rope(z[:, sl], cq, sq).astype(BF16)
        yield
        z = proj(wmid_ref, _C_KA, 2 * ATT_KV_W)
        for j in range(ATT_KV_W // LANES):
            sl = slice(j * LANES, (j + 1) * LANES)
            kv_new[blk:, sl] = _rope(z[:, sl], ck, sk).astype(BF16)
        kv_new[blk:, ATT_KV_W:] = z[:, ATT_KV_W:].astype(BF16)
        vo_ref[:, :ML_V_W] = proj(wmid_ref, _C_VM, ML_V_W).astype(BF16)
        yield
        vo_ref[:, ML_V_W:] = proj(wmid_ref, _C_OM, ML_V_W).astype(BF16)
        gates_ref[:, :D_MODEL] = proj(wgate_ref, 0, D_MODEL).astype(BF16)
        yield
        gates_ref[:, D_MODEL:] = proj(wgate_ref, D_MODEL, D_MODEL).astype(BF16)
        gif_ref[...] = proj(wif_ref, 0, 2 * LANES)
        qkm_ref[...] = proj(wmid_ref, _C_QKM, 2 * ML_QK_W).astype(BF16)
        yield

    _interleave(mixer(), dense())


def _inproj(x1, g, pos, wq, wmid, wgate, wif, sinks, tm, seq):
    t = x1.shape[0]
    n = t // tm
    tiles_per_seq = seq // tm

    def cur(i):
        return jnp.minimum(i, n - 1)

    def tok(width):
        return pl.BlockSpec((tm, width), lambda i: (cur(i), 0))

    pos_spec = pl.BlockSpec((tm, 4 * LANES), lambda i: (cur(i) % tiles_per_seq, 0))
    lagged = pl.BlockSpec((tm, ATT_Q_W), lambda i: (jnp.maximum(i - 1, 0), 0))
    widths = (2 * ML_QK_W, 2 * ML_V_W, 2 * D_MODEL)
    out_shape = [jax.ShapeDtypeStruct((t, ATT_Q_W), BF16)]
    out_shape += [jax.ShapeDtypeStruct((t, w_), BF16) for w_ in widths]
    out_shape += [jax.ShapeDtypeStruct((t, 2 * LANES), F32)]
    return pl.pallas_call(
        functools.partial(_inproj_kernel, seq // ATT_BLOCK),
        grid=(n + 1,),
        in_specs=[tok(D_MODEL), _resident((1, D_MODEL)), pos_spec, _resident((D_MODEL, ATT_Q_W)),
                  _resident((D_MODEL, W_MID)), _resident((D_MODEL, 2 * D_MODEL)), _resident((D_MODEL, 2 * LANES)),
                  _resident((ATT_HEADS, LANES))],
        out_specs=[lagged] + [tok(w_) for w_ in widths] + [tok(2 * LANES)],
        out_shape=out_shape,
        scratch_shapes=[pltpu.VMEM((2, tm, ATT_Q_W), BF16),
                        pltpu.VMEM((2, ATT_BLOCK + tm, 2 * ATT_KV_W), BF16),
                        pltpu.VMEM((2 * ATT_KV_HEADS, 2 * ATT_BLOCK, Q_CHUNKS_PER_GROUP * ATT_BLOCK), BF16)],
        compiler_params=pltpu.CompilerParams(dimension_semantics=("arbitrary",),
                                             vmem_limit_bytes=VMEM_LIMIT_BYTES),
        name="in_proj_attn",
    )(x1, g, pos, wq, wmid, wgate, wif, sinks)


def _scan_rows(x, op, fill):
    n = x.shape[0]
    row = lax.broadcasted_iota(jnp.int32, x.shape, 0)
    shift = 1
    while shift < n:
        prev = jnp.where(row >= shift, pltpu.roll(x, shift, axis=0), fill)
        x = op(x, prev)
        shift *= 2
    return x


def _mlstm_prologue(rows, qk_ref, gif_ref, cw_ref, cb_ref, bi_ref, bf_ref, m_scr, conv_scr):
    L = MLSTM_CHUNK
    tail = CONV_TAIL_ROWS
    x = qk_ref[rows, :].astype(F32)
    conv_scr[tail:, :] = x
    y = cb_ref[...] + cw_ref[MLSTM_CONV - 1:MLSTM_CONV, :] * x
    for j in range(1, MLSTM_CONV):
        y = y + cw_ref[MLSTM_CONV - 1 - j:MLSTM_CONV - j, :] * conv_scr[tail - j:tail - j + L, :]
    conv_scr[:tail, :] = x[L - tail:, :]
    qk = y * _sigmoid(y)

    ig = gif_ref[rows, :LANES] + bi_ref[...]
    fpre = gif_ref[rows, LANES:] + bf_ref[...]
    logf = jnp.minimum(fpre, 0.0) - jnp.log1p(jnp.exp(-jnp.abs(fpre)))
    b = _scan_rows(logf, jnp.add, 0.0)
    g = ig - b
    cm = _scan_rows(g, jnp.maximum, -jnp.inf)
    m_prev = m_scr[0:1, :]
    u = jnp.maximum(cm, m_prev)
    g_max = cm[L - 1:L, :]
    u_last = u[L - 1:L, :]
    gates = dict(u=u, a=jnp.exp(g - g_max), inter=jnp.exp(m_prev - u), emt=jnp.exp(-(b + u)),
                 s_old=jnp.exp(m_prev - u_last), s_new=jnp.exp(g_max - u_last),
                 g_rows=g.T)
    m_scr[0:1, :] = b[L - 1:L, :] + u_last
    return qk, gates


def _mlstm_head(h, rows, qk, gates, vo_ref, hn_ref, out_ref, ct_scr, n_scr):
    L = MLSTM_CHUNK
    row = lax.broadcasted_iota(jnp.int32, (L, L), 0)
    col = lax.broadcasted_iota(jnp.int32, (L, L), 1)
    causal = col <= row
    qs = slice(h * MLSTM_QK_DIM, (h + 1) * MLSTM_QK_DIM)
    ks = slice(ML_QK_W + h * MLSTM_QK_DIM, ML_QK_W + (h + 1) * MLSTM_QK_DIM)
    vs = slice(h * MLSTM_V_DIM, (h + 1) * MLSTM_V_DIM)
    qh = qk[:, qs]
    kh = qk[:, ks] * (MLSTM_QK_DIM ** -0.5)
    qb = qh.astype(BF16)
    kb = kh.astype(BF16)
    vb = vo_ref[rows, vs]
    inter_col = gates["inter"][:, h:h + 1]
    a_col = gates["a"][:, h:h + 1]

    decay = jnp.where(causal, jnp.exp(gates["g_rows"][h:h + 1, :] - gates["u"][:, h:h + 1]), 0.0)
    s = lax.dot_general(qb, kb, (((1,), (1,)), ((), ())), preferred_element_type=F32) * decay
    ct = ct_scr[h]
    n_row = n_scr[h:h + 1, :]
    yield 1
    num = (jnp.dot(s.astype(BF16), vb, preferred_element_type=F32)
           + inter_col * jnp.dot(qb, ct.astype(BF16), preferred_element_type=F32))
    den = jnp.sum(s + inter_col * (qh * n_row), axis=-1, keepdims=True)
    hh = num * (1.0 / jnp.maximum(jnp.abs(den), gates["emt"][:, h:h + 1]))
    hn = _rms(hh, hn_ref[:, vs])
    o_pre = vo_ref[rows, ML_V_W + h * MLSTM_V_DIM:ML_V_W + (h + 1) * MLSTM_V_DIM]
    out_ref[rows, vs] = (_sigmoid(o_pre.astype(F32)) * hn).astype(BF16)
    yield 1
    av = (a_col * vb.astype(F32)).astype(BF16)
    d_ct = lax.dot_general(kb, av, (((0,), (0,)), ((), ())), preferred_element_type=F32)
    d_n = jnp.sum(a_col * kh, axis=0, keepdims=True)
    so = gates["s_old"][:, h:h + 1]
    sn = gates["s_new"][:, h:h + 1]
    ct_scr[h] = so * ct + sn * d_ct
    n_scr[h:h + 1, :] = so * n_row + sn * d_n
    yield 1


def _out_kernel(tiles_per_seq, n_tiles, att_ref, gates_ref, x1_ref, qk_ref, vo_ref, gif_ref,
                cw_ref, cb_ref, bi_ref, bf_ref, hn_ref, watt_ref, wml_ref, wout_ref, g2_ref, wg_ref, wu_ref,
                wd_ref, gfin_ref, out_ref, hm_scr, ct_scr, n_scr, m_scr, conv_scr):
    tm = x1_ref.shape[0]
    L = MLSTM_CHUNK
    step = pl.program_id(0)
    slot = step % 2
    hm_new, hm_old = hm_scr.at[slot], hm_scr.at[1 - slot]

    @pl.when(step == 0)
    def _():
        hm_scr[...] = jnp.zeros_like(hm_scr)

    @pl.when(jnp.minimum(step, n_tiles - 1) % tiles_per_seq == 0)
    def _():
        ct_scr[...] = jnp.zeros_like(ct_scr)
        n_scr[...] = jnp.zeros_like(n_scr)
        m_scr[...] = jnp.zeros_like(m_scr)
        conv_scr[:CONV_TAIL_ROWS, :] = jnp.zeros((CONV_TAIL_ROWS, conv_scr.shape[1]), F32)

    def mixer():
        for c in range(tm // L):
            rows = slice(c * L, (c + 1) * L)
            qk, gates = _mlstm_prologue(rows, qk_ref, gif_ref, cw_ref, cb_ref, bi_ref, bf_ref, m_scr, conv_scr)
            yield 1
            for h in range(MLSTM_HEADS):
                yield from _mlstm_head(h, rows, qk, gates, vo_ref, hn_ref, hm_new, ct_scr, n_scr)

    def dense():
        def dot(x, w):
            return jnp.dot(x, w, preferred_element_type=F32)

        cols = _col_groups(0, D_MODEL)
        ya, ym, x2p = [], [], []
        for lo, hi in cols:
            ya.append(dot(att_ref[...], watt_ref[:, lo:hi]))
            yield
        for lo, hi in cols:
            ym.append(dot(hm_old[...], wml_ref[:, lo:hi]))
            yield
        y = jnp.concatenate(
            [_sigmoid(gates_ref[:, lo:hi].astype(F32)) * ya[j]
             + _sigmoid(gates_ref[:, D_MODEL + lo:D_MODEL + hi].astype(F32)) * ym[j]
             for j, (lo, hi) in enumerate(cols)], axis=1).astype(BF16)
        for lo, hi in cols:
            x2p.append(dot(y, wout_ref[:, lo:hi]))
            yield
        x2 = x1_ref[...] + jnp.concatenate(x2p, axis=1)
        h2 = _rms(x2, g2_ref[...]).astype(BF16)
        acc = None
        for clo, chi in FF_CHUNKS:
            groups = _col_groups(clo, chi)
            g = []
            for lo, hi in groups:
                g.append(dot(h2, wg_ref[:, lo:hi]))
                yield
            a = []
            for j, (lo, hi) in enumerate(groups):
                u = dot(h2, wu_ref[:, lo:hi])
                a.append((g[j] * _sigmoid(g[j]) * u).astype(BF16))
                yield
            a = jnp.concatenate(a, axis=1)
            part = []
            for lo, hi in cols:
                part.append(dot(a, wd_ref[clo:chi, lo:hi]))
                yield
            acc = part if acc is None else [p + q for p, q in zip(acc, part)]
        x3 = x2 + 0.5 * jnp.concatenate(acc, axis=1)
        out_ref[...] = _rms(x3, gfin_ref[...])
        yield

    _interleave(mixer(), dense())


def _out(att, gates, x1, qkm, vo, gif, cw, cb, bi, bf, hn, watt, wml, wout, g2, wg, wu, wd, gfin, tm, seq):
    t = x1.shape[0]
    n = t // tm

    def cur(width):
        return pl.BlockSpec((tm, width), lambda i: (jnp.minimum(i, n - 1), 0))

    def lagged(width):
        return pl.BlockSpec((tm, width), lambda i: (jnp.maximum(i - 1, 0), 0))

    sq = _resident((D_MODEL, D_MODEL))
    vec = _resident((1, D_MODEL))
    return pl.pallas_call(
        functools.partial(_out_kernel, seq // tm, n),
        grid=(n + 1,),
        in_specs=[lagged(D_MODEL), lagged(2 * D_MODEL), lagged(D_MODEL),
                  cur(2 * ML_QK_W), cur(2 * ML_V_W), cur(2 * LANES),
                  _resident((MLSTM_CONV, 2 * ML_QK_W)), _resident((1, 2 * ML_QK_W)),
                  _resident((1, LANES)), _resident((1, LANES)), vec,
                  sq, sq, sq, vec, _resident((D_MODEL, D_FF)), _resident((D_MODEL, D_FF)),
                  _resident((D_FF, D_MODEL)), vec],
        out_specs=lagged(D_MODEL),
        out_shape=jax.ShapeDtypeStruct((t, D_MODEL), F32),
        scratch_shapes=[pltpu.VMEM((2, tm, ML_V_W), BF16),
                        pltpu.VMEM((MLSTM_HEADS, MLSTM_QK_DIM, MLSTM_V_DIM), F32),
                        pltpu.VMEM((SUBLANES, LANES), F32),
                        pltpu.VMEM((SUBLANES, LANES), F32),
                        pltpu.VMEM((CONV_TAIL_ROWS + MLSTM_CHUNK, 2 * ML_QK_W), F32)],
        compiler_params=pltpu.CompilerParams(dimension_semantics=("arbitrary",),
                                             vmem_limit_bytes=VMEM_LIMIT_BYTES),
        name="mlstm_merge_ffn2",
    )(att, gates, x1, qkm, vo, gif, cw, cb, bi, bf, hn, watt, wml, wout, g2, wg, wu, wd, gfin)


def _rope_tables(seq):
    half = ATT_HEAD_DIM // 2
    pos = jnp.arange(seq, dtype=F32)
    inv_freq = ROPE_THETA ** (-jnp.arange(half, dtype=F32) / half)
    ang = pos[:, None] * inv_freq[None, :]
    cos, sin = jnp.cos(ang), jnp.sin(ang)
    reps = LANES // ATT_HEAD_DIM
    cos = jnp.tile(jnp.concatenate([cos, cos], axis=-1), (1, reps))
    sin = jnp.tile(jnp.concatenate([-sin, sin], axis=-1), (1, reps))
    return cos, sin


def _pad_lanes(v):
    return jnp.pad(v, ((0, 0), (0, LANES - v.shape[-1])))


def _layer(x2d, batch, seq, p):
    (ffn1_norm, ffn1_w_gate, ffn1_w_up, ffn1_w_down, mix_norm, w_in, b_i, b_f, attn_sinks, conv_w, conv_b,
     head_norm, w_att, w_mlstm, w_out, ffn2_norm, ffn2_w_gate, ffn2_w_up, ffn2_w_down, final_norm) = p
    tm = min(TOKEN_TILE, seq)
    assert seq % tm == 0 and seq % ATT_BLOCK == 0 and seq % MLSTM_CHUNK == 0

    cos, sin = _rope_tables(seq)
    q_scale = ATT_HEAD_DIM ** -0.5
    sinks = jnp.broadcast_to(attn_sinks[:, None], (ATT_HEADS, LANES)).astype(F32)

    x1, (w_q, w_mid, w_gate, w_if), (watt, wml, wout, wg2, wu2, wd2) = _ffn1(
        x2d, ffn1_norm[None], ffn1_w_gate.astype(BF16), ffn1_w_up.astype(BF16), ffn1_w_down.astype(BF16),
        w_in, (w_att, w_mlstm, w_out, ffn2_w_gate, ffn2_w_up, ffn2_w_down), tm)
    pos = jnp.concatenate([cos * q_scale, sin * q_scale, cos, sin], axis=1)
    att, qkm, vo, gates, gif = _inproj(x1, mix_norm[None], pos, w_q, w_mid, w_gate, w_if, sinks, tm, seq)
    return _out(att, gates, x1, qkm, vo, gif, conv_w, conv_b[None], _pad_lanes(b_i[None]),
                _pad_lanes(b_f[None]), head_norm[None], watt, wml, wout, ffn2_norm[None], wg2, wu2, wd2,
                final_norm[None], tm, seq)


def kernel(x, ffn1_norm, ffn1_w_gate, ffn1_w_up, ffn1_w_down, mix_norm, w_in, b_i, b_f, attn_sinks, conv_w,
           conv_b, head_norm, w_att, w_mlstm, w_out, ffn2_norm, ffn2_w_gate, ffn2_w_up, ffn2_w_down, final_norm):
    batch, seq, d = x.shape
    assert d == D_MODEL and ffn1_norm.shape[0] == 1, "single-layer kernel"
    per_layer = (ffn1_norm, ffn1_w_gate, ffn1_w_up, ffn1_w_down, mix_norm, w_in, b_i, b_f, attn_sinks, conv_w,
                 conv_b, head_norm, w_att, w_mlstm, w_out, ffn2_norm, ffn2_w_gate, ffn2_w_up, ffn2_w_down)
    params = tuple(a[0] for a in per_layer) + (final_norm,)
    out = _layer(x.reshape(batch * seq, d), batch, seq, params)
    return out.reshape(batch, seq, d)
```

```python
import functools

import jax
import jax.numpy as jnp
import numpy as np
from jax import lax
from jax.experimental import pallas as pl
from jax.experimental.pallas import tpu as pltpu

F32 = jnp.float32
BF16 = jnp.bfloat16

D_MODEL = 1024
ATT_HEADS = 16
ATT_KV_HEADS = 4
ATT_HEAD_DIM = 64
ATT_BLOCK = 128
ROPE_THETA = 10000.0
MLSTM_HEADS = 4
MLSTM_V_DIM = D_MODEL // MLSTM_HEADS
MLSTM_QK_DIM = MLSTM_V_DIM // 2
MLSTM_CHUNK = 128
MLSTM_CONV = 4
D_FF = 2816
RMS_EPS = 1e-5
NEG_LOG2E = -1.4426950408889634

ATT_Q_W = ATT_HEADS * ATT_HEAD_DIM
ATT_KV_W = ATT_KV_HEADS * ATT_HEAD_DIM
ML_QK_W = MLSTM_HEADS * MLSTM_QK_DIM
ML_V_W = MLSTM_HEADS * MLSTM_V_DIM

LANES = 128
MXU_COLS = 256
SUBLANES = 8
BF16_SUBLANES = 16
CONV_TAIL_ROWS = SUBLANES
VMEM_LIMIT_BYTES = 60 * 1024 * 1024

TOKEN_TILE = 512
FF_CHUNKS = ((0, 768), (768, 1536), (1536, 2304), (2304, D_FF))

Q_CHUNKS_PER_GROUP = ATT_HEADS // ATT_KV_HEADS * ATT_HEAD_DIM // LANES

_C_KA = 0
_C_VA = _C_KA + ATT_KV_W
_C_QKM = _C_VA + ATT_KV_W
_C_VM = _C_QKM + 2 * ML_QK_W
_C_OM = _C_VM + ML_V_W
W_MID = _C_OM + ML_V_W

_IN_OFFS = tuple(int(v) for v in np.cumsum((0, ATT_Q_W, ATT_KV_W, ATT_KV_W, ML_QK_W, ML_QK_W, ML_V_W, ML_V_W,
                                            MLSTM_HEADS, MLSTM_HEADS, D_MODEL, D_MODEL)))
IN_PROJ_W = _IN_OFFS[-1]


def _resident(shape):
    return pl.BlockSpec(shape, lambda *_: (0,) * len(shape), pipeline_mode=pl.Buffered(1))


def _col_groups(lo, hi):
    return [(c, min(c + MXU_COLS, hi)) for c in range(lo, hi, MXU_COLS)]


def _interleave(mixer, dense):
    for n_dense in mixer:
        for _ in range(n_dense):
            next(dense, None)
    for _ in dense:
        pass


def _rms(x, g):
    ms = jnp.mean(x * x, axis=-1, keepdims=True)
    return x * lax.rsqrt(ms + RMS_EPS) * g


def _sigmoid(x):
    return 1.0 / (1.0 + jnp.exp2(x * NEG_LOG2E))


def _swiglu(h, wg_ref, wu_ref, wd_ref):
    acc = None
    for lo, hi in FF_CHUNKS:
        g = jnp.dot(h, wg_ref[:, lo:hi], preferred_element_type=F32)
        u = jnp.dot(h, wu_ref[:, lo:hi], preferred_element_type=F32)
        a = (g * _sigmoid(g) * u).astype(BF16)
        d = jnp.dot(a, wd_ref[lo:hi, :], preferred_element_type=F32)
        acc = d if acc is None else acc + d
    return acc


def _ffn1_kernel(x_ref, g_ref, wg_ref, wu_ref, wd_ref, win_ref, *refs):
    n_plain = (len(refs) - 5) // 2
    plain_in, (o_ref, wq_ref, wmid_ref, wgate_ref, wif_ref), plain_out = (
        refs[:n_plain], refs[n_plain:n_plain + 5], refs[n_plain + 5:])
    for src, dst in zip(plain_in, plain_out):
        dst[...] = src[...].astype(BF16)
    w = win_ref[...]
    wq_ref[...] = w[:, _IN_OFFS[0]:_IN_OFFS[1]].astype(BF16)
    wmid_ref[...] = w[:, _IN_OFFS[1]:_IN_OFFS[7]].astype(BF16)
    wgate_ref[...] = w[:, _IN_OFFS[9]:].astype(BF16)
    pad = jnp.zeros((w.shape[0], LANES - MLSTM_HEADS), F32)
    wif_ref[...] = jnp.concatenate([w[:, _IN_OFFS[7]:_IN_OFFS[8]], pad, w[:, _IN_OFFS[8]:_IN_OFFS[9]], pad],
                                   axis=1).astype(BF16)

    x = x_ref[...]
    h = _rms(x, g_ref[...]).astype(BF16)
    o_ref[...] = x + 0.5 * _swiglu(h, wg_ref, wu_ref, wd_ref)


def _row_block(rows, n_steps):
    return next(r for r in range(BF16_SUBLANES, rows + 1, BF16_SUBLANES)
                if rows % r == 0 and r * n_steps >= rows)


def _ffn1(x, g, wg, wu, wd, w_in, later_weights, tm):
    t = x.shape[0]
    n = t // tm
    tok = pl.BlockSpec((tm, D_MODEL), lambda i: (i, 0))

    def streamed(w):
        r = _row_block(w.shape[0], n)
        last = w.shape[0] // r - 1
        return lambda width: pl.BlockSpec((r, width), lambda i: (jnp.minimum(i, last), 0))

    in_spec = streamed(w_in)
    plain = [streamed(w) for w in later_weights]
    in_widths = (ATT_Q_W, W_MID, 2 * D_MODEL, 2 * LANES)
    out = pl.pallas_call(
        _ffn1_kernel,
        grid=(n,),
        in_specs=[tok, _resident((1, D_MODEL)), _resident((D_MODEL, D_FF)), _resident((D_MODEL, D_FF)),
                  _resident((D_FF, D_MODEL)), in_spec(IN_PROJ_W)]
        + [spec(w.shape[1]) for spec, w in zip(plain, later_weights)],
        out_specs=[tok] + [in_spec(w_) for w_ in in_widths]
        + [spec(w.shape[1]) for spec, w in zip(plain, later_weights)],
        out_shape=[jax.ShapeDtypeStruct((t, D_MODEL), F32)]
        + [jax.ShapeDtypeStruct((D_MODEL, w_), BF16) for w_ in in_widths]
        + [jax.ShapeDtypeStruct(w.shape, BF16) for w in later_weights],
        compiler_params=pltpu.CompilerParams(dimension_semantics=("arbitrary",),
                                             vmem_limit_bytes=VMEM_LIMIT_BYTES),
        name="ffn1",
    )(x, g, wg, wu, wd, w_in, *later_weights)
    return out[0], out[1:5], out[5:]


def _rope(x, cos, sin_signed):
    lane = lax.broadcasted_iota(jnp.int32, x.shape, 1)
    half = ATT_HEAD_DIM // 2
    partner = jnp.where(lane % ATT_HEAD_DIM < half, pltpu.roll(x, LANES - half, axis=1),
                        pltpu.roll(x, half, axis=1))
    return x * cos + partner * sin_signed


def _attn_scores(g, q_blk, k_cur, k_prev):
    low = lax.broadcasted_iota(jnp.int32, (ATT_BLOCK, LANES), 1) < ATT_HEAD_DIM
    c, in_high = divmod(g, 2)
    ksl = slice(c * LANES, (c + 1) * LANES)
    parts = []
    for k in (k_cur, k_prev):
        x = k[:, ksl].astype(F32)
        other = pltpu.roll(x, ATT_HEAD_DIM, axis=1)
        in_low_lanes, in_high_lanes = (other, x) if in_high else (x, other)
        parts.append((jnp.where(low, in_low_lanes, 0.0), jnp.where(low, 0.0, in_high_lanes)))
    kstack = jnp.concatenate([parts[0][0], parts[1][0], parts[0][1], parts[1][1]], axis=0).astype(BF16)
    qstack = jnp.concatenate(
        [q_blk[:, (g * Q_CHUNKS_PER_GROUP + r) * LANES:(g * Q_CHUNKS_PER_GROUP + r + 1) * LANES]
         for r in range(Q_CHUNKS_PER_GROUP)], axis=0)
    return lax.dot_general(kstack, qstack, (((1,), (1,)), ((), ())), preferred_element_type=F32)


def _attn_block(q_blk, kv_cur, kv_prev, has_prev, sink_ref, pt_scr, o_ref, rows):
    blk = ATT_BLOCK
    key = lax.broadcasted_iota(jnp.int32, (blk, blk), 0)
    qry = lax.broadcasted_iota(jnp.int32, (blk, blk), 1)
    in_cur = key <= qry
    neg_inf = jnp.float32(-jnp.inf)
    st = [_attn_scores(g, q_blk, kv_cur[:, :ATT_KV_W], kv_prev[:, :ATT_KV_W]) for g in range(ATT_KV_HEADS)]
    yield 1
    v_cat = jnp.concatenate([kv_cur[:, ATT_KV_W:], kv_prev[:, ATT_KV_W:]], axis=0).astype(F32)
    vt_all = v_cat.T.astype(BF16)
    for g in range(ATT_KV_HEADS):
        for r in range(Q_CHUNKS_PER_GROUP):
            qs = slice(r * blk, (r + 1) * blk)
            for half in range(2):
                head = (g * Q_CHUNKS_PER_GROUP + r) * 2 + half
                s_cur = st[g][(2 * half) * blk:(2 * half + 1) * blk, qs]
                s_prev = st[g][(2 * half + 1) * blk:(2 * half + 2) * blk, qs]
                s = jnp.where(in_cur, s_cur, jnp.where(has_prev, s_prev, neg_inf))
                sink = sink_ref[head:head + 1, :]
                m = jnp.maximum(jnp.max(s, axis=0, keepdims=True), sink)
                p = jnp.exp(s - m)
                denom = jnp.sum(p, axis=0, keepdims=True) + jnp.exp(sink - m)
                p = p * (1.0 / denom)
                zp = jnp.zeros_like(p)
                pt_scr[g * 2 + half, :blk, qs] = jnp.where(in_cur, p, zp).astype(BF16)
                pt_scr[g * 2 + half, blk:, qs] = jnp.where(in_cur, zp, p).astype(BF16)
        vt = vt_all[g * ATT_HEAD_DIM:(g + 1) * ATT_HEAD_DIM, :]
        ot = [jnp.dot(vt, pt_scr[g * 2 + half], preferred_element_type=F32) for half in range(2)]
        for r in range(Q_CHUNKS_PER_GROUP):
            j = g * Q_CHUNKS_PER_GROUP + r
            qs = slice(r * blk, (r + 1) * blk)
            o_t = jnp.concatenate([ot[0][:, qs], ot[1][:, qs]], axis=0)
            o_ref[rows, j * LANES:(j + 1) * LANES] = o_t.T.astype(BF16)
    yield 0


def _inproj_kernel(blocks_per_seq, x_ref, g_ref, cos_ref, sin_ref, wq_ref, wmid_ref, wgate_ref, wif_ref, sink_ref,
                   att_ref, qkm_ref, vm_ref, om_ref, ga_ref, gm_ref, gi_ref, gf_ref, q_scr, kv_scr, pt_scr):
    tm = x_ref.shape[0]
    blk = ATT_BLOCK
    step = pl.program_id(0)
    slot = step % 2
    q_new, kv_new = q_scr.at[slot], kv_scr.at[slot]
    q_old, kv_old = q_scr.at[1 - slot], kv_scr.at[1 - slot]

    @pl.when(step == 0)
    def _():
        q_scr[...] = jnp.zeros_like(q_scr)
        kv_scr[...] = jnp.zeros_like(kv_scr)

    h = _rms(x_ref[...], g_ref[...]).astype(BF16)
    kv_new[:blk, :] = kv_old[tm:, :]

    def mixer():
        for k in range(tm // blk):
            rows = slice(k * blk, (k + 1) * blk)
            has_prev = ((step - 1) * (tm // blk) + k) % blocks_per_seq != 0
            yield from _attn_block(q_old[rows, :], kv_old[blk + k * blk:2 * blk + k * blk, :],
                                   kv_old[k * blk:(k + 1) * blk, :], has_prev, sink_ref, pt_scr, att_ref, rows)

    def dense():
        def proj(w_ref, lo, width):
            return jnp.dot(h, w_ref[:, lo:lo + width], preferred_element_type=F32)

        z = proj(wq_ref, 0, ATT_Q_W) * (ATT_HEAD_DIM ** -0.5)
        cos, sin = cos_ref[...], sin_ref[...]
        for j in range(ATT_Q_W // LANES):
            sl = slice(j * LANES, (j + 1) * LANES)
            q_new[:, sl] = _rope(z[:, sl], cos, sin).astype(BF16)
        yield
        z = proj(wmid_ref, _C_KA, 2 * ATT_KV_W)
        for j in range(ATT_KV_W // LANES):
            sl = slice(j * LANES, (j + 1) * LANES)
            kv_new[blk:, sl] = _rope(z[:, sl], cos, sin).astype(BF16)
        kv_new[blk:, ATT_KV_W:] = z[:, ATT_KV_W:].astype(BF16)
        vm_ref[...] = proj(wmid_ref, _C_VM, ML_V_W).astype(BF16)
        yield
        om_ref[...] = proj(wmid_ref, _C_OM, ML_V_W).astype(BF16)
        ga_ref[...] = proj(wgate_ref, 0, D_MODEL).astype(BF16)
        yield
        gm_ref[...] = proj(wgate_ref, D_MODEL, D_MODEL).astype(BF16)
        z = proj(wif_ref, 0, 2 * LANES)
        gi_ref[...] = z[:, :LANES]
        gf_ref[...] = z[:, LANES:]
        qkm_ref[...] = proj(wmid_ref, _C_QKM, 2 * ML_QK_W).astype(BF16)
        yield

    _interleave(mixer(), dense())


def _inproj(x1, g, cos, sin, wq, wmid, wgate, wif, sinks, tm, seq):
    t = x1.shape[0]
    n = t // tm
    tiles_per_seq = seq // tm

    def cur(i):
        return jnp.minimum(i, n - 1)

    def tok(width):
        return pl.BlockSpec((tm, width), lambda i: (cur(i), 0))

    pos = pl.BlockSpec((tm, LANES), lambda i: (cur(i) % tiles_per_seq, 0))
    lagged = pl.BlockSpec((tm, ATT_Q_W), lambda i: (jnp.maximum(i - 1, 0), 0))
    widths = (2 * ML_QK_W, ML_V_W, ML_V_W, D_MODEL, D_MODEL)
    out_shape = [jax.ShapeDtypeStruct((t, ATT_Q_W), BF16)]
    out_shape += [jax.ShapeDtypeStruct((t, w_), BF16) for w_ in widths]
    out_shape += [jax.ShapeDtypeStruct((t, LANES), F32)] * 2
    return pl.pallas_call(
        functools.partial(_inproj_kernel, seq // ATT_BLOCK),
        grid=(n + 1,),
        in_specs=[tok(D_MODEL), _resident((1, D_MODEL)), pos, pos, _resident((D_MODEL, ATT_Q_W)),
                  _resident((D_MODEL, W_MID)), _resident((D_MODEL, 2 * D_MODEL)), _resident((D_MODEL, 2 * LANES)),
                  _resident((ATT_HEADS, LANES))],
        out_specs=[lagged] + [tok(w_) for w_ in widths] + [tok(LANES)] * 2,
        out_shape=out_shape,
        scratch_shapes=[pltpu.VMEM((2, tm, ATT_Q_W), BF16),
                        pltpu.VMEM((2, ATT_BLOCK + tm, 2 * ATT_KV_W), BF16),
                        pltpu.VMEM((2 * ATT_KV_HEADS, 2 * ATT_BLOCK, Q_CHUNKS_PER_GROUP * ATT_BLOCK), BF16)],
        compiler_params=pltpu.CompilerParams(dimension_semantics=("arbitrary",),
                                             vmem_limit_bytes=VMEM_LIMIT_BYTES),
        name="in_proj_attn",
    )(x1, g, cos, sin, wq, wmid, wgate, wif, sinks)


def _scan_rows(x, op, fill):
    n = x.shape[0]
    row = lax.broadcasted_iota(jnp.int32, x.shape, 0)
    shift = 1
    while shift < n:
        prev = jnp.where(row >= shift, pltpu.roll(x, shift, axis=0), fill)
        x = op(x, prev)
        shift *= 2
    return x


def _mlstm_chunk_inputs(rows, qk_ref, gi_ref, gf_ref, cw_ref, cb_ref, bi_ref, bf_ref, conv_scr):
    L = MLSTM_CHUNK
    tail = CONV_TAIL_ROWS
    x = qk_ref[rows, :].astype(F32)
    conv_scr[tail:, :] = x
    y = cb_ref[...] + cw_ref[MLSTM_CONV - 1:MLSTM_CONV, :] * x
    for j in range(1, MLSTM_CONV):
        y = y + cw_ref[MLSTM_CONV - 1 - j:MLSTM_CONV - j, :] * conv_scr[tail - j:tail - j + L, :]
    conv_scr[:tail, :] = x[L - tail:, :]
    qk = y * _sigmoid(y)

    ig = gi_ref[rows, :] + bi_ref[...]
    fpre = gf_ref[rows, :] + bf_ref[...]
    logf = jnp.minimum(fpre, 0.0) - jnp.log1p(jnp.exp(-jnp.abs(fpre)))
    b = _scan_rows(logf, jnp.add, 0.0)
    g = ig - b
    cm = _scan_rows(g, jnp.maximum, -jnp.inf)
    return qk, b, g, cm


def _mlstm_gates(b, g, cm, m_scr):
    L = MLSTM_CHUNK
    m_prev = m_scr[0:1, :]
    u = jnp.maximum(cm, m_prev)
    g_max = cm[L - 1:L, :]
    u_last = u[L - 1:L, :]
    gates = dict(u=u, a=jnp.exp(g - g_max), inter=jnp.exp(m_prev - u), emt=jnp.exp(-(b + u)),
                 s_old=jnp.exp(m_prev - u_last), s_new=jnp.exp(g_max - u_last),
                 g_rows=g.T)
    m_scr[0:1, :] = b[L - 1:L, :] + u_last
    return gates


def _mlstm_head(h, rows, qk, gates, v_ref, o_ref, hn_ref, out_ref, ct_scr, n_scr):
    L = MLSTM_CHUNK
    row = lax.broadcasted_iota(jnp.int32, (L, L), 0)
    col = lax.broadcasted_iota(jnp.int32, (L, L), 1)
    causal = col <= row
    qs = slice(h * MLSTM_QK_DIM, (h + 1) * MLSTM_QK_DIM)
    ks = slice(ML_QK_W + h * MLSTM_QK_DIM, ML_QK_W + (h + 1) * MLSTM_QK_DIM)
    vs = slice(h * MLSTM_V_DIM, (h + 1) * MLSTM_V_DIM)
    qh = qk[:, qs]
    kh = qk[:, ks] * (MLSTM_QK_DIM ** -0.5)
    qb = qh.astype(BF16)
    kb = kh.astype(BF16)
    vb = v_ref[rows, vs]
    inter_col = gates["inter"][:, h:h + 1]
    a_col = gates["a"][:, h:h + 1]

    decay = jnp.where(causal, jnp.exp(gates["g_rows"][h:h + 1, :] - gates["u"][:, h:h + 1]), 0.0)
    s = lax.dot_general(qb, kb, (((1,), (1,)), ((), ())), preferred_element_type=F32) * decay
    ct = ct_scr[h]
    n_row = n_scr[h:h + 1, :]
    yield 1
    num = (jnp.dot(s.astype(BF16), vb, preferred_element_type=F32)
           + inter_col * jnp.dot(qb, ct.astype(BF16), preferred_element_type=F32))
    den = jnp.sum(s + inter_col * (qh * n_row), axis=-1, keepdims=True)
    hh = num * (1.0 / jnp.maximum(jnp.abs(den), gates["emt"][:, h:h + 1]))
    hn = _rms(hh, hn_ref[:, vs])
    out_ref[rows, vs] = (_sigmoid(o_ref[rows, vs].astype(F32)) * hn).astype(BF16)
    yield 1
    av = (a_col * vb.astype(F32)).astype(BF16)
    d_ct = lax.dot_general(kb, av, (((0,), (0,)), ((), ())), preferred_element_type=F32)
    d_n = jnp.sum(a_col * kh, axis=0, keepdims=True)
    so = gates["s_old"][:, h:h + 1]
    sn = gates["s_new"][:, h:h + 1]
    ct_scr[h] = so * ct + sn * d_ct
    n_scr[h:h + 1, :] = so * n_row + sn * d_n
    yield 1


def _out_kernel(tiles_per_seq, n_tiles, att_ref, ga_ref, gm_ref, x1_ref, qk_ref, v_ref, o_ref, gi_ref, gf_ref,
                cw_ref, cb_ref, bi_ref, bf_ref, hn_ref, watt_ref, wml_ref, wout_ref, g2_ref, wg_ref, wu_ref,
                wd_ref, gfin_ref, out_ref, hm_scr, ct_scr, n_scr, m_scr, conv_scr):
    tm = x1_ref.shape[0]
    L = MLSTM_CHUNK
    step = pl.program_id(0)
    slot = step % 2
    hm_new, hm_old = hm_scr.at[slot], hm_scr.at[1 - slot]

    @pl.when(step == 0)
    def _():
        hm_scr[...] = jnp.zeros_like(hm_scr)

    @pl.when(jnp.minimum(step, n_tiles - 1) % tiles_per_seq == 0)
    def _():
        ct_scr[...] = jnp.zeros_like(ct_scr)
        n_scr[...] = jnp.zeros_like(n_scr)
        m_scr[...] = jnp.zeros_like(m_scr)
        conv_scr[:CONV_TAIL_ROWS, :] = jnp.zeros((CONV_TAIL_ROWS, conv_scr.shape[1]), F32)

    def chunk_inputs(c):
        return _mlstm_chunk_inputs(slice(c * L, (c + 1) * L), qk_ref, gi_ref, gf_ref, cw_ref, cb_ref, bi_ref,
                                   bf_ref, conv_scr)

    def mixer():
        for c in range(tm // L):
            rows = slice(c * L, (c + 1) * L)
            qk, b, g, cm = chunk_inputs(c)
            gates = _mlstm_gates(b, g, cm, m_scr)
            yield 1
            for h in range(MLSTM_HEADS):
                yield from _mlstm_head(h, rows, qk, gates, v_ref, o_ref, hn_ref, hm_new, ct_scr, n_scr)

    def dense():
        def dot(x, w):
            return jnp.dot(x, w, preferred_element_type=F32)

        cols = _col_groups(0, D_MODEL)
        ya, ym, x2p = [], [], []
        for lo, hi in cols:
            ya.append(dot(att_ref[...], watt_ref[:, lo:hi]))
            yield
        for lo, hi in cols:
            ym.append(dot(hm_old[...], wml_ref[:, lo:hi]))
            yield
        y = jnp.concatenate(
            [_sigmoid(ga_ref[:, lo:hi].astype(F32)) * ya[j] + _sigmoid(gm_ref[:, lo:hi].astype(F32)) * ym[j]
             for j, (lo, hi) in enumerate(cols)], axis=1).astype(BF16)
        for lo, hi in cols:
            x2p.append(dot(y, wout_ref[:, lo:hi]))
            yield
        x2 = x1_ref[...] + jnp.concatenate(x2p, axis=1)
        h2 = _rms(x2, g2_ref[...]).astype(BF16)
        acc = None
        for clo, chi in FF_CHUNKS:
            groups = _col_groups(clo, chi)
            g = []
            for lo, hi in groups:
                g.append(dot(h2, wg_ref[:, lo:hi]))
                yield
            a = []
            for j, (lo, hi) in enumerate(groups):
                u = dot(h2, wu_ref[:, lo:hi])
                a.append((g[j] * _sigmoid(g[j]) * u).astype(BF16))
                yield
            a = jnp.concatenate(a, axis=1)
            part = []
            for lo, hi in cols:
                part.append(dot(a, wd_ref[clo:chi, lo:hi]))
                yield
            acc = part if acc is None else [p + q for p, q in zip(acc, part)]
        x3 = x2 + 0.5 * jnp.concatenate(acc, axis=1)
        out_ref[...] = _rms(x3, gfin_ref[...])
        yield

    _interleave(mixer(), dense())


def _out(att, ga, gm, x1, qkm, vm, om, gi, gf, cw, cb, bi, bf, hn, watt, wml, wout, g2, wg, wu, wd, gfin, tm, seq):
    t = x1.shape[0]
    n = t // tm

    def cur(width):
        return pl.BlockSpec((tm, width), lambda i: (jnp.minimum(i, n - 1), 0))

    lagged = pl.BlockSpec((tm, D_MODEL), lambda i: (jnp.maximum(i - 1, 0), 0))
    sq = _resident((D_MODEL, D_MODEL))
    vec = _resident((1, D_MODEL))
    return pl.pallas_call(
        functools.partial(_out_kernel, seq // tm, n),
        grid=(n + 1,),
        in_specs=[lagged, lagged, lagged, lagged,
                  cur(2 * ML_QK_W), cur(ML_V_W), cur(ML_V_W), cur(LANES), cur(LANES),
                  _resident((MLSTM_CONV, 2 * ML_QK_W)), _resident((1, 2 * ML_QK_W)),
                  _resident((1, LANES)), _resident((1, LANES)), vec,
                  sq, sq, sq, vec, _resident((D_MODEL, D_FF)), _resident((D_MODEL, D_FF)),
                  _resident((D_FF, D_MODEL)), vec],
        out_specs=lagged,
        out_shape=jax.ShapeDtypeStruct((t, D_MODEL), F32),
        scratch_shapes=[pltpu.VMEM((2, tm, ML_V_W), BF16),
                        pltpu.VMEM((MLSTM_HEADS, MLSTM_QK_DIM, MLSTM_V_DIM), F32),
                        pltpu.VMEM((SUBLANES, LANES), F32),
                        pltpu.VMEM((SUBLANES, LANES), F32),
                        pltpu.VMEM((CONV_TAIL_ROWS + MLSTM_CHUNK, 2 * ML_QK_W), F32)],
        compiler_params=pltpu.CompilerParams(dimension_semantics=("arbitrary",),
                                             vmem_limit_bytes=VMEM_LIMIT_BYTES),
        name="mlstm_merge_ffn2",
    )(att, ga, gm, x1, qkm, vm, om, gi, gf, cw, cb, bi, bf, hn, watt, wml, wout, g2, wg, wu, wd, gfin)


def _rope_tables(seq):
    half = ATT_HEAD_DIM // 2
    pos = jnp.arange(seq, dtype=F32)
    inv_freq = ROPE_THETA ** (-jnp.arange(half, dtype=F32) / half)
    lane = np.arange(LANES)
    ang = pos[:, None] * inv_freq[lane % half][None, :]
    sign = jnp.asarray(np.where(lane % ATT_HEAD_DIM < half, -1.0, 1.0), F32)
    return jnp.cos(ang), jnp.sin(ang) * sign


def _pad_lanes(v):
    return jnp.pad(v, ((0, 0), (0, LANES - v.shape[-1])))


def _layer(x2d, batch, seq, p):
    (ffn1_norm, ffn1_w_gate, ffn1_w_up, ffn1_w_down, mix_norm, w_in, b_i, b_f, attn_sinks, conv_w, conv_b,
     head_norm, w_att, w_mlstm, w_out, ffn2_norm, ffn2_w_gate, ffn2_w_up, ffn2_w_down, final_norm) = p
    tm = min(TOKEN_TILE, seq)
    assert seq % tm == 0 and seq % ATT_BLOCK == 0 and seq % MLSTM_CHUNK == 0

    cos, sin = _rope_tables(seq)
    sinks = jnp.broadcast_to(attn_sinks[:, None], (ATT_HEADS, LANES)).astype(F32)

    x1, (w_q, w_mid, w_gate, w_if), (watt, wml, wout, wg2, wu2, wd2) = _ffn1(
        x2d, ffn1_norm[None], ffn1_w_gate.astype(BF16), ffn1_w_up.astype(BF16), ffn1_w_down.astype(BF16),
        w_in, (w_att, w_mlstm, w_out, ffn2_w_gate, ffn2_w_up, ffn2_w_down), tm)
    att, qkm, vm, om, ga, gm, gi, gf = _inproj(
        x1, mix_norm[None], cos, sin, w_q, w_mid, w_gate, w_if, sinks, tm, seq)
    return _out(att, ga, gm, x1, qkm, vm, om, gi, gf, conv_w, conv_b[None], _pad_lanes(b_i[None]),
                _pad_lanes(b_f[None]), head_norm[None], watt, wml, wout, ffn2_norm[None], wg2, wu2, wd2,
                final_norm[None], tm, seq)


def kernel(x, ffn1_norm, ffn1_w_gate, ffn1_w_up, ffn1_w_down, mix_norm, w_in, b_i, b_f, attn_sinks, conv_w,
           conv_b, head_norm, w_att, w_mlstm, w_out, ffn2_norm, ffn2_w_gate, ffn2_w_up, ffn2_w_down, final_norm):
    batch, seq, d = x.shape
    assert d == D_MODEL and ffn1_norm.shape[0] == 1, "single-layer kernel"
    per_layer = (ffn1_norm, ffn1_w_gate, ffn1_w_up, ffn1_w_down, mix_norm, w_in, b_i, b_f, attn_sinks, conv_w,
                 conv_b, head_norm, w_att, w_mlstm, w_out, ffn2_norm, ffn2_w_gate, ffn2_w_up, ffn2_w_down)
    params = tuple(a[0] for a in per_layer) + (final_norm,)
    out = _layer(x.reshape(batch * seq, d), batch, seq, params)
    return out.reshape(batch, seq, d)
```

```python
import functools

import jax
import jax.numpy as jnp
import numpy as np
from jax import lax
from jax.experimental import pallas as pl
from jax.experimental.pallas import tpu as pltpu

F32 = jnp.float32
BF16 = jnp.bfloat16

D_MODEL = 1024
ATT_HEADS = 16
ATT_KV_HEADS = 4
ATT_HEAD_DIM = 64
ATT_BLOCK = 128
ROPE_THETA = 10000.0
MLSTM_HEADS = 4
MLSTM_V_DIM = D_MODEL // MLSTM_HEADS
MLSTM_QK_DIM = MLSTM_V_DIM // 2
MLSTM_CHUNK = 128
MLSTM_CONV = 4
D_FF = 2816
RMS_EPS = 1e-5
NEG_LOG2E = -1.4426950408889634

ATT_Q_W = ATT_HEADS * ATT_HEAD_DIM
ATT_KV_W = ATT_KV_HEADS * ATT_HEAD_DIM
ML_QK_W = MLSTM_HEADS * MLSTM_QK_DIM
ML_V_W = MLSTM_HEADS * MLSTM_V_DIM

LANES = 128
MXU_COLS = 256
SUBLANES = 8
BF16_SUBLANES = 16
CONV_TAIL_ROWS = SUBLANES
VMEM_LIMIT_BYTES = 60 * 1024 * 1024

TOKEN_TILE = 512
FFN1_TOKEN_TILE = 1024
FF_CHUNKS = ((0, 768), (768, 1536), (1536, 2304), (2304, D_FF))

Q_CHUNKS_PER_GROUP = ATT_HEADS // ATT_KV_HEADS * ATT_HEAD_DIM // LANES

_C_KA = 0
_C_VA = _C_KA + ATT_KV_W
_C_QKM = _C_VA + ATT_KV_W
_C_VM = _C_QKM + 2 * ML_QK_W
_C_OM = _C_VM + ML_V_W
W_MID = _C_OM + ML_V_W

_IN_OFFS = tuple(int(v) for v in np.cumsum((0, ATT_Q_W, ATT_KV_W, ATT_KV_W, ML_QK_W, ML_QK_W, ML_V_W, ML_V_W,
                                            MLSTM_HEADS, MLSTM_HEADS, D_MODEL, D_MODEL)))
IN_PROJ_W = _IN_OFFS[-1]


def _resident(shape):
    return pl.BlockSpec(shape, lambda *_: (0,) * len(shape), pipeline_mode=pl.Buffered(1))


def _col_groups(lo, hi):
    return [(c, min(c + MXU_COLS, hi)) for c in range(lo, hi, MXU_COLS)]


def _interleave(mixer, dense):
    for n_dense in mixer:
        for _ in range(n_dense):
            next(dense, None)
    for _ in dense:
        pass


def _rms(x, g):
    ms = jnp.mean(x * x, axis=-1, keepdims=True)
    return x * lax.rsqrt(ms + RMS_EPS) * g


def _sigmoid(x):
    return 1.0 / (1.0 + jnp.exp2(x * NEG_LOG2E))


def _swiglu(h, wg_ref, wu_ref, wd_ref):
    acc = None
    for lo, hi in FF_CHUNKS:
        g = jnp.dot(h, wg_ref[:, lo:hi], preferred_element_type=F32)
        u = jnp.dot(h, wu_ref[:, lo:hi], preferred_element_type=F32)
        a = (g * _sigmoid(g) * u).astype(BF16)
        d = jnp.dot(a, wd_ref[lo:hi, :], preferred_element_type=F32)
        acc = d if acc is None else acc + d
    return acc


def _ffn1_kernel(x_ref, g_ref, wg_ref, wu_ref, wd_ref, win_ref, *refs):
    n_plain = (len(refs) - 5) // 2
    plain_in, (o_ref, wq_ref, wmid_ref, wgate_ref, wif_ref), plain_out = (
        refs[:n_plain], refs[n_plain:n_plain + 5], refs[n_plain + 5:])
    for src, dst in zip(plain_in, plain_out):
        dst[...] = src[...].astype(BF16)
    w = win_ref[...]
    wq_ref[...] = w[:, _IN_OFFS[0]:_IN_OFFS[1]].astype(BF16)
    wmid_ref[...] = w[:, _IN_OFFS[1]:_IN_OFFS[7]].astype(BF16)
    wgate_ref[...] = w[:, _IN_OFFS[9]:].astype(BF16)
    pad = jnp.zeros((w.shape[0], LANES - MLSTM_HEADS), F32)
    wif_ref[...] = jnp.concatenate([w[:, _IN_OFFS[7]:_IN_OFFS[8]], pad, w[:, _IN_OFFS[8]:_IN_OFFS[9]], pad],
                                   axis=1).astype(BF16)

    x = x_ref[...]
    h = _rms(x, g_ref[...]).astype(BF16)
    o_ref[...] = x + 0.5 * _swiglu(h, wg_ref, wu_ref, wd_ref)


def _row_block(rows, n_steps):
    return next(r for r in range(BF16_SUBLANES, rows + 1, BF16_SUBLANES)
                if rows % r == 0 and r * n_steps >= rows)


def _ffn1(x, g, wg, wu, wd, w_in, later_weights, tm):
    t = x.shape[0]
    n = t // tm
    tok = pl.BlockSpec((tm, D_MODEL), lambda i: (i, 0))

    def streamed(w):
        r = _row_block(w.shape[0], n)
        last = w.shape[0] // r - 1
        return lambda width: pl.BlockSpec((r, width), lambda i: (jnp.minimum(i, last), 0))

    in_spec = streamed(w_in)
    plain = [streamed(w) for w in later_weights]
    in_widths = (ATT_Q_W, W_MID, 2 * D_MODEL, 2 * LANES)
    out = pl.pallas_call(
        _ffn1_kernel,
        grid=(n,),
        in_specs=[tok, _resident((1, D_MODEL)), _resident((D_MODEL, D_FF)), _resident((D_MODEL, D_FF)),
                  _resident((D_FF, D_MODEL)), in_spec(IN_PROJ_W)]
        + [spec(w.shape[1]) for spec, w in zip(plain, later_weights)],
        out_specs=[tok] + [in_spec(w_) for w_ in in_widths]
        + [spec(w.shape[1]) for spec, w in zip(plain, later_weights)],
        out_shape=[jax.ShapeDtypeStruct((t, D_MODEL), F32)]
        + [jax.ShapeDtypeStruct((D_MODEL, w_), BF16) for w_ in in_widths]
        + [jax.ShapeDtypeStruct(w.shape, BF16) for w in later_weights],
        compiler_params=pltpu.CompilerParams(dimension_semantics=("arbitrary",),
                                             vmem_limit_bytes=VMEM_LIMIT_BYTES),
        name="ffn1",
    )(x, g, wg, wu, wd, w_in, *later_weights)
    return out[0], out[1:5], out[5:]


def _rope(x, cos, sin_signed):
    lane = lax.broadcasted_iota(jnp.int32, x.shape, 1)
    half = ATT_HEAD_DIM // 2
    partner = jnp.where(lane % ATT_HEAD_DIM < half, pltpu.roll(x, LANES - half, axis=1),
                        pltpu.roll(x, half, axis=1))
    return x * cos + partner * sin_signed


def _attn_scores(g, q_blk, k_cur, k_prev):
    low = lax.broadcasted_iota(jnp.int32, (ATT_BLOCK, LANES), 1) < ATT_HEAD_DIM
    c, in_high = divmod(g, 2)
    ksl = slice(c * LANES, (c + 1) * LANES)
    parts = []
    for k in (k_cur, k_prev):
        x = k[:, ksl].astype(F32)
        other = pltpu.roll(x, ATT_HEAD_DIM, axis=1)
        in_low_lanes, in_high_lanes = (other, x) if in_high else (x, other)
        parts.append((jnp.where(low, in_low_lanes, 0.0), jnp.where(low, 0.0, in_high_lanes)))
    kstack = jnp.concatenate([parts[0][0], parts[1][0], parts[0][1], parts[1][1]], axis=0).astype(BF16)
    qstack = jnp.concatenate(
        [q_blk[:, (g * Q_CHUNKS_PER_GROUP + r) * LANES:(g * Q_CHUNKS_PER_GROUP + r + 1) * LANES]
         for r in range(Q_CHUNKS_PER_GROUP)], axis=0)
    return lax.dot_general(kstack, qstack, (((1,), (1,)), ((), ())), preferred_element_type=F32)


def _attn_block(q_blk, kv_cur, kv_prev, has_prev, sink_ref, pt_scr, o_ref, rows):
    blk = ATT_BLOCK
    key = lax.broadcasted_iota(jnp.int32, (blk, blk), 0)
    qry = lax.broadcasted_iota(jnp.int32, (blk, blk), 1)
    in_cur = key <= qry
    neg_inf = jnp.float32(-jnp.inf)
    st = [_attn_scores(g, q_blk, kv_cur[:, :ATT_KV_W], kv_prev[:, :ATT_KV_W]) for g in range(ATT_KV_HEADS)]
    yield 1
    v_cat = jnp.concatenate([kv_cur[:, ATT_KV_W:], kv_prev[:, ATT_KV_W:]], axis=0).astype(F32)
    vt_all = v_cat.T.astype(BF16)
    for g in range(ATT_KV_HEADS):
        for r in range(Q_CHUNKS_PER_GROUP):
            qs = slice(r * blk, (r + 1) * blk)
            for half in range(2):
                head = (g * Q_CHUNKS_PER_GROUP + r) * 2 + half
                s_cur = st[g][(2 * half) * blk:(2 * half + 1) * blk, qs]
                s_prev = st[g][(2 * half + 1) * blk:(2 * half + 2) * blk, qs]
                s = jnp.where(in_cur, s_cur, jnp.where(has_prev, s_prev, neg_inf))
                sink = sink_ref[head:head + 1, :]
                m = jnp.maximum(jnp.max(s, axis=0, keepdims=True), sink)
                p = jnp.exp(s - m)
                denom = jnp.sum(p, axis=0, keepdims=True) + jnp.exp(sink - m)
                p = p * (1.0 / denom)
                zp = jnp.zeros_like(p)
                pt_scr[g * 2 + half, :blk, qs] = jnp.where(in_cur, p, zp).astype(BF16)
                pt_scr[g * 2 + half, blk:, qs] = jnp.where(in_cur, zp, p).astype(BF16)
        vt = vt_all[g * ATT_HEAD_DIM:(g + 1) * ATT_HEAD_DIM, :]
        ot = [jnp.dot(vt, pt_scr[g * 2 + half], preferred_element_type=F32) for half in range(2)]
        for r in range(Q_CHUNKS_PER_GROUP):
            j = g * Q_CHUNKS_PER_GROUP + r
            qs = slice(r * blk, (r + 1) * blk)
            o_t = jnp.concatenate([ot[0][:, qs], ot[1][:, qs]], axis=0)
            o_ref[rows, j * LANES:(j + 1) * LANES] = o_t.T.astype(BF16)
    yield 0


def _inproj_kernel(blocks_per_seq, x_ref, g_ref, cos_ref, sin_ref, wq_ref, wmid_ref, wgate_ref, wif_ref, sink_ref,
                   att_ref, qkm_ref, vm_ref, om_ref, ga_ref, gm_ref, gi_ref, gf_ref, q_scr, kv_scr, pt_scr):
    tm = x_ref.shape[0]
    blk = ATT_BLOCK
    step = pl.program_id(0)
    slot = step % 2
    q_new, kv_new = q_scr.at[slot], kv_scr.at[slot]
    q_old, kv_old = q_scr.at[1 - slot], kv_scr.at[1 - slot]

    @pl.when(step == 0)
    def _():
        q_scr[...] = jnp.zeros_like(q_scr)
        kv_scr[...] = jnp.zeros_like(kv_scr)

    h = _rms(x_ref[...], g_ref[...]).astype(BF16)
    kv_new[:blk, :] = kv_old[tm:, :]

    def mixer():
        for k in range(tm // blk):
            rows = slice(k * blk, (k + 1) * blk)
            has_prev = ((step - 1) * (tm // blk) + k) % blocks_per_seq != 0
            yield from _attn_block(q_old[rows, :], kv_old[blk + k * blk:2 * blk + k * blk, :],
                                   kv_old[k * blk:(k + 1) * blk, :], has_prev, sink_ref, pt_scr, att_ref, rows)

    def dense():
        def proj(w_ref, lo, width):
            return jnp.dot(h, w_ref[:, lo:lo + width], preferred_element_type=F32)

        z = proj(wq_ref, 0, ATT_Q_W) * (ATT_HEAD_DIM ** -0.5)
        cos, sin = cos_ref[...], sin_ref[...]
        for j in range(ATT_Q_W // LANES):
            sl = slice(j * LANES, (j + 1) * LANES)
            q_new[:, sl] = _rope(z[:, sl], cos, sin).astype(BF16)
        yield
        z = proj(wmid_ref, _C_KA, 2 * ATT_KV_W)
        for j in range(ATT_KV_W // LANES):
            sl = slice(j * LANES, (j + 1) * LANES)
            kv_new[blk:, sl] = _rope(z[:, sl], cos, sin).astype(BF16)
        kv_new[blk:, ATT_KV_W:] = z[:, ATT_KV_W:].astype(BF16)
        vm_ref[...] = proj(wmid_ref, _C_VM, ML_V_W).astype(BF16)
        yield
        om_ref[...] = proj(wmid_ref, _C_OM, ML_V_W).astype(BF16)
        ga_ref[...] = proj(wgate_ref, 0, D_MODEL).astype(BF16)
        yield
        gm_ref[...] = proj(wgate_ref, D_MODEL, D_MODEL).astype(BF16)
        z = proj(wif_ref, 0, 2 * LANES)
        gi_ref[...] = z[:, :LANES]
        gf_ref[...] = z[:, LANES:]
        qkm_ref[...] = proj(wmid_ref, _C_QKM, 2 * ML_QK_W).astype(BF16)
        yield

    _interleave(mixer(), dense())


def _inproj(x1, g, cos, sin, wq, wmid, wgate, wif, sinks, tm, seq):
    t = x1.shape[0]
    n = t // tm
    tiles_per_seq = seq // tm

    def cur(i):
        return jnp.minimum(i, n - 1)

    def tok(width):
        return pl.BlockSpec((tm, width), lambda i: (cur(i), 0))

    pos = pl.BlockSpec((tm, LANES), lambda i: (cur(i) % tiles_per_seq, 0))
    lagged = pl.BlockSpec((tm, ATT_Q_W), lambda i: (jnp.maximum(i - 1, 0), 0))
    widths = (2 * ML_QK_W, ML_V_W, ML_V_W, D_MODEL, D_MODEL)
    out_shape = [jax.ShapeDtypeStruct((t, ATT_Q_W), BF16)]
    out_shape += [jax.ShapeDtypeStruct((t, w_), BF16) for w_ in widths]
    out_shape += [jax.ShapeDtypeStruct((t, LANES), F32)] * 2
    return pl.pallas_call(
        functools.partial(_inproj_kernel, seq // ATT_BLOCK),
        grid=(n + 1,),
        in_specs=[tok(D_MODEL), _resident((1, D_MODEL)), pos, pos, _resident((D_MODEL, ATT_Q_W)),
                  _resident((D_MODEL, W_MID)), _resident((D_MODEL, 2 * D_MODEL)), _resident((D_MODEL, 2 * LANES)),
                  _resident((ATT_HEADS, LANES))],
        out_specs=[lagged] + [tok(w_) for w_ in widths] + [tok(LANES)] * 2,
        out_shape=out_shape,
        scratch_shapes=[pltpu.VMEM((2, tm, ATT_Q_W), BF16),
                        pltpu.VMEM((2, ATT_BLOCK + tm, 2 * ATT_KV_W), BF16),
                        pltpu.VMEM((2 * ATT_KV_HEADS, 2 * ATT_BLOCK, Q_CHUNKS_PER_GROUP * ATT_BLOCK), BF16)],
        compiler_params=pltpu.CompilerParams(dimension_semantics=("arbitrary",),
                                             vmem_limit_bytes=VMEM_LIMIT_BYTES),
        name="in_proj_attn",
    )(x1, g, cos, sin, wq, wmid, wgate, wif, sinks)


def _scan_rows(x, op, fill):
    n = x.shape[0]
    row = lax.broadcasted_iota(jnp.int32, x.shape, 0)
    shift = 1
    while shift < n:
        prev = jnp.where(row >= shift, pltpu.roll(x, shift, axis=0), fill)
        x = op(x, prev)
        shift *= 2
    return x


def _mlstm_chunk_inputs(rows, qk_ref, gi_ref, gf_ref, cw_ref, cb_ref, bi_ref, bf_ref, conv_scr):
    L = MLSTM_CHUNK
    tail = CONV_TAIL_ROWS
    x = qk_ref[rows, :].astype(F32)
    conv_scr[tail:, :] = x
    y = cb_ref[...] + cw_ref[MLSTM_CONV - 1:MLSTM_CONV, :] * x
    for j in range(1, MLSTM_CONV):
        y = y + cw_ref[MLSTM_CONV - 1 - j:MLSTM_CONV - j, :] * conv_scr[tail - j:tail - j + L, :]
    conv_scr[:tail, :] = x[L - tail:, :]
    qk = y * _sigmoid(y)

    ig = gi_ref[rows, :] + bi_ref[...]
    fpre = gf_ref[rows, :] + bf_ref[...]
    logf = jnp.minimum(fpre, 0.0) - jnp.log1p(jnp.exp(-jnp.abs(fpre)))
    b = _scan_rows(logf, jnp.add, 0.0)
    g = ig - b
    cm = _scan_rows(g, jnp.maximum, -jnp.inf)
    return qk, b, g, cm


def _mlstm_gates(b, g, cm, m_scr):
    L = MLSTM_CHUNK
    m_prev = m_scr[0:1, :]
    u = jnp.maximum(cm, m_prev)
    g_max = cm[L - 1:L, :]
    u_last = u[L - 1:L, :]
    gates = dict(u=u, a=jnp.exp(g - g_max), inter=jnp.exp(m_prev - u), emt=jnp.exp(-(b + u)),
                 s_old=jnp.exp(m_prev - u_last), s_new=jnp.exp(g_max - u_last),
                 g_rows=g.T)
    m_scr[0:1, :] = b[L - 1:L, :] + u_last
    return gates


def _mlstm_head(h, rows, qk, gates, v_ref, o_ref, hn_ref, out_ref, ct_scr, n_scr):
    L = MLSTM_CHUNK
    row = lax.broadcasted_iota(jnp.int32, (L, L), 0)
    col = lax.broadcasted_iota(jnp.int32, (L, L), 1)
    causal = col <= row
    qs = slice(h * MLSTM_QK_DIM, (h + 1) * MLSTM_QK_DIM)
    ks = slice(ML_QK_W + h * MLSTM_QK_DIM, ML_QK_W + (h + 1) * MLSTM_QK_DIM)
    vs = slice(h * MLSTM_V_DIM, (h + 1) * MLSTM_V_DIM)
    qh = qk[:, qs]
    kh = qk[:, ks] * (MLSTM_QK_DIM ** -0.5)
    qb = qh.astype(BF16)
    kb = kh.astype(BF16)
    vb = v_ref[rows, vs]
    inter_col = gates["inter"][:, h:h + 1]
    a_col = gates["a"][:, h:h + 1]

    decay = jnp.where(causal, jnp.exp(gates["g_rows"][h:h + 1, :] - gates["u"][:, h:h + 1]), 0.0)
    s = lax.dot_general(qb, kb, (((1,), (1,)), ((), ())), preferred_element_type=F32) * decay
    ct = ct_scr[h]
    n_row = n_scr[h:h + 1, :]
    yield 1
    num = (jnp.dot(s.astype(BF16), vb, preferred_element_type=F32)
           + inter_col * jnp.dot(qb, ct.astype(BF16), preferred_element_type=F32))
    den = jnp.sum(s + inter_col * (qh * n_row), axis=-1, keepdims=True)
    hh = num * (1.0 / jnp.maximum(jnp.abs(den), gates["emt"][:, h:h + 1]))
    hn = _rms(hh, hn_ref[:, vs])
    out_ref[rows, vs] = (_sigmoid(o_ref[rows, vs].astype(F32)) * hn).astype(BF16)
    yield 1
    av = (a_col * vb.astype(F32)).astype(BF16)
    d_ct = lax.dot_general(kb, av, (((0,), (0,)), ((), ())), preferred_element_type=F32)
    d_n = jnp.sum(a_col * kh, axis=0, keepdims=True)
    so = gates["s_old"][:, h:h + 1]
    sn = gates["s_new"][:, h:h + 1]
    ct_scr[h] = so * ct + sn * d_ct
    n_scr[h:h + 1, :] = so * n_row + sn * d_n
    yield 1


def _out_kernel(tiles_per_seq, n_tiles, att_ref, ga_ref, gm_ref, x1_ref, qk_ref, v_ref, o_ref, gi_ref, gf_ref,
                cw_ref, cb_ref, bi_ref, bf_ref, hn_ref, watt_ref, wml_ref, wout_ref, g2_ref, wg_ref, wu_ref,
                wd_ref, gfin_ref, out_ref, hm_scr, ct_scr, n_scr, m_scr, conv_scr):
    tm = x1_ref.shape[0]
    L = MLSTM_CHUNK
    step = pl.program_id(0)
    slot = step % 2
    hm_new, hm_old = hm_scr.at[slot], hm_scr.at[1 - slot]

    @pl.when(step == 0)
    def _():
        hm_scr[...] = jnp.zeros_like(hm_scr)

    @pl.when(jnp.minimum(step, n_tiles - 1) % tiles_per_seq == 0)
    def _():
        ct_scr[...] = jnp.zeros_like(ct_scr)
        n_scr[...] = jnp.zeros_like(n_scr)
        m_scr[...] = jnp.zeros_like(m_scr)
        conv_scr[:CONV_TAIL_ROWS, :] = jnp.zeros((CONV_TAIL_ROWS, conv_scr.shape[1]), F32)

    def chunk_inputs(c):
        return _mlstm_chunk_inputs(slice(c * L, (c + 1) * L), qk_ref, gi_ref, gf_ref, cw_ref, cb_ref, bi_ref,
                                   bf_ref, conv_scr)

    def mixer():
        for c in range(tm // L):
            rows = slice(c * L, (c + 1) * L)
            qk, b, g, cm = chunk_inputs(c)
            gates = _mlstm_gates(b, g, cm, m_scr)
            yield 1
            for h in range(MLSTM_HEADS):
                yield from _mlstm_head(h, rows, qk, gates, v_ref, o_ref, hn_ref, hm_new, ct_scr, n_scr)

    def dense():
        def dot(x, w):
            return jnp.dot(x, w, preferred_element_type=F32)

        cols = _col_groups(0, D_MODEL)
        ya, ym, x2p = [], [], []
        for lo, hi in cols:
            ya.append(dot(att_ref[...], watt_ref[:, lo:hi]))
            yield
        for lo, hi in cols:
            ym.append(dot(hm_old[...], wml_ref[:, lo:hi]))
            yield
        y = jnp.concatenate(
            [_sigmoid(ga_ref[:, lo:hi].astype(F32)) * ya[j] + _sigmoid(gm_ref[:, lo:hi].astype(F32)) * ym[j]
             for j, (lo, hi) in enumerate(cols)], axis=1).astype(BF16)
        for lo, hi in cols:
            x2p.append(dot(y, wout_ref[:, lo:hi]))
            yield
        x2 = x1_ref[...] + jnp.concatenate(x2p, axis=1)
        h2 = _rms(x2, g2_ref[...]).astype(BF16)
        acc = None
        for clo, chi in FF_CHUNKS:
            groups = _col_groups(clo, chi)
            g = []
            for lo, hi in groups:
                g.append(dot(h2, wg_ref[:, lo:hi]))
                yield
            a = []
            for j, (lo, hi) in enumerate(groups):
                u = dot(h2, wu_ref[:, lo:hi])
                a.append((g[j] * _sigmoid(g[j]) * u).astype(BF16))
                yield
            a = jnp.concatenate(a, axis=1)
            part = []
            for lo, hi in cols:
                part.append(dot(a, wd_ref[clo:chi, lo:hi]))
                yield
            acc = part if acc is None else [p + q for p, q in zip(acc, part)]
        x3 = x2 + 0.5 * jnp.concatenate(acc, axis=1)
        out_ref[...] = _rms(x3, gfin_ref[...])
        yield

    _interleave(mixer(), dense())


def _out(att, ga, gm, x1, qkm, vm, om, gi, gf, cw, cb, bi, bf, hn, watt, wml, wout, g2, wg, wu, wd, gfin, tm, seq):
    t = x1.shape[0]
    n = t // tm

    def cur(width):
        return pl.BlockSpec((tm, width), lambda i: (jnp.minimum(i, n - 1), 0))

    lagged = pl.BlockSpec((tm, D_MODEL), lambda i: (jnp.maximum(i - 1, 0), 0))
    sq = _resident((D_MODEL, D_MODEL))
    vec = _resident((1, D_MODEL))
    return pl.pallas_call(
        functools.partial(_out_kernel, seq // tm, n),
        grid=(n + 1,),
        in_specs=[lagged, lagged, lagged, lagged,
                  cur(2 * ML_QK_W), cur(ML_V_W), cur(ML_V_W), cur(LANES), cur(LANES),
                  _resident((MLSTM_CONV, 2 * ML_QK_W)), _resident((1, 2 * ML_QK_W)),
                  _resident((1, LANES)), _resident((1, LANES)), vec,
                  sq, sq, sq, vec, _resident((D_MODEL, D_FF)), _resident((D_MODEL, D_FF)),
                  _resident((D_FF, D_MODEL)), vec],
        out_specs=lagged,
        out_shape=jax.ShapeDtypeStruct((t, D_MODEL), F32),
        scratch_shapes=[pltpu.VMEM((2, tm, ML_V_W), BF16),
                        pltpu.VMEM((MLSTM_HEADS, MLSTM_QK_DIM, MLSTM_V_DIM), F32),
                        pltpu.VMEM((SUBLANES, LANES), F32),
                        pltpu.VMEM((SUBLANES, LANES), F32),
                        pltpu.VMEM((CONV_TAIL_ROWS + MLSTM_CHUNK, 2 * ML_QK_W), F32)],
        compiler_params=pltpu.CompilerParams(dimension_semantics=("arbitrary",),
                                             vmem_limit_bytes=VMEM_LIMIT_BYTES),
        name="mlstm_merge_ffn2",
    )(att, ga, gm, x1, qkm, vm, om, gi, gf, cw, cb, bi, bf, hn, watt, wml, wout, g2, wg, wu, wd, gfin)


def _rope_tables(seq):
    half = ATT_HEAD_DIM // 2
    pos = jnp.arange(seq, dtype=F32)
    inv_freq = ROPE_THETA ** (-jnp.arange(half, dtype=F32) / half)
    ang = pos[:, None] * inv_freq[None, :]
    lane = np.arange(LANES)
    sign = jnp.asarray(np.where(lane % ATT_HEAD_DIM < half, -1.0, 1.0), F32)
    return jnp.cos(ang)[:, lane % half], jnp.sin(ang)[:, lane % half] * sign


def _pad_lanes(v):
    return jnp.pad(v, ((0, 0), (0, LANES - v.shape[-1])))


def _layer(x2d, batch, seq, p):
    (ffn1_norm, ffn1_w_gate, ffn1_w_up, ffn1_w_down, mix_norm, w_in, b_i, b_f, attn_sinks, conv_w, conv_b,
     head_norm, w_att, w_mlstm, w_out, ffn2_norm, ffn2_w_gate, ffn2_w_up, ffn2_w_down, final_norm) = p
    tm = min(TOKEN_TILE, seq)
    assert seq % tm == 0 and seq % ATT_BLOCK == 0 and seq % MLSTM_CHUNK == 0

    cos, sin = _rope_tables(seq)
    sinks = jnp.broadcast_to(attn_sinks[:, None], (ATT_HEADS, LANES)).astype(F32)

    x1, (w_q, w_mid, w_gate, w_if), (watt, wml, wout, wg2, wu2, wd2) = _ffn1(
        x2d, ffn1_norm[None], ffn1_w_gate.astype(BF16), ffn1_w_up.astype(BF16), ffn1_w_down.astype(BF16),
        w_in, (w_att, w_mlstm, w_out, ffn2_w_gate, ffn2_w_up, ffn2_w_down), min(FFN1_TOKEN_TILE, batch * seq))
    att, qkm, vm, om, ga, gm, gi, gf = _inproj(
        x1, mix_norm[None], cos, sin, w_q, w_mid, w_gate, w_if, sinks, tm, seq)
    return _out(att, ga, gm, x1, qkm, vm, om, gi, gf, conv_w, conv_b[None], _pad_lanes(b_i[None]),
                _pad_lanes(b_f[None]), head_norm[None], watt, wml, wout, ffn2_norm[None], wg2, wu2, wd2,
                final_norm[None], tm, seq)


def kernel(x, ffn1_norm, ffn1_w_gate, ffn1_w_up, ffn1_w_down, mix_norm, w_in, b_i, b_f, attn_sinks, conv_w,
           conv_b, head_norm, w_att, w_mlstm, w_out, ffn2_norm, ffn2_w_gate, ffn2_w_up, ffn2_w_down, final_norm):
    batch, seq, d = x.shape
    assert d == D_MODEL and ffn1_norm.shape[0] == 1, "single-layer kernel"
    per_layer = (ffn1_norm, ffn1_w_gate, ffn1_w_up, ffn1_w_down, mix_norm, w_in, b_i, b_f, attn_sinks, conv_w,
                 conv_b, head_norm, w_att, w_mlstm, w_out, ffn2_norm, ffn2_w_gate, ffn2_w_up, ffn2_w_down)
    params = tuple(a[0] for a in per_layer) + (final_norm,)
    out = _layer(x.reshape(batch * seq, d), batch, seq, params)
    return out.reshape(batch, seq, d)
```

```python
import functools

import jax
import jax.numpy as jnp
import numpy as np
from jax import lax
from jax.experimental import pallas as pl
from jax.experimental.pallas import tpu as pltpu

F32 = jnp.float32
BF16 = jnp.bfloat16

D_MODEL = 1024
ATT_HEADS = 16
ATT_KV_HEADS = 4
ATT_HEAD_DIM = 64
ATT_BLOCK = 128
ROPE_THETA = 10000.0
MLSTM_HEADS = 4
MLSTM_V_DIM = D_MODEL // MLSTM_HEADS
MLSTM_QK_DIM = MLSTM_V_DIM // 2
MLSTM_CHUNK = 128
MLSTM_CONV = 4
D_FF = 2816
RMS_EPS = 1e-5
NEG_LOG2E = -1.4426950408889634

ATT_Q_W = ATT_HEADS * ATT_HEAD_DIM
ATT_KV_W = ATT_KV_HEADS * ATT_HEAD_DIM
ML_QK_W = MLSTM_HEADS * MLSTM_QK_DIM
ML_V_W = MLSTM_HEADS * MLSTM_V_DIM

LANES = 128
MXU_COLS = 256
SUBLANES = 8
BF16_SUBLANES = 16
CONV_TAIL_ROWS = SUBLANES
VMEM_LIMIT_BYTES = 60 * 1024 * 1024

TOKEN_TILE = 512
FFN1_TOKEN_TILE = 1024
FF_CHUNKS = ((0, 768), (768, 1536), (1536, 2304), (2304, D_FF))

Q_CHUNKS_PER_GROUP = ATT_HEADS // ATT_KV_HEADS * ATT_HEAD_DIM // LANES

_C_KA = 0
_C_VA = _C_KA + ATT_KV_W
_C_QKM = _C_VA + ATT_KV_W
_C_VM = _C_QKM + 2 * ML_QK_W
_C_OM = _C_VM + ML_V_W
W_MID = _C_OM + ML_V_W

_IN_OFFS = tuple(int(v) for v in np.cumsum((0, ATT_Q_W, ATT_KV_W, ATT_KV_W, ML_QK_W, ML_QK_W, ML_V_W, ML_V_W,
                                            MLSTM_HEADS, MLSTM_HEADS, D_MODEL, D_MODEL)))
IN_PROJ_W = _IN_OFFS[-1]


def _resident(shape):
    return pl.BlockSpec(shape, lambda *_: (0,) * len(shape), pipeline_mode=pl.Buffered(1))


def _col_groups(lo, hi):
    return [(c, min(c + MXU_COLS, hi)) for c in range(lo, hi, MXU_COLS)]


def _interleave(mixer, dense):
    for n_dense in mixer:
        for _ in range(n_dense):
            next(dense, None)
    for _ in dense:
        pass


def _rms(x, g):
    ms = jnp.mean(x * x, axis=-1, keepdims=True)
    return x * lax.rsqrt(ms + RMS_EPS) * g


def _sigmoid(x):
    return 1.0 / (1.0 + jnp.exp2(x * NEG_LOG2E))


def _swiglu(h, wg_ref, wu_ref, wd_ref):
    acc = None
    for lo, hi in FF_CHUNKS:
        g = jnp.dot(h, wg_ref[:, lo:hi], preferred_element_type=F32)
        u = jnp.dot(h, wu_ref[:, lo:hi], preferred_element_type=F32)
        a = (g * _sigmoid(g) * u).astype(BF16)
        d = jnp.dot(a, wd_ref[lo:hi, :], preferred_element_type=F32)
        acc = d if acc is None else acc + d
    return acc


def _ffn1_kernel(x_ref, g_ref, wg_ref, wu_ref, wd_ref, win_ref, *refs):
    n_plain = (len(refs) - 5) // 2
    plain_in, (o_ref, wq_ref, wmid_ref, wgate_ref, wif_ref), plain_out = (
        refs[:n_plain], refs[n_plain:n_plain + 5], refs[n_plain + 5:])
    for src, dst in zip(plain_in, plain_out):
        dst[...] = src[...].astype(BF16)
    w = win_ref[...]
    wq_ref[...] = w[:, _IN_OFFS[0]:_IN_OFFS[1]].astype(BF16)
    wmid_ref[...] = w[:, _IN_OFFS[1]:_IN_OFFS[7]].astype(BF16)
    wgate_ref[...] = w[:, _IN_OFFS[9]:].astype(BF16)
    pad = jnp.zeros((w.shape[0], LANES - MLSTM_HEADS), F32)
    wif_ref[...] = jnp.concatenate([w[:, _IN_OFFS[7]:_IN_OFFS[8]], pad, w[:, _IN_OFFS[8]:_IN_OFFS[9]], pad],
                                   axis=1).astype(BF16)

    x = x_ref[...]
    h = _rms(x, g_ref[...]).astype(BF16)
    o_ref[...] = x + 0.5 * _swiglu(h, wg_ref, wu_ref, wd_ref)


def _row_block(rows, n_steps):
    return next(r for r in range(BF16_SUBLANES, rows + 1, BF16_SUBLANES)
                if rows % r == 0 and r * n_steps >= rows)


def _ffn1(x, g, wg, wu, wd, w_in, later_weights, tm):
    t = x.shape[0]
    n = t // tm
    tok = pl.BlockSpec((tm, D_MODEL), lambda i: (i, 0))

    def streamed(w):
        r = _row_block(w.shape[0], n)
        last = w.shape[0] // r - 1
        return lambda width: pl.BlockSpec((r, width), lambda i: (jnp.minimum(i, last), 0))

    in_spec = streamed(w_in)
    plain = [streamed(w) for w in later_weights]
    in_widths = (ATT_Q_W, W_MID, 2 * D_MODEL, 2 * LANES)
    out = pl.pallas_call(
        _ffn1_kernel,
        grid=(n,),
        in_specs=[tok, _resident((1, D_MODEL)), _resident((D_MODEL, D_FF)), _resident((D_MODEL, D_FF)),
                  _resident((D_FF, D_MODEL)), in_spec(IN_PROJ_W)]
        + [spec(w.shape[1]) for spec, w in zip(plain, later_weights)],
        out_specs=[tok] + [in_spec(w_) for w_ in in_widths]
        + [spec(w.shape[1]) for spec, w in zip(plain, later_weights)],
        out_shape=[jax.ShapeDtypeStruct((t, D_MODEL), F32)]
        + [jax.ShapeDtypeStruct((D_MODEL, w_), BF16) for w_ in in_widths]
        + [jax.ShapeDtypeStruct(w.shape, BF16) for w in later_weights],
        compiler_params=pltpu.CompilerParams(dimension_semantics=("arbitrary",),
                                             vmem_limit_bytes=VMEM_LIMIT_BYTES),
        name="ffn1",
    )(x, g, wg, wu, wd, w_in, *later_weights)
    return out[0], out[1:5], out[5:]


def _rope(x, cos, sin_signed):
    lane = lax.broadcasted_iota(jnp.int32, x.shape, 1)
    half = ATT_HEAD_DIM // 2
    partner = jnp.where(lane % ATT_HEAD_DIM < half, pltpu.roll(x, LANES - half, axis=1),
                        pltpu.roll(x, half, axis=1))
    return x * cos + partner * sin_signed


def _attn_scores(g, q_blk, k_cur, k_prev):
    low = lax.broadcasted_iota(jnp.int32, (ATT_BLOCK, LANES), 1) < ATT_HEAD_DIM
    c, in_high = divmod(g, 2)
    ksl = slice(c * LANES, (c + 1) * LANES)
    parts = []
    for k in (k_cur, k_prev):
        x = k[:, ksl].astype(F32)
        other = pltpu.roll(x, ATT_HEAD_DIM, axis=1)
        in_low_lanes, in_high_lanes = (other, x) if in_high else (x, other)
        parts.append((jnp.where(low, in_low_lanes, 0.0), jnp.where(low, 0.0, in_high_lanes)))
    kstack = jnp.concatenate([parts[0][0], parts[1][0], parts[0][1], parts[1][1]], axis=0).astype(BF16)
    qstack = jnp.concatenate(
        [q_blk[:, (g * Q_CHUNKS_PER_GROUP + r) * LANES:(g * Q_CHUNKS_PER_GROUP + r + 1) * LANES]
         for r in range(Q_CHUNKS_PER_GROUP)], axis=0)
    return lax.dot_general(kstack, qstack, (((1,), (1,)), ((), ())), preferred_element_type=F32)


def _attn_block(q_blk, kv_cur, kv_prev, has_prev, sink_ref, pt_scr, o_ref, rows):
    blk = ATT_BLOCK
    key = lax.broadcasted_iota(jnp.int32, (blk, blk), 0)
    qry = lax.broadcasted_iota(jnp.int32, (blk, blk), 1)
    in_cur = key <= qry
    neg_inf = jnp.float32(-jnp.inf)
    st = [_attn_scores(g, q_blk, kv_cur[:, :ATT_KV_W], kv_prev[:, :ATT_KV_W]) for g in range(ATT_KV_HEADS)]
    yield 1
    v_cat = jnp.concatenate([kv_cur[:, ATT_KV_W:], kv_prev[:, ATT_KV_W:]], axis=0).astype(F32)
    vt_all = v_cat.T.astype(BF16)
    for g in range(ATT_KV_HEADS):
        for r in range(Q_CHUNKS_PER_GROUP):
            qs = slice(r * blk, (r + 1) * blk)
            for half in range(2):
                head = (g * Q_CHUNKS_PER_GROUP + r) * 2 + half
                s_cur = st[g][(2 * half) * blk:(2 * half + 1) * blk, qs]
                s_prev = st[g][(2 * half + 1) * blk:(2 * half + 2) * blk, qs]
                s = jnp.where(in_cur, s_cur, jnp.where(has_prev, s_prev, neg_inf))
                sink = sink_ref[head:head + 1, :]
                m = jnp.maximum(jnp.max(s, axis=0, keepdims=True), sink)
                p = jnp.exp(s - m)
                denom = jnp.sum(p, axis=0, keepdims=True) + jnp.exp(sink - m)
                p = p * (1.0 / denom)
                zp = jnp.zeros_like(p)
                pt_scr[g * 2 + half, :blk, qs] = jnp.where(in_cur, p, zp).astype(BF16)
                pt_scr[g * 2 + half, blk:, qs] = jnp.where(in_cur, zp, p).astype(BF16)
        vt = vt_all[g * ATT_HEAD_DIM:(g + 1) * ATT_HEAD_DIM, :]
        ot = [jnp.dot(vt, pt_scr[g * 2 + half], preferred_element_type=F32) for half in range(2)]
        for r in range(Q_CHUNKS_PER_GROUP):
            j = g * Q_CHUNKS_PER_GROUP + r
            qs = slice(r * blk, (r + 1) * blk)
            o_t = jnp.concatenate([ot[0][:, qs], ot[1][:, qs]], axis=0)
            o_ref[rows, j * LANES:(j + 1) * LANES] = o_t.T.astype(BF16)
    yield 0


def _inproj_kernel(blocks_per_seq, x_ref, g_ref, cos_ref, sin_ref, wq_ref, wmid_ref, wgate_ref, wif_ref, sink_ref,
                   att_ref, qkm_ref, vm_ref, om_ref, ga_ref, gm_ref, gi_ref, gf_ref, q_scr, kv_scr, pt_scr):
    tm = x_ref.shape[0]
    blk = ATT_BLOCK
    step = pl.program_id(0)
    slot = step % 2
    q_new, kv_new = q_scr.at[slot], kv_scr.at[slot]
    q_old, kv_old = q_scr.at[1 - slot], kv_scr.at[1 - slot]

    @pl.when(step == 0)
    def _():
        q_scr[...] = jnp.zeros_like(q_scr)
        kv_scr[...] = jnp.zeros_like(kv_scr)

    h = _rms(x_ref[...], g_ref[...]).astype(BF16)
    kv_new[:blk, :] = kv_old[tm:, :]

    def mixer():
        for k in range(tm // blk):
            rows = slice(k * blk, (k + 1) * blk)
            has_prev = ((step - 1) * (tm // blk) + k) % blocks_per_seq != 0
            yield from _attn_block(q_old[rows, :], kv_old[blk + k * blk:2 * blk + k * blk, :],
                                   kv_old[k * blk:(k + 1) * blk, :], has_prev, sink_ref, pt_scr, att_ref, rows)

    def dense():
        def proj(w_ref, lo, width):
            return jnp.dot(h, w_ref[:, lo:lo + width], preferred_element_type=F32)

        z = proj(wq_ref, 0, ATT_Q_W) * (ATT_HEAD_DIM ** -0.5)
        cos, sin = cos_ref[...], sin_ref[...]
        for j in range(ATT_Q_W // LANES):
            sl = slice(j * LANES, (j + 1) * LANES)
            q_new[:, sl] = _rope(z[:, sl], cos, sin).astype(BF16)
        yield
        z = proj(wmid_ref, _C_KA, 2 * ATT_KV_W)
        for j in range(ATT_KV_W // LANES):
            sl = slice(j * LANES, (j + 1) * LANES)
            kv_new[blk:, sl] = _rope(z[:, sl], cos, sin).astype(BF16)
        kv_new[blk:, ATT_KV_W:] = z[:, ATT_KV_W:].astype(BF16)
        vm_ref[...] = proj(wmid_ref, _C_VM, ML_V_W).astype(BF16)
        yield
        om_ref[...] = proj(wmid_ref, _C_OM, ML_V_W).astype(BF16)
        ga_ref[...] = proj(wgate_ref, 0, D_MODEL).astype(BF16)
        yield
        gm_ref[...] = proj(wgate_ref, D_MODEL, D_MODEL).astype(BF16)
        z = proj(wif_ref, 0, 2 * LANES)
        gi_ref[...] = z[:, :LANES]
        gf_ref[...] = z[:, LANES:]
        qkm_ref[...] = proj(wmid_ref, _C_QKM, 2 * ML_QK_W).astype(BF16)
        yield

    _interleave(mixer(), dense())


def _inproj(x1, g, cos, sin, wq, wmid, wgate, wif, sinks, tm, seq):
    t = x1.shape[0]
    n = t // tm
    tiles_per_seq = seq // tm

    def cur(i):
        return jnp.minimum(i, n - 1)

    def tok(width):
        return pl.BlockSpec((tm, width), lambda i: (cur(i), 0))

    pos = pl.BlockSpec((tm, LANES), lambda i: (cur(i) % tiles_per_seq, 0))
    lagged = pl.BlockSpec((tm, ATT_Q_W), lambda i: (jnp.maximum(i - 1, 0), 0))
    widths = (2 * ML_QK_W, ML_V_W, ML_V_W, D_MODEL, D_MODEL)
    out_shape = [jax.ShapeDtypeStruct((t, ATT_Q_W), BF16)]
    out_shape += [jax.ShapeDtypeStruct((t, w_), BF16) for w_ in widths]
    out_shape += [jax.ShapeDtypeStruct((t, LANES), F32)] * 2
    return pl.pallas_call(
        functools.partial(_inproj_kernel, seq // ATT_BLOCK),
        grid=(n + 1,),
        in_specs=[tok(D_MODEL), _resident((1, D_MODEL)), pos, pos, _resident((D_MODEL, ATT_Q_W)),
                  _resident((D_MODEL, W_MID)), _resident((D_MODEL, 2 * D_MODEL)), _resident((D_MODEL, 2 * LANES)),
                  _resident((ATT_HEADS, LANES))],
        out_specs=[lagged] + [tok(w_) for w_ in widths] + [tok(LANES)] * 2,
        out_shape=out_shape,
        scratch_shapes=[pltpu.VMEM((2, tm, ATT_Q_W), BF16),
                        pltpu.VMEM((2, ATT_BLOCK + tm, 2 * ATT_KV_W), BF16),
                        pltpu.VMEM((2 * ATT_KV_HEADS, 2 * ATT_BLOCK, Q_CHUNKS_PER_GROUP * ATT_BLOCK), BF16)],
        compiler_params=pltpu.CompilerParams(dimension_semantics=("arbitrary",),
                                             vmem_limit_bytes=VMEM_LIMIT_BYTES),
        name="in_proj_attn",
    )(x1, g, cos, sin, wq, wmid, wgate, wif, sinks)


def _scan_rows(x, op, fill):
    n = x.shape[0]
    row = lax.broadcasted_iota(jnp.int32, x.shape, 0)
    shift = 1
    while shift < n:
        prev = jnp.where(row >= shift, pltpu.roll(x, shift, axis=0), fill)
        x = op(x, prev)
        shift *= 2
    return x


def _mlstm_chunk_inputs(rows, qk_ref, gi_ref, gf_ref, cw_ref, cb_ref, bi_ref, bf_ref, conv_scr):
    L = MLSTM_CHUNK
    tail = CONV_TAIL_ROWS
    x = qk_ref[rows, :].astype(F32)
    conv_scr[tail:, :] = x
    y = cb_ref[...] + cw_ref[MLSTM_CONV - 1:MLSTM_CONV, :] * x
    for j in range(1, MLSTM_CONV):
        y = y + cw_ref[MLSTM_CONV - 1 - j:MLSTM_CONV - j, :] * conv_scr[tail - j:tail - j + L, :]
    conv_scr[:tail, :] = x[L - tail:, :]
    qk = y * _sigmoid(y)

    ig = gi_ref[rows, :] + bi_ref[...]
    fpre = (gf_ref[rows, :] + bf_ref[...]).T[:SUBLANES, :]
    logf = jnp.minimum(fpre, 0.0) - jnp.log1p(jnp.exp(-jnp.abs(fpre)))
    logf = jnp.concatenate([logf, jnp.zeros((LANES - SUBLANES, L), F32)], axis=0).T
    b = _scan_rows(logf, jnp.add, 0.0)
    g = ig - b
    cm = _scan_rows(g, jnp.maximum, -jnp.inf)
    return qk, b, g, cm


def _mlstm_gates(b, g, cm, m_scr):
    L = MLSTM_CHUNK
    m_prev = m_scr[0:1, :]
    u = jnp.maximum(cm, m_prev)
    g_max = cm[L - 1:L, :]
    u_last = u[L - 1:L, :]
    gates = dict(u=u, a=jnp.exp(g - g_max), inter=jnp.exp(m_prev - u), emt=jnp.exp(-(b + u)),
                 s_old=jnp.exp(m_prev - u_last), s_new=jnp.exp(g_max - u_last),
                 g_rows=g.T)
    m_scr[0:1, :] = b[L - 1:L, :] + u_last
    return gates


def _mlstm_head(h, rows, qk, gates, v_ref, o_ref, hn_ref, out_ref, ct_scr, n_scr):
    L = MLSTM_CHUNK
    row = lax.broadcasted_iota(jnp.int32, (L, L), 0)
    col = lax.broadcasted_iota(jnp.int32, (L, L), 1)
    causal = col <= row
    qs = slice(h * MLSTM_QK_DIM, (h + 1) * MLSTM_QK_DIM)
    ks = slice(ML_QK_W + h * MLSTM_QK_DIM, ML_QK_W + (h + 1) * MLSTM_QK_DIM)
    vs = slice(h * MLSTM_V_DIM, (h + 1) * MLSTM_V_DIM)
    qh = qk[:, qs]
    kh = qk[:, ks] * (MLSTM_QK_DIM ** -0.5)
    qb = qh.astype(BF16)
    kb = kh.astype(BF16)
    vb = v_ref[rows, vs]
    inter_col = gates["inter"][:, h:h + 1]
    a_col = gates["a"][:, h:h + 1]

    decay = jnp.where(causal, jnp.exp(gates["g_rows"][h:h + 1, :] - gates["u"][:, h:h + 1]), 0.0)
    s = lax.dot_general(qb, kb, (((1,), (1,)), ((), ())), preferred_element_type=F32) * decay
    ct = ct_scr[h]
    n_row = n_scr[h:h + 1, :]
    yield 1
    num = (jnp.dot(s.astype(BF16), vb, preferred_element_type=F32)
           + inter_col * jnp.dot(qb, ct.astype(BF16), preferred_element_type=F32))
    den = jnp.sum(s + inter_col * (qh * n_row), axis=-1, keepdims=True)
    hh = num * (1.0 / jnp.maximum(jnp.abs(den), gates["emt"][:, h:h + 1]))
    hn = _rms(hh, hn_ref[:, vs])
    out_ref[rows, vs] = (_sigmoid(o_ref[rows, vs].astype(F32)) * hn).astype(BF16)
    yield 1
    av = (a_col * vb.astype(F32)).astype(BF16)
    d_ct = lax.dot_general(kb, av, (((0,), (0,)), ((), ())), preferred_element_type=F32)
    d_n = jnp.sum(a_col * kh, axis=0, keepdims=True)
    so = gates["s_old"][:, h:h + 1]
    sn = gates["s_new"][:, h:h + 1]
    ct_scr[h] = so * ct + sn * d_ct
    n_scr[h:h + 1, :] = so * n_row + sn * d_n
    yield 1


def _out_kernel(tiles_per_seq, n_tiles, att_ref, ga_ref, gm_ref, x1_ref, qk_ref, v_ref, o_ref, gi_ref, gf_ref,
                cw_ref, cb_ref, bi_ref, bf_ref, hn_ref, watt_ref, wml_ref, wout_ref, g2_ref, wg_ref, wu_ref,
                wd_ref, gfin_ref, out_ref, hm_scr, ct_scr, n_scr, m_scr, conv_scr):
    tm = x1_ref.shape[0]
    L = MLSTM_CHUNK
    step = pl.program_id(0)
    slot = step % 2
    hm_new, hm_old = hm_scr.at[slot], hm_scr.at[1 - slot]

    @pl.when(step == 0)
    def _():
        hm_scr[...] = jnp.zeros_like(hm_scr)

    @pl.when(jnp.minimum(step, n_tiles - 1) % tiles_per_seq == 0)
    def _():
        ct_scr[...] = jnp.zeros_like(ct_scr)
        n_scr[...] = jnp.zeros_like(n_scr)
        m_scr[...] = jnp.zeros_like(m_scr)
        conv_scr[:CONV_TAIL_ROWS, :] = jnp.zeros((CONV_TAIL_ROWS, conv_scr.shape[1]), F32)

    def chunk_inputs(c):
        return _mlstm_chunk_inputs(slice(c * L, (c + 1) * L), qk_ref, gi_ref, gf_ref, cw_ref, cb_ref, bi_ref,
                                   bf_ref, conv_scr)

    def mixer():
        for c in range(tm // L):
            rows = slice(c * L, (c + 1) * L)
            qk, b, g, cm = chunk_inputs(c)
            gates = _mlstm_gates(b, g, cm, m_scr)
            yield 1
            for h in range(MLSTM_HEADS):
                yield from _mlstm_head(h, rows, qk, gates, v_ref, o_ref, hn_ref, hm_new, ct_scr, n_scr)

    def dense():
        def dot(x, w):
            return jnp.dot(x, w, preferred_element_type=F32)

        cols = _col_groups(0, D_MODEL)
        ya, ym, x2p = [], [], []
        for lo, hi in cols:
            ya.append(dot(att_ref[...], watt_ref[:, lo:hi]))
            yield
        for lo, hi in cols:
            ym.append(dot(hm_old[...], wml_ref[:, lo:hi]))
            yield
        y = jnp.concatenate(
            [_sigmoid(ga_ref[:, lo:hi].astype(F32)) * ya[j] + _sigmoid(gm_ref[:, lo:hi].astype(F32)) * ym[j]
             for j, (lo, hi) in enumerate(cols)], axis=1).astype(BF16)
        for lo, hi in cols:
            x2p.append(dot(y, wout_ref[:, lo:hi]))
            yield
        x2 = x1_ref[...] + jnp.concatenate(x2p, axis=1)
        h2 = _rms(x2, g2_ref[...]).astype(BF16)
        acc = None
        for clo, chi in FF_CHUNKS:
            groups = _col_groups(clo, chi)
            g = []
            for lo, hi in groups:
                g.append(dot(h2, wg_ref[:, lo:hi]))
                yield
            a = []
            for j, (lo, hi) in enumerate(groups):
                u = dot(h2, wu_ref[:, lo:hi])
                a.append((g[j] * _sigmoid(g[j]) * u).astype(BF16))
                yield
            a = jnp.concatenate(a, axis=1)
            part = []
            for lo, hi in cols:
                part.append(dot(a, wd_ref[clo:chi, lo:hi]))
                yield
            acc = part if acc is None else [p + q for p, q in zip(acc, part)]
        x3 = x2 + 0.5 * jnp.concatenate(acc, axis=1)
        out_ref[...] = _rms(x3, gfin_ref[...])
        yield

    _interleave(mixer(), dense())


def _out(att, ga, gm, x1, qkm, vm, om, gi, gf, cw, cb, bi, bf, hn, watt, wml, wout, g2, wg, wu, wd, gfin, tm, seq):
    t = x1.shape[0]
    n = t // tm

    def cur(width):
        return pl.BlockSpec((tm, width), lambda i: (jnp.minimum(i, n - 1), 0))

    lagged = pl.BlockSpec((tm, D_MODEL), lambda i: (jnp.maximum(i - 1, 0), 0))
    sq = _resident((D_MODEL, D_MODEL))
    vec = _resident((1, D_MODEL))
    return pl.pallas_call(
        functools.partial(_out_kernel, seq // tm, n),
        grid=(n + 1,),
        in_specs=[lagged, lagged, lagged, lagged,
                  cur(2 * ML_QK_W), cur(ML_V_W), cur(ML_V_W), cur(LANES), cur(LANES),
                  _resident((MLSTM_CONV, 2 * ML_QK_W)), _resident((1, 2 * ML_QK_W)),
                  _resident((1, LANES)), _resident((1, LANES)), vec,
                  sq, sq, sq, vec, _resident((D_MODEL, D_FF)), _resident((D_MODEL, D_FF)),
                  _resident((D_FF, D_MODEL)), vec],
        out_specs=lagged,
        out_shape=jax.ShapeDtypeStruct((t, D_MODEL), F32),
        scratch_shapes=[pltpu.VMEM((2, tm, ML_V_W), BF16),
                        pltpu.VMEM((MLSTM_HEADS, MLSTM_QK_DIM, MLSTM_V_DIM), F32),
                        pltpu.VMEM((SUBLANES, LANES), F32),
                        pltpu.VMEM((SUBLANES, LANES), F32),
                        pltpu.VMEM((CONV_TAIL_ROWS + MLSTM_CHUNK, 2 * ML_QK_W), F32)],
        compiler_params=pltpu.CompilerParams(dimension_semantics=("arbitrary",),
                                             vmem_limit_bytes=VMEM_LIMIT_BYTES),
        name="mlstm_merge_ffn2",
    )(att, ga, gm, x1, qkm, vm, om, gi, gf, cw, cb, bi, bf, hn, watt, wml, wout, g2, wg, wu, wd, gfin)


def _rope_tables(seq):
    half = ATT_HEAD_DIM // 2
    pos = jnp.arange(seq, dtype=F32)
    inv_freq = ROPE_THETA ** (-jnp.arange(half, dtype=F32) / half)
    ang = pos[:, None] * inv_freq[None, :]
    lane = np.arange(LANES)
    sign = jnp.asarray(np.where(lane % ATT_HEAD_DIM < half, -1.0, 1.0), F32)
    return jnp.cos(ang)[:, lane % half], jnp.sin(ang)[:, lane % half] * sign


def _pad_lanes(v):
    return jnp.pad(v, ((0, 0), (0, LANES - v.shape[-1])))


def _layer(x2d, batch, seq, p):
    (ffn1_norm, ffn1_w_gate, ffn1_w_up, ffn1_w_down, mix_norm, w_in, b_i, b_f, attn_sinks, conv_w, conv_b,
     head_norm, w_att, w_mlstm, w_out, ffn2_norm, ffn2_w_gate, ffn2_w_up, ffn2_w_down, final_norm) = p
    tm = min(TOKEN_TILE, seq)
    assert seq % tm == 0 and seq % ATT_BLOCK == 0 and seq % MLSTM_CHUNK == 0

    cos, sin = _rope_tables(seq)
    sinks = jnp.broadcast_to(attn_sinks[:, None], (ATT_HEADS, LANES)).astype(F32)

    x1, (w_q, w_mid, w_gate, w_if), (watt, wml, wout, wg2, wu2, wd2) = _ffn1(
        x2d, ffn1_norm[None], ffn1_w_gate.astype(BF16), ffn1_w_up.astype(BF16), ffn1_w_down.astype(BF16),
        w_in, (w_att, w_mlstm, w_out, ffn2_w_gate, ffn2_w_up, ffn2_w_down), min(FFN1_TOKEN_TILE, batch * seq))
    att, qkm, vm, om, ga, gm, gi, gf = _inproj(
        x1, mix_norm[None], cos, sin, w_q, w_mid, w_gate, w_if, sinks, tm, seq)
    return _out(att, ga, gm, x1, qkm, vm, om, gi, gf, conv_w, conv_b[None], _pad_lanes(b_i[None]),
                _pad_lanes(b_f[None]), head_norm[None], watt, wml, wout, ffn2_norm[None], wg2, wu2, wd2,
                final_norm[None], tm, seq)


def kernel(x, ffn1_norm, ffn1_w_gate, ffn1_w_up, ffn1_w_down, mix_norm, w_in, b_i, b_f, attn_sinks, conv_w,
           conv_b, head_norm, w_att, w_mlstm, w_out, ffn2_norm, ffn2_w_gate, ffn2_w_up, ffn2_w_down, final_norm):
    batch, seq, d = x.shape
    assert d == D_MODEL and ffn1_norm.shape[0] == 1, "single-layer kernel"
    per_layer = (ffn1_norm, ffn1_w_gate, ffn1_w_up, ffn1_w_down, mix_norm, w_in, b_i, b_f, attn_sinks, conv_w,
                 conv_b, head_norm, w_att, w_mlstm, w_out, ffn2_norm, ffn2_w_gate, ffn2_w_up, ffn2_w_down)
    params = tuple(a[0] for a in per_layer) + (final_norm,)
    out = _layer(x.reshape(batch * seq, d), batch, seq, params)
    return out.reshape(batch, seq, d)
```

```python
import functools

import jax
import jax.numpy as jnp
import numpy as np
from jax import lax
from jax.experimental import pallas as pl
from jax.experimental.pallas import tpu as pltpu

F32 = jnp.float32
BF16 = jnp.bfloat16

D_MODEL = 1024
ATT_HEADS = 16
ATT_KV_HEADS = 4
ATT_HEAD_DIM = 64
ATT_BLOCK = 128
ROPE_THETA = 10000.0
MLSTM_HEADS = 4
MLSTM_V_DIM = D_MODEL // MLSTM_HEADS
MLSTM_QK_DIM = MLSTM_V_DIM // 2
MLSTM_CHUNK = 128
MLSTM_CONV = 4
D_FF = 2816
RMS_EPS = 1e-5
NEG_LOG2E = -1.4426950408889634

ATT_Q_W = ATT_HEADS * ATT_HEAD_DIM
ATT_KV_W = ATT_KV_HEADS * ATT_HEAD_DIM
ML_QK_W = MLSTM_HEADS * MLSTM_QK_DIM
ML_V_W = MLSTM_HEADS * MLSTM_V_DIM

LANES = 128
MXU_COLS = 256
SUBLANES = 8
BF16_SUBLANES = 16
CONV_TAIL_ROWS = SUBLANES
VMEM_LIMIT_BYTES = 60 * 1024 * 1024

TOKEN_TILE = 512
FFN1_TOKEN_TILE = 1024
FF_CHUNKS = ((0, 768), (768, 1536), (1536, 2304), (2304, D_FF))

Q_CHUNKS_PER_GROUP = ATT_HEADS // ATT_KV_HEADS * ATT_HEAD_DIM // LANES

_C_KA = 0
_C_VA = _C_KA + ATT_KV_W
_C_QKM = _C_VA + ATT_KV_W
_C_VM = _C_QKM + 2 * ML_QK_W
_C_OM = _C_VM + ML_V_W
W_MID = _C_OM + ML_V_W

_IN_OFFS = tuple(int(v) for v in np.cumsum((0, ATT_Q_W, ATT_KV_W, ATT_KV_W, ML_QK_W, ML_QK_W, ML_V_W, ML_V_W,
                                            MLSTM_HEADS, MLSTM_HEADS, D_MODEL, D_MODEL)))
IN_PROJ_W = _IN_OFFS[-1]


def _resident(shape):
    return pl.BlockSpec(shape, lambda *_: (0,) * len(shape), pipeline_mode=pl.Buffered(1))


def _col_groups(lo, hi):
    return [(c, min(c + MXU_COLS, hi)) for c in range(lo, hi, MXU_COLS)]


def _interleave(mixer, dense):
    for n_dense in mixer:
        for _ in range(n_dense):
            next(dense, None)
    for _ in dense:
        pass


def _rms(x, g):
    ms = jnp.mean(x * x, axis=-1, keepdims=True)
    return x * lax.rsqrt(ms + RMS_EPS) * g


def _sigmoid(x):
    return 1.0 / (1.0 + jnp.exp2(x * NEG_LOG2E))


def _swiglu(h, wg_ref, wu_ref, wd_ref):
    acc = None
    for lo, hi in FF_CHUNKS:
        g = jnp.dot(h, wg_ref[:, lo:hi], preferred_element_type=F32)
        u = jnp.dot(h, wu_ref[:, lo:hi], preferred_element_type=F32)
        a = (g * _sigmoid(g) * u).astype(BF16)
        d = jnp.dot(a, wd_ref[lo:hi, :], preferred_element_type=F32)
        acc = d if acc is None else acc + d
    return acc


def _ffn1_kernel(x_ref, g_ref, wg_ref, wu_ref, wd_ref, win_ref, *refs):
    n_plain = (len(refs) - 5) // 2
    plain_in, (o_ref, wq_ref, wmid_ref, wgate_ref, wif_ref), plain_out = (
        refs[:n_plain], refs[n_plain:n_plain + 5], refs[n_plain + 5:])
    for src, dst in zip(plain_in, plain_out):
        dst[...] = src[...].astype(BF16)
    w = win_ref[...]
    wq_ref[...] = w[:, _IN_OFFS[0]:_IN_OFFS[1]].astype(BF16)
    wmid_ref[...] = w[:, _IN_OFFS[1]:_IN_OFFS[7]].astype(BF16)
    wgate_ref[...] = w[:, _IN_OFFS[9]:].astype(BF16)
    pad = jnp.zeros((w.shape[0], LANES - MLSTM_HEADS), F32)
    wif_ref[...] = jnp.concatenate([w[:, _IN_OFFS[7]:_IN_OFFS[8]], pad, w[:, _IN_OFFS[8]:_IN_OFFS[9]], pad],
                                   axis=1).astype(BF16)

    x = x_ref[...]
    h = _rms(x, g_ref[...]).astype(BF16)
    o_ref[...] = x + 0.5 * _swiglu(h, wg_ref, wu_ref, wd_ref)


def _row_block(rows, n_steps):
    return next(r for r in range(BF16_SUBLANES, rows + 1, BF16_SUBLANES)
                if rows % r == 0 and r * n_steps >= rows)


def _ffn1(x, g, wg, wu, wd, w_in, later_weights, tm):
    t = x.shape[0]
    assert t % tm == 0
    n = t // tm
    tok = pl.BlockSpec((tm, D_MODEL), lambda i: (i, 0))

    def streamed(w):
        r = _row_block(w.shape[0], n)
        last = w.shape[0] // r - 1
        return lambda width: pl.BlockSpec((r, width), lambda i: (jnp.minimum(i, last), 0))

    in_spec = streamed(w_in)
    plain = [streamed(w) for w in later_weights]
    in_widths = (ATT_Q_W, W_MID, 2 * D_MODEL, 2 * LANES)
    out = pl.pallas_call(
        _ffn1_kernel,
        grid=(n,),
        in_specs=[tok, _resident((1, D_MODEL)), _resident((D_MODEL, D_FF)), _resident((D_MODEL, D_FF)),
                  _resident((D_FF, D_MODEL)), in_spec(IN_PROJ_W)]
        + [spec(w.shape[1]) for spec, w in zip(plain, later_weights)],
        out_specs=[tok] + [in_spec(w_) for w_ in in_widths]
        + [spec(w.shape[1]) for spec, w in zip(plain, later_weights)],
        out_shape=[jax.ShapeDtypeStruct((t, D_MODEL), F32)]
        + [jax.ShapeDtypeStruct((D_MODEL, w_), BF16) for w_ in in_widths]
        + [jax.ShapeDtypeStruct(w.shape, BF16) for w in later_weights],
        compiler_params=pltpu.CompilerParams(dimension_semantics=("arbitrary",),
                                             vmem_limit_bytes=VMEM_LIMIT_BYTES),
        name="ffn1",
    )(x, g, wg, wu, wd, w_in, *later_weights)
    return out[0], out[1:5], out[5:]


def _rope(x, cos, sin_signed):
    lane = lax.broadcasted_iota(jnp.int32, x.shape, 1)
    half = ATT_HEAD_DIM // 2
    partner = jnp.where(lane % ATT_HEAD_DIM < half, pltpu.roll(x, LANES - half, axis=1),
                        pltpu.roll(x, half, axis=1))
    return x * cos + partner * sin_signed


def _attn_scores(g, q_blk, k_cur, k_prev):
    low = lax.broadcasted_iota(jnp.int32, (ATT_BLOCK, LANES), 1) < ATT_HEAD_DIM
    c, in_high = divmod(g, 2)
    ksl = slice(c * LANES, (c + 1) * LANES)
    parts = []
    for k in (k_cur, k_prev):
        x = k[:, ksl].astype(F32)
        other = pltpu.roll(x, ATT_HEAD_DIM, axis=1)
        in_low_lanes, in_high_lanes = (other, x) if in_high else (x, other)
        parts.append((jnp.where(low, in_low_lanes, 0.0), jnp.where(low, 0.0, in_high_lanes)))
    kstack = jnp.concatenate([parts[0][0], parts[1][0], parts[0][1], parts[1][1]], axis=0).astype(BF16)
    qstack = jnp.concatenate(
        [q_blk[:, (g * Q_CHUNKS_PER_GROUP + r) * LANES:(g * Q_CHUNKS_PER_GROUP + r + 1) * LANES]
         for r in range(Q_CHUNKS_PER_GROUP)], axis=0)
    return lax.dot_general(kstack, qstack, (((1,), (1,)), ((), ())), preferred_element_type=F32)


def _attn_block(q_blk, kv_cur, kv_prev, has_prev, sink_ref, pt_scr, o_ref, rows):
    blk = ATT_BLOCK
    key = lax.broadcasted_iota(jnp.int32, (blk, blk), 0)
    qry = lax.broadcasted_iota(jnp.int32, (blk, blk), 1)
    in_cur = key <= qry
    neg_inf = jnp.float32(-jnp.inf)
    st = [_attn_scores(g, q_blk, kv_cur[:, :ATT_KV_W], kv_prev[:, :ATT_KV_W]) for g in range(ATT_KV_HEADS)]
    yield 1
    v_cat = jnp.concatenate([kv_cur[:, ATT_KV_W:], kv_prev[:, ATT_KV_W:]], axis=0).astype(F32)
    vt_all = v_cat.T.astype(BF16)
    for g in range(ATT_KV_HEADS):
        for r in range(Q_CHUNKS_PER_GROUP):
            qs = slice(r * blk, (r + 1) * blk)
            for half in range(2):
                head = (g * Q_CHUNKS_PER_GROUP + r) * 2 + half
                s_cur = st[g][(2 * half) * blk:(2 * half + 1) * blk, qs]
                s_prev = st[g][(2 * half + 1) * blk:(2 * half + 2) * blk, qs]
                s = jnp.where(in_cur, s_cur, jnp.where(has_prev, s_prev, neg_inf))
                sink = sink_ref[head:head + 1, :]
                m = jnp.maximum(jnp.max(s, axis=0, keepdims=True), sink)
                p = jnp.exp(s - m)
                denom = jnp.sum(p, axis=0, keepdims=True) + jnp.exp(sink - m)
                p = p * (1.0 / denom)
                zp = jnp.zeros_like(p)
                pt_scr[g * 2 + half, :blk, qs] = jnp.where(in_cur, p, zp).astype(BF16)
                pt_scr[g * 2 + half, blk:, qs] = jnp.where(in_cur, zp, p).astype(BF16)
        vt = vt_all[g * ATT_HEAD_DIM:(g + 1) * ATT_HEAD_DIM, :]
        ot = [jnp.dot(vt, pt_scr[g * 2 + half], preferred_element_type=F32) for half in range(2)]
        for r in range(Q_CHUNKS_PER_GROUP):
            j = g * Q_CHUNKS_PER_GROUP + r
            qs = slice(r * blk, (r + 1) * blk)
            o_t = jnp.concatenate([ot[0][:, qs], ot[1][:, qs]], axis=0)
            o_ref[rows, j * LANES:(j + 1) * LANES] = o_t.T.astype(BF16)
    yield 0


def _inproj_kernel(blocks_per_seq, x_ref, g_ref, cos_ref, sin_ref, wq_ref, wmid_ref, wgate_ref, wif_ref, sink_ref,
                   att_ref, qkm_ref, vm_ref, om_ref, ga_ref, gm_ref, gi_ref, gf_ref, q_scr, kv_scr, pt_scr):
    tm = x_ref.shape[0]
    blk = ATT_BLOCK
    step = pl.program_id(0)
    slot = step % 2
    q_new, kv_new = q_scr.at[slot], kv_scr.at[slot]
    q_old, kv_old = q_scr.at[1 - slot], kv_scr.at[1 - slot]

    @pl.when(step == 0)
    def _():
        q_scr[...] = jnp.zeros_like(q_scr)
        kv_scr[...] = jnp.zeros_like(kv_scr)

    h = _rms(x_ref[...], g_ref[...]).astype(BF16)
    kv_new[:blk, :] = kv_old[tm:, :]

    def mixer():
        for k in range(tm // blk):
            rows = slice(k * blk, (k + 1) * blk)
            has_prev = ((step - 1) * (tm // blk) + k) % blocks_per_seq != 0
            yield from _attn_block(q_old[rows, :], kv_old[blk + k * blk:2 * blk + k * blk, :],
                                   kv_old[k * blk:(k + 1) * blk, :], has_prev, sink_ref, pt_scr, att_ref, rows)

    def dense():
        def proj(w_ref, lo, width):
            return jnp.dot(h, w_ref[:, lo:lo + width], preferred_element_type=F32)

        z = proj(wq_ref, 0, ATT_Q_W) * (ATT_HEAD_DIM ** -0.5)
        cos, sin = cos_ref[...], sin_ref[...]
        for j in range(ATT_Q_W // LANES):
            sl = slice(j * LANES, (j + 1) * LANES)
            q_new[:, sl] = _rope(z[:, sl], cos, sin).astype(BF16)
        yield
        z = proj(wmid_ref, _C_KA, 2 * ATT_KV_W)
        for j in range(ATT_KV_W // LANES):
            sl = slice(j * LANES, (j + 1) * LANES)
            kv_new[blk:, sl] = _rope(z[:, sl], cos, sin).astype(BF16)
        kv_new[blk:, ATT_KV_W:] = z[:, ATT_KV_W:].astype(BF16)
        vm_ref[...] = proj(wmid_ref, _C_VM, ML_V_W).astype(BF16)
        yield
        om_ref[...] = proj(wmid_ref, _C_OM, ML_V_W).astype(BF16)
        ga_ref[...] = proj(wgate_ref, 0, D_MODEL).astype(BF16)
        yield
        gm_ref[...] = proj(wgate_ref, D_MODEL, D_MODEL).astype(BF16)
        z = proj(wif_ref, 0, 2 * LANES)
        gi_ref[...] = z[:, :LANES]
        gf_ref[...] = z[:, LANES:]
        qkm_ref[...] = proj(wmid_ref, _C_QKM, 2 * ML_QK_W).astype(BF16)
        yield

    _interleave(mixer(), dense())


def _inproj(x1, g, cos, sin, wq, wmid, wgate, wif, sinks, tm, seq):
    t = x1.shape[0]
    n = t // tm
    tiles_per_seq = seq // tm

    def cur(i):
        return jnp.minimum(i, n - 1)

    def tok(width):
        return pl.BlockSpec((tm, width), lambda i: (cur(i), 0))

    pos = pl.BlockSpec((tm, LANES), lambda i: (cur(i) % tiles_per_seq, 0))
    lagged = pl.BlockSpec((tm, ATT_Q_W), lambda i: (jnp.maximum(i - 1, 0), 0))
    widths = (2 * ML_QK_W, ML_V_W, ML_V_W, D_MODEL, D_MODEL)
    out_shape = [jax.ShapeDtypeStruct((t, ATT_Q_W), BF16)]
    out_shape += [jax.ShapeDtypeStruct((t, w_), BF16) for w_ in widths]
    out_shape += [jax.ShapeDtypeStruct((t, LANES), F32)] * 2
    return pl.pallas_call(
        functools.partial(_inproj_kernel, seq // ATT_BLOCK),
        grid=(n + 1,),
        in_specs=[tok(D_MODEL), _resident((1, D_MODEL)), pos, pos, _resident((D_MODEL, ATT_Q_W)),
                  _resident((D_MODEL, W_MID)), _resident((D_MODEL, 2 * D_MODEL)), _resident((D_MODEL, 2 * LANES)),
                  _resident((ATT_HEADS, LANES))],
        out_specs=[lagged] + [tok(w_) for w_ in widths] + [tok(LANES)] * 2,
        out_shape=out_shape,
        scratch_shapes=[pltpu.VMEM((2, tm, ATT_Q_W), BF16),
                        pltpu.VMEM((2, ATT_BLOCK + tm, 2 * ATT_KV_W), BF16),
                        pltpu.VMEM((2 * ATT_KV_HEADS, 2 * ATT_BLOCK, Q_CHUNKS_PER_GROUP * ATT_BLOCK), BF16)],
        compiler_params=pltpu.CompilerParams(dimension_semantics=("arbitrary",),
                                             vmem_limit_bytes=VMEM_LIMIT_BYTES),
        name="in_proj_attn",
    )(x1, g, cos, sin, wq, wmid, wgate, wif, sinks)


def _scan_rows(x, op, fill):
    rows, lanes = x.shape
    n = rows // SUBLANES
    x3 = x.reshape(n, SUBLANES, lanes)
    sub = lax.broadcasted_iota(jnp.int32, x3.shape, 1)
    shift = 1
    while shift < SUBLANES:
        x3 = op(x3, jnp.where(sub >= shift, pltpu.roll(x3, shift, axis=1), fill))
        shift *= 2

    def shifted(t, k):
        return jnp.concatenate([jnp.full((k, 1, lanes), fill, x.dtype), t[:-k]], axis=0)

    before = shifted(x3[:, SUBLANES - 1:, :], 1)
    shift = 1
    while shift < n:
        before = op(before, shifted(before, shift))
        shift *= 2
    return op(x3, before).reshape(rows, lanes)


def _mlstm_chunk_inputs(rows, qk_ref, gi_ref, gf_ref, cw_ref, cb_ref, bi_ref, bf_ref, conv_scr):
    L = MLSTM_CHUNK
    tail = CONV_TAIL_ROWS
    x = qk_ref[rows, :].astype(F32)
    conv_scr[tail:, :] = x
    y = cb_ref[...] + cw_ref[MLSTM_CONV - 1:MLSTM_CONV, :] * x
    for j in range(1, MLSTM_CONV):
        y = y + cw_ref[MLSTM_CONV - 1 - j:MLSTM_CONV - j, :] * conv_scr[tail - j:tail - j + L, :]
    conv_scr[:tail, :] = x[L - tail:, :]
    qk = y * _sigmoid(y)

    ig = gi_ref[rows, :] + bi_ref[...]
    fpre = (gf_ref[rows, :] + bf_ref[...]).T[:SUBLANES, :]
    logf = jnp.minimum(fpre, 0.0) - jnp.log1p(jnp.exp(-jnp.abs(fpre)))
    logf = jnp.concatenate([logf, jnp.zeros((LANES - SUBLANES, L), F32)], axis=0).T
    b = _scan_rows(logf, jnp.add, 0.0)
    g = ig - b
    cm = _scan_rows(g, jnp.maximum, -jnp.inf)
    return qk, b, g, cm


def _mlstm_gates(b, g, cm, m_scr):
    L = MLSTM_CHUNK
    m_prev = m_scr[0:1, :]
    u = jnp.maximum(cm, m_prev)
    g_max = cm[L - 1:L, :]
    u_last = u[L - 1:L, :]
    gates = dict(u=u, a=jnp.exp(g - g_max), inter=jnp.exp(m_prev - u), emt=jnp.exp(-(b + u)),
                 s_old=jnp.exp(m_prev - u_last), s_new=jnp.exp(g_max - u_last),
                 g_rows=g.T)
    m_scr[0:1, :] = b[L - 1:L, :] + u_last
    return gates


def _mlstm_head(h, rows, qk, gates, v_ref, o_ref, hn_ref, out_ref, ct_scr, n_scr):
    L = MLSTM_CHUNK
    row = lax.broadcasted_iota(jnp.int32, (L, L), 0)
    col = lax.broadcasted_iota(jnp.int32, (L, L), 1)
    causal = col <= row
    qs = slice(h * MLSTM_QK_DIM, (h + 1) * MLSTM_QK_DIM)
    ks = slice(ML_QK_W + h * MLSTM_QK_DIM, ML_QK_W + (h + 1) * MLSTM_QK_DIM)
    vs = slice(h * MLSTM_V_DIM, (h + 1) * MLSTM_V_DIM)
    qh = qk[:, qs]
    kh = qk[:, ks] * (MLSTM_QK_DIM ** -0.5)
    qb = qh.astype(BF16)
    kb = kh.astype(BF16)
    vb = v_ref[rows, vs]
    inter_col = gates["inter"][:, h:h + 1]
    a_col = gates["a"][:, h:h + 1]

    decay = jnp.where(causal, jnp.exp(gates["g_rows"][h:h + 1, :] - gates["u"][:, h:h + 1]), 0.0)
    s = lax.dot_general(qb, kb, (((1,), (1,)), ((), ())), preferred_element_type=F32) * decay
    ct = ct_scr[h]
    n_row = n_scr[h:h + 1, :]
    yield 1
    num = (jnp.dot(s.astype(BF16), vb, preferred_element_type=F32)
           + inter_col * jnp.dot(qb, ct.astype(BF16), preferred_element_type=F32))
    den = jnp.sum(s + inter_col * (qh * n_row), axis=-1, keepdims=True)
    hh = num * (1.0 / jnp.maximum(jnp.abs(den), gates["emt"][:, h:h + 1]))
    hn = _rms(hh, hn_ref[:, vs])
    out_ref[rows, vs] = (_sigmoid(o_ref[rows, vs].astype(F32)) * hn).astype(BF16)
    yield 1
    av = (a_col * vb.astype(F32)).astype(BF16)
    d_ct = lax.dot_general(kb, av, (((0,), (0,)), ((), ())), preferred_element_type=F32)
    d_n = jnp.sum(a_col * kh, axis=0, keepdims=True)
    so = gates["s_old"][:, h:h + 1]
    sn = gates["s_new"][:, h:h + 1]
    ct_scr[h] = so * ct + sn * d_ct
    n_scr[h:h + 1, :] = so * n_row + sn * d_n
    yield 1


def _out_kernel(tiles_per_seq, n_tiles, att_ref, ga_ref, gm_ref, x1_ref, qk_ref, v_ref, o_ref, gi_ref, gf_ref,
                cw_ref, cb_ref, bi_ref, bf_ref, hn_ref, watt_ref, wml_ref, wout_ref, g2_ref, wg_ref, wu_ref,
                wd_ref, gfin_ref, out_ref, hm_scr, ct_scr, n_scr, m_scr, conv_scr):
    tm = x1_ref.shape[0]
    L = MLSTM_CHUNK
    step = pl.program_id(0)
    slot = step % 2
    hm_new, hm_old = hm_scr.at[slot], hm_scr.at[1 - slot]

    @pl.when(step == 0)
    def _():
        hm_scr[...] = jnp.zeros_like(hm_scr)

    @pl.when(jnp.minimum(step, n_tiles - 1) % tiles_per_seq == 0)
    def _():
        ct_scr[...] = jnp.zeros_like(ct_scr)
        n_scr[...] = jnp.zeros_like(n_scr)
        m_scr[...] = jnp.zeros_like(m_scr)
        conv_scr[:CONV_TAIL_ROWS, :] = jnp.zeros((CONV_TAIL_ROWS, conv_scr.shape[1]), F32)

    def chunk_inputs(c):
        return _mlstm_chunk_inputs(slice(c * L, (c + 1) * L), qk_ref, gi_ref, gf_ref, cw_ref, cb_ref, bi_ref,
                                   bf_ref, conv_scr)

    def mixer():
        for c in range(tm // L):
            rows = slice(c * L, (c + 1) * L)
            qk, b, g, cm = chunk_inputs(c)
            gates = _mlstm_gates(b, g, cm, m_scr)
            yield 1
            for h in range(MLSTM_HEADS):
                yield from _mlstm_head(h, rows, qk, gates, v_ref, o_ref, hn_ref, hm_new, ct_scr, n_scr)

    def dense():
        def dot(x, w):
            return jnp.dot(x, w, preferred_element_type=F32)

        cols = _col_groups(0, D_MODEL)
        ya, ym, x2p = [], [], []
        for lo, hi in cols:
            ya.append(dot(att_ref[...], watt_ref[:, lo:hi]))
            yield
        for lo, hi in cols:
            ym.append(dot(hm_old[...], wml_ref[:, lo:hi]))
            yield
        y = jnp.concatenate(
            [_sigmoid(ga_ref[:, lo:hi].astype(F32)) * ya[j] + _sigmoid(gm_ref[:, lo:hi].astype(F32)) * ym[j]
             for j, (lo, hi) in enumerate(cols)], axis=1).astype(BF16)
        for lo, hi in cols:
            x2p.append(dot(y, wout_ref[:, lo:hi]))
            yield
        x2 = x1_ref[...] + jnp.concatenate(x2p, axis=1)
        h2 = _rms(x2, g2_ref[...]).astype(BF16)
        acc = None
        for clo, chi in FF_CHUNKS:
            groups = _col_groups(clo, chi)
            g = []
            for lo, hi in groups:
                g.append(dot(h2, wg_ref[:, lo:hi]))
                yield
            a = []
            for j, (lo, hi) in enumerate(groups):
                u = dot(h2, wu_ref[:, lo:hi])
                a.append((g[j] * _sigmoid(g[j]) * u).astype(BF16))
                yield
            a = jnp.concatenate(a, axis=1)
            part = []
            for lo, hi in cols:
                part.append(dot(a, wd_ref[clo:chi, lo:hi]))
                yield
            acc = part if acc is None else [p + q for p, q in zip(acc, part)]
        x3 = x2 + 0.5 * jnp.concatenate(acc, axis=1)
        out_ref[...] = _rms(x3, gfin_ref[...])
        yield

    _interleave(mixer(), dense())


def _out(att, ga, gm, x1, qkm, vm, om, gi, gf, cw, cb, bi, bf, hn, watt, wml, wout, g2, wg, wu, wd, gfin, tm, seq):
    t = x1.shape[0]
    n = t // tm

    def cur(width):
        return pl.BlockSpec((tm, width), lambda i: (jnp.minimum(i, n - 1), 0))

    lagged = pl.BlockSpec((tm, D_MODEL), lambda i: (jnp.maximum(i - 1, 0), 0))
    sq = _resident((D_MODEL, D_MODEL))
    vec = _resident((1, D_MODEL))
    return pl.pallas_call(
        functools.partial(_out_kernel, seq // tm, n),
        grid=(n + 1,),
        in_specs=[lagged, lagged, lagged, lagged,
                  cur(2 * ML_QK_W), cur(ML_V_W), cur(ML_V_W), cur(LANES), cur(LANES),
                  _resident((MLSTM_CONV, 2 * ML_QK_W)), _resident((1, 2 * ML_QK_W)),
                  _resident((1, LANES)), _resident((1, LANES)), vec,
                  sq, sq, sq, vec, _resident((D_MODEL, D_FF)), _resident((D_MODEL, D_FF)),
                  _resident((D_FF, D_MODEL)), vec],
        out_specs=lagged,
        out_shape=jax.ShapeDtypeStruct((t, D_MODEL), F32),
        scratch_shapes=[pltpu.VMEM((2, tm, ML_V_W), BF16),
                        pltpu.VMEM((MLSTM_HEADS, MLSTM_QK_DIM, MLSTM_V_DIM), F32),
                        pltpu.VMEM((SUBLANES, LANES), F32),
                        pltpu.VMEM((SUBLANES, LANES), F32),
                        pltpu.VMEM((CONV_TAIL_ROWS + MLSTM_CHUNK, 2 * ML_QK_W), F32)],
        compiler_params=pltpu.CompilerParams(dimension_semantics=("arbitrary",),
                                             vmem_limit_bytes=VMEM_LIMIT_BYTES),
        name="mlstm_merge_ffn2",
    )(att, ga, gm, x1, qkm, vm, om, gi, gf, cw, cb, bi, bf, hn, watt, wml, wout, g2, wg, wu, wd, gfin)


def _rope_tables(seq):
    half = ATT_HEAD_DIM // 2
    pos = jnp.arange(seq, dtype=F32)
    inv_freq = ROPE_THETA ** (-jnp.arange(half, dtype=F32) / half)
    ang = pos[:, None] * inv_freq[None, :]
    lane = np.arange(LANES)
    sign = jnp.asarray(np.where(lane % ATT_HEAD_DIM < half, -1.0, 1.0), F32)
    return jnp.cos(ang)[:, lane % half], jnp.sin(ang)[:, lane % half] * sign


def _pad_lanes(v):
    return jnp.pad(v, ((0, 0), (0, LANES - v.shape[-1])))


def _layer(x2d, batch, seq, p):
    (ffn1_norm, ffn1_w_gate, ffn1_w_up, ffn1_w_down, mix_norm, w_in, b_i, b_f, attn_sinks, conv_w, conv_b,
     head_norm, w_att, w_mlstm, w_out, ffn2_norm, ffn2_w_gate, ffn2_w_up, ffn2_w_down, final_norm) = p
    tm = min(TOKEN_TILE, seq)
    assert seq % tm == 0 and seq % ATT_BLOCK == 0 and seq % MLSTM_CHUNK == 0

    cos, sin = _rope_tables(seq)
    sinks = jnp.broadcast_to(attn_sinks[:, None], (ATT_HEADS, LANES)).astype(F32)

    x1, (w_q, w_mid, w_gate, w_if), (watt, wml, wout, wg2, wu2, wd2) = _ffn1(
        x2d, ffn1_norm[None], ffn1_w_gate.astype(BF16), ffn1_w_up.astype(BF16), ffn1_w_down.astype(BF16),
        w_in, (w_att, w_mlstm, w_out, ffn2_w_gate, ffn2_w_up, ffn2_w_down),
        FFN1_TOKEN_TILE if (batch * seq) % FFN1_TOKEN_TILE == 0 else tm)
    att, qkm, vm, om, ga, gm, gi, gf = _inproj(
        x1, mix_norm[None], cos, sin, w_q, w_mid, w_gate, w_if, sinks, tm, seq)
    return _out(att, ga, gm, x1, qkm, vm, om, gi, gf, conv_w, conv_b[None], _pad_lanes(b_i[None]),
                _pad_lanes(b_f[None]), head_norm[None], watt, wml, wout, ffn2_norm[None], wg2, wu2, wd2,
                final_norm[None], tm, seq)


def kernel(x, ffn1_norm, ffn1_w_gate, ffn1_w_up, ffn1_w_down, mix_norm, w_in, b_i, b_f, attn_sinks, conv_w,
           conv_b, head_norm, w_att, w_mlstm, w_out, ffn2_norm, ffn2_w_gate, ffn2_w_up, ffn2_w_down, final_norm):
    batch, seq, d = x.shape
    assert d == D_MODEL and ffn1_norm.shape[0] == 1, "single-layer kernel"
    per_layer = (ffn1_norm, ffn1_w_gate, ffn1_w_up, ffn1_w_down, mix_norm, w_in, b_i, b_f, attn_sinks, conv_w,
                 conv_b, head_norm, w_att, w_mlstm, w_out, ffn2_norm, ffn2_w_gate, ffn2_w_up, ffn2_w_down)
    params = tuple(a[0] for a in per_layer) + (final_norm,)
    out = _layer(x.reshape(batch * seq, d), batch, seq, params)
    return out.reshape(batch, seq, d)
```

```python
import functools

import jax
import jax.numpy as jnp
import numpy as np
from jax import lax
from jax.experimental import pallas as pl
from jax.experimental.pallas import tpu as pltpu

F32 = jnp.float32
BF16 = jnp.bfloat16

D_MODEL = 1024
ATT_HEADS = 16
ATT_KV_HEADS = 4
ATT_HEAD_DIM = 64
ATT_BLOCK = 128
ROPE_THETA = 10000.0
MLSTM_HEADS = 4
MLSTM_V_DIM = D_MODEL // MLSTM_HEADS
MLSTM_QK_DIM = MLSTM_V_DIM // 2
MLSTM_CHUNK = 128
MLSTM_CONV = 4
D_FF = 2816
RMS_EPS = 1e-5
NEG_LOG2E = -1.4426950408889634

ATT_Q_W = ATT_HEADS * ATT_HEAD_DIM
ATT_KV_W = ATT_KV_HEADS * ATT_HEAD_DIM
ML_QK_W = MLSTM_HEADS * MLSTM_QK_DIM
ML_V_W = MLSTM_HEADS * MLSTM_V_DIM

LANES = 128
MXU_COLS = 256
SUBLANES = 8
BF16_SUBLANES = 16
CONV_TAIL_ROWS = SUBLANES
VMEM_LIMIT_BYTES = 60 * 1024 * 1024

TOKEN_TILE = 512
FFN1_TOKEN_TILE = 1024
FF_CHUNKS = ((0, 768), (768, 1536), (1536, 2304), (2304, D_FF))

Q_CHUNKS_PER_GROUP = ATT_HEADS // ATT_KV_HEADS * ATT_HEAD_DIM // LANES

_C_KA = 0
_C_VA = _C_KA + ATT_KV_W
_C_QKM = _C_VA + ATT_KV_W
_C_VM = _C_QKM + 2 * ML_QK_W
_C_OM = _C_VM + ML_V_W
W_MID = _C_OM + ML_V_W

_IN_OFFS = tuple(int(v) for v in np.cumsum((0, ATT_Q_W, ATT_KV_W, ATT_KV_W, ML_QK_W, ML_QK_W, ML_V_W, ML_V_W,
                                            MLSTM_HEADS, MLSTM_HEADS, D_MODEL, D_MODEL)))
IN_PROJ_W = _IN_OFFS[-1]


def _resident(shape):
    return pl.BlockSpec(shape, lambda *_: (0,) * len(shape), pipeline_mode=pl.Buffered(1))


def _col_groups(lo, hi):
    return [(c, min(c + MXU_COLS, hi)) for c in range(lo, hi, MXU_COLS)]


def _interleave(mixer, dense):
    for n_dense in mixer:
        for _ in range(n_dense):
            next(dense, None)
    for _ in dense:
        pass


def _rms(x, g):
    ms = jnp.mean(x * x, axis=-1, keepdims=True)
    return x * lax.rsqrt(ms + RMS_EPS) * g


def _sigmoid(x):
    return 1.0 / (1.0 + jnp.exp2(x * NEG_LOG2E))


def _swiglu(h, wg_ref, wu_ref, wd_ref):
    acc = None
    for lo, hi in FF_CHUNKS:
        g = jnp.dot(h, wg_ref[:, lo:hi], preferred_element_type=F32)
        u = jnp.dot(h, wu_ref[:, lo:hi], preferred_element_type=F32)
        a = (g * _sigmoid(g) * u).astype(BF16)
        d = jnp.dot(a, wd_ref[lo:hi, :], preferred_element_type=F32)
        acc = d if acc is None else acc + d
    return acc


def _ffn1_kernel(x_ref, g_ref, wg_ref, wu_ref, wd_ref, win_ref, *refs):
    n_plain = (len(refs) - 5) // 2
    plain_in, (o_ref, wq_ref, wmid_ref, wgate_ref, wif_ref), plain_out = (
        refs[:n_plain], refs[n_plain:n_plain + 5], refs[n_plain + 5:])
    for src, dst in zip(plain_in, plain_out):
        dst[...] = src[...].astype(BF16)
    w = win_ref[...]
    wq_ref[...] = w[:, _IN_OFFS[0]:_IN_OFFS[1]].astype(BF16)
    wmid_ref[...] = w[:, _IN_OFFS[1]:_IN_OFFS[7]].astype(BF16)
    wgate_ref[...] = w[:, _IN_OFFS[9]:].astype(BF16)
    pad = jnp.zeros((w.shape[0], LANES - MLSTM_HEADS), F32)
    wif_ref[...] = jnp.concatenate([w[:, _IN_OFFS[7]:_IN_OFFS[8]], pad, w[:, _IN_OFFS[8]:_IN_OFFS[9]], pad],
                                   axis=1).astype(BF16)

    x = x_ref[...]
    h = _rms(x, g_ref[...]).astype(BF16)
    o_ref[...] = x + 0.5 * _swiglu(h, wg_ref, wu_ref, wd_ref)


def _row_block(rows, n_steps):
    return next(r for r in range(BF16_SUBLANES, rows + 1, BF16_SUBLANES)
                if rows % r == 0 and r * n_steps >= rows)


def _ffn1(x, g, wg, wu, wd, w_in, later_weights, tm):
    t = x.shape[0]
    assert t % tm == 0
    n = t // tm
    tok = pl.BlockSpec((tm, D_MODEL), lambda i: (i, 0))

    def streamed(w):
        r = _row_block(w.shape[0], n)
        last = w.shape[0] // r - 1
        return lambda width: pl.BlockSpec((r, width), lambda i: (jnp.minimum(i, last), 0))

    in_spec = streamed(w_in)
    plain = [streamed(w) for w in later_weights]
    in_widths = (ATT_Q_W, W_MID, 2 * D_MODEL, 2 * LANES)
    out = pl.pallas_call(
        _ffn1_kernel,
        grid=(n,),
        in_specs=[tok, _resident((1, D_MODEL)), _resident((D_MODEL, D_FF)), _resident((D_MODEL, D_FF)),
                  _resident((D_FF, D_MODEL)), in_spec(IN_PROJ_W)]
        + [spec(w.shape[1]) for spec, w in zip(plain, later_weights)],
        out_specs=[tok] + [in_spec(w_) for w_ in in_widths]
        + [spec(w.shape[1]) for spec, w in zip(plain, later_weights)],
        out_shape=[jax.ShapeDtypeStruct((t, D_MODEL), F32)]
        + [jax.ShapeDtypeStruct((D_MODEL, w_), BF16) for w_ in in_widths]
        + [jax.ShapeDtypeStruct(w.shape, BF16) for w in later_weights],
        compiler_params=pltpu.CompilerParams(dimension_semantics=("arbitrary",),
                                             vmem_limit_bytes=VMEM_LIMIT_BYTES),
        name="ffn1",
    )(x, g, wg, wu, wd, w_in, *later_weights)
    return out[0], out[1:5], out[5:]


def _rope(x, cos, sin_signed):
    lane = lax.broadcasted_iota(jnp.int32, x.shape, 1)
    half = ATT_HEAD_DIM // 2
    partner = jnp.where(lane % ATT_HEAD_DIM < half, pltpu.roll(x, LANES - half, axis=1),
                        pltpu.roll(x, half, axis=1))
    return x * cos + partner * sin_signed


def _attn_scores(g, q_blk, k_cur, k_prev):
    low = lax.broadcasted_iota(jnp.int32, (ATT_BLOCK, LANES), 1) < ATT_HEAD_DIM
    c, in_high = divmod(g, 2)
    ksl = slice(c * LANES, (c + 1) * LANES)
    parts = []
    for k in (k_cur, k_prev):
        x = k[:, ksl].astype(F32)
        other = pltpu.roll(x, ATT_HEAD_DIM, axis=1)
        in_low_lanes, in_high_lanes = (other, x) if in_high else (x, other)
        parts.append((jnp.where(low, in_low_lanes, 0.0), jnp.where(low, 0.0, in_high_lanes)))
    kstack = jnp.concatenate([parts[0][0], parts[1][0], parts[0][1], parts[1][1]], axis=0).astype(BF16)
    qstack = jnp.concatenate(
        [q_blk[:, (g * Q_CHUNKS_PER_GROUP + r) * LANES:(g * Q_CHUNKS_PER_GROUP + r + 1) * LANES]
         for r in range(Q_CHUNKS_PER_GROUP)], axis=0)
    return lax.dot_general(kstack, qstack, (((1,), (1,)), ((), ())), preferred_element_type=F32)


def _attn_block(q_blk, kv_cur, kv_prev, has_prev, sink_ref, pt_scr, o_ref, rows):
    blk = ATT_BLOCK
    key = lax.broadcasted_iota(jnp.int32, (blk, blk), 0)
    qry = lax.broadcasted_iota(jnp.int32, (blk, blk), 1)
    in_cur = key <= qry
    neg_inf = jnp.float32(-jnp.inf)
    st = [_attn_scores(g, q_blk, kv_cur[:, :ATT_KV_W], kv_prev[:, :ATT_KV_W]) for g in range(ATT_KV_HEADS)]
    yield 1
    v_cat = jnp.concatenate([kv_cur[:, ATT_KV_W:], kv_prev[:, ATT_KV_W:]], axis=0).astype(F32)
    vt_all = v_cat.T.astype(BF16)
    for g in range(ATT_KV_HEADS):
        for r in range(Q_CHUNKS_PER_GROUP):
            qs = slice(r * blk, (r + 1) * blk)
            for half in range(2):
                head = (g * Q_CHUNKS_PER_GROUP + r) * 2 + half
                s_cur = st[g][(2 * half) * blk:(2 * half + 1) * blk, qs]
                s_prev = st[g][(2 * half + 1) * blk:(2 * half + 2) * blk, qs]
                s = jnp.where(in_cur, s_cur, jnp.where(has_prev, s_prev, neg_inf))
                sink = sink_ref[head:head + 1, :]
                m = jnp.maximum(jnp.max(s, axis=0, keepdims=True), sink)
                p = jnp.exp(s - m)
                denom = jnp.sum(p, axis=0, keepdims=True) + jnp.exp(sink - m)
                p = p * (1.0 / denom)
                zp = jnp.zeros_like(p)
                pt_scr[g * 2 + half, :blk, qs] = jnp.where(in_cur, p, zp).astype(BF16)
                pt_scr[g * 2 + half, blk:, qs] = jnp.where(in_cur, zp, p).astype(BF16)
        vt = vt_all[g * ATT_HEAD_DIM:(g + 1) * ATT_HEAD_DIM, :]
        ot = [jnp.dot(vt, pt_scr[g * 2 + half], preferred_element_type=F32) for half in range(2)]
        for r in range(Q_CHUNKS_PER_GROUP):
            j = g * Q_CHUNKS_PER_GROUP + r
            qs = slice(r * blk, (r + 1) * blk)
            o_t = jnp.concatenate([ot[0][:, qs], ot[1][:, qs]], axis=0)
            o_ref[rows, j * LANES:(j + 1) * LANES] = o_t.T.astype(BF16)
    yield 0


def _inproj_kernel(blocks_per_seq, x_ref, g_ref, cos_ref, sin_ref, wq_ref, wmid_ref, wgate_ref, wif_ref, sink_ref,
                   att_ref, qkm_ref, vm_ref, om_ref, ga_ref, gm_ref, gi_ref, gf_ref, q_scr, kv_scr, pt_scr):
    tm = x_ref.shape[0]
    blk = ATT_BLOCK
    step = pl.program_id(0)
    slot = step % 2
    q_new, kv_new = q_scr.at[slot], kv_scr.at[slot]
    q_old, kv_old = q_scr.at[1 - slot], kv_scr.at[1 - slot]

    @pl.when(step == 0)
    def _():
        q_scr[...] = jnp.zeros_like(q_scr)
        kv_scr[...] = jnp.zeros_like(kv_scr)

    h = _rms(x_ref[...], g_ref[...]).astype(BF16)
    kv_new[:blk, :] = kv_old[tm:, :]

    def mixer():
        for k in range(tm // blk):
            rows = slice(k * blk, (k + 1) * blk)
            has_prev = ((step - 1) * (tm // blk) + k) % blocks_per_seq != 0
            yield from _attn_block(q_old[rows, :], kv_old[blk + k * blk:2 * blk + k * blk, :],
                                   kv_old[k * blk:(k + 1) * blk, :], has_prev, sink_ref, pt_scr, att_ref, rows)

    def dense():
        def proj(w_ref, lo, width):
            return jnp.dot(h, w_ref[:, lo:lo + width], preferred_element_type=F32)

        z = proj(wq_ref, 0, ATT_Q_W) * (ATT_HEAD_DIM ** -0.5)
        cos, sin = cos_ref[...], sin_ref[...]
        for j in range(ATT_Q_W // LANES):
            sl = slice(j * LANES, (j + 1) * LANES)
            q_new[:, sl] = _rope(z[:, sl], cos, sin).astype(BF16)
        yield
        z = proj(wmid_ref, _C_KA, 2 * ATT_KV_W)
        for j in range(ATT_KV_W // LANES):
            sl = slice(j * LANES, (j + 1) * LANES)
            kv_new[blk:, sl] = _rope(z[:, sl], cos, sin).astype(BF16)
        kv_new[blk:, ATT_KV_W:] = z[:, ATT_KV_W:].astype(BF16)
        vm_ref[...] = proj(wmid_ref, _C_VM, ML_V_W).astype(BF16)
        yield
        om_ref[...] = proj(wmid_ref, _C_OM, ML_V_W).astype(BF16)
        ga_ref[...] = proj(wgate_ref, 0, D_MODEL).astype(BF16)
        yield
        gm_ref[...] = proj(wgate_ref, D_MODEL, D_MODEL).astype(BF16)
        z = proj(wif_ref, 0, 2 * LANES)
        gi_ref[...] = z[:, :LANES]
        gf_ref[...] = z[:, LANES:]
        qkm_ref[...] = proj(wmid_ref, _C_QKM, 2 * ML_QK_W).astype(BF16)
        yield

    _interleave(mixer(), dense())


def _inproj(x1, g, cos, sin, wq, wmid, wgate, wif, sinks, tm, seq):
    t = x1.shape[0]
    n = t // tm
    tiles_per_seq = seq // tm

    def cur(i):
        return jnp.minimum(i, n - 1)

    def tok(width):
        return pl.BlockSpec((tm, width), lambda i: (cur(i), 0))

    pos = pl.BlockSpec((tm, LANES), lambda i: (cur(i) % tiles_per_seq, 0))
    lagged = pl.BlockSpec((tm, ATT_Q_W), lambda i: (jnp.maximum(i - 1, 0), 0))
    widths = (2 * ML_QK_W, ML_V_W, ML_V_W, D_MODEL, D_MODEL)
    out_shape = [jax.ShapeDtypeStruct((t, ATT_Q_W), BF16)]
    out_shape += [jax.ShapeDtypeStruct((t, w_), BF16) for w_ in widths]
    out_shape += [jax.ShapeDtypeStruct((t, LANES), F32)] * 2
    return pl.pallas_call(
        functools.partial(_inproj_kernel, seq // ATT_BLOCK),
        grid=(n + 1,),
        in_specs=[tok(D_MODEL), _resident((1, D_MODEL)), pos, pos, _resident((D_MODEL, ATT_Q_W)),
                  _resident((D_MODEL, W_MID)), _resident((D_MODEL, 2 * D_MODEL)), _resident((D_MODEL, 2 * LANES)),
                  _resident((ATT_HEADS, LANES))],
        out_specs=[lagged] + [tok(w_) for w_ in widths] + [tok(LANES)] * 2,
        out_shape=out_shape,
        scratch_shapes=[pltpu.VMEM((2, tm, ATT_Q_W), BF16),
                        pltpu.VMEM((2, ATT_BLOCK + tm, 2 * ATT_KV_W), BF16),
                        pltpu.VMEM((2 * ATT_KV_HEADS, 2 * ATT_BLOCK, Q_CHUNKS_PER_GROUP * ATT_BLOCK), BF16)],
        compiler_params=pltpu.CompilerParams(dimension_semantics=("arbitrary",),
                                             vmem_limit_bytes=VMEM_LIMIT_BYTES),
        name="in_proj_attn",
    )(x1, g, cos, sin, wq, wmid, wgate, wif, sinks)


def _scan_rows(x, op, fill):
    n = x.shape[0]
    row = lax.broadcasted_iota(jnp.int32, x.shape, 0)
    shift = 1
    while shift < n:
        prev = jnp.where(row >= shift, pltpu.roll(x, shift, axis=0), fill)
        x = op(x, prev)
        shift *= 2
    return x


def _mlstm_chunk_inputs(rows, qk_ref, gi_ref, gf_ref, cw_ref, cb_ref, bi_ref, bf_ref, conv_scr):
    L = MLSTM_CHUNK
    tail = CONV_TAIL_ROWS
    x = qk_ref[rows, :].astype(F32)
    conv_scr[tail:, :] = x
    y = cb_ref[...] + cw_ref[MLSTM_CONV - 1:MLSTM_CONV, :] * x
    for j in range(1, MLSTM_CONV):
        y = y + cw_ref[MLSTM_CONV - 1 - j:MLSTM_CONV - j, :] * conv_scr[tail - j:tail - j + L, :]
    conv_scr[:tail, :] = x[L - tail:, :]
    qk = y * _sigmoid(y)

    ig = gi_ref[rows, :] + bi_ref[...]
    fpre = (gf_ref[rows, :] + bf_ref[...]).T[:SUBLANES, :]
    logf = jnp.minimum(fpre, 0.0) - jnp.log1p(jnp.exp(-jnp.abs(fpre)))
    logf = jnp.concatenate([logf, jnp.zeros((LANES - SUBLANES, L), F32)], axis=0).T
    b = _scan_rows(logf, jnp.add, 0.0)
    g = ig - b
    cm = _scan_rows(g, jnp.maximum, -jnp.inf)
    return qk, b, g, cm


def _mlstm_gates(b, g, cm, m_scr):
    L = MLSTM_CHUNK
    m_prev = m_scr[0:1, :]
    u = jnp.maximum(cm, m_prev)
    g_max = cm[L - 1:L, :]
    u_last = u[L - 1:L, :]
    gates = dict(u=u, a=jnp.exp(g - g_max), inter=jnp.exp(m_prev - u), emt=jnp.exp(-(b + u)),
                 s_old=jnp.exp(m_prev - u_last), s_new=jnp.exp(g_max - u_last),
                 g_rows=g.T)
    m_scr[0:1, :] = b[L - 1:L, :] + u_last
    return gates


def _mlstm_head(h, rows, qk, gates, v_ref, o_ref, hn_ref, out_ref, ct_scr, n_scr):
    L = MLSTM_CHUNK
    row = lax.broadcasted_iota(jnp.int32, (L, L), 0)
    col = lax.broadcasted_iota(jnp.int32, (L, L), 1)
    causal = col <= row
    qs = slice(h * MLSTM_QK_DIM, (h + 1) * MLSTM_QK_DIM)
    ks = slice(ML_QK_W + h * MLSTM_QK_DIM, ML_QK_W + (h + 1) * MLSTM_QK_DIM)
    vs = slice(h * MLSTM_V_DIM, (h + 1) * MLSTM_V_DIM)
    qh = qk[:, qs]
    kh = qk[:, ks] * (MLSTM_QK_DIM ** -0.5)
    qb = qh.astype(BF16)
    kb = kh.astype(BF16)
    vb = v_ref[rows, vs]
    inter_col = gates["inter"][:, h:h + 1]
    a_col = gates["a"][:, h:h + 1]

    decay = jnp.where(causal, jnp.exp(gates["g_rows"][h:h + 1, :] - gates["u"][:, h:h + 1]), 0.0)
    s = lax.dot_general(qb, kb, (((1,), (1,)), ((), ())), preferred_element_type=F32) * decay
    ct = ct_scr[h]
    n_row = n_scr[h:h + 1, :]
    yield 1
    num = (jnp.dot(s.astype(BF16), vb, preferred_element_type=F32)
           + inter_col * jnp.dot(qb, ct.astype(BF16), preferred_element_type=F32))
    den = jnp.sum(s + inter_col * (qh * n_row), axis=-1, keepdims=True)
    d = jnp.maximum(jnp.abs(den), gates["emt"][:, h:h + 1])
    ms = jnp.mean(num * num, axis=-1, keepdims=True)
    hn = num * lax.rsqrt(ms + RMS_EPS * (d * d)) * hn_ref[:, vs]
    out_ref[rows, vs] = (_sigmoid(o_ref[rows, vs].astype(F32)) * hn).astype(BF16)
    yield 1
    av = (a_col * vb.astype(F32)).astype(BF16)
    d_ct = lax.dot_general(kb, av, (((0,), (0,)), ((), ())), preferred_element_type=F32)
    d_n = jnp.sum(a_col * kh, axis=0, keepdims=True)
    so = gates["s_old"][:, h:h + 1]
    sn = gates["s_new"][:, h:h + 1]
    ct_scr[h] = so * ct + sn * d_ct
    n_scr[h:h + 1, :] = so * n_row + sn * d_n
    yield 1


def _out_kernel(tiles_per_seq, n_tiles, att_ref, ga_ref, gm_ref, x1_ref, qk_ref, v_ref, o_ref, gi_ref, gf_ref,
                cw_ref, cb_ref, bi_ref, bf_ref, hn_ref, watt_ref, wml_ref, wout_ref, g2_ref, wg_ref, wu_ref,
                wd_ref, gfin_ref, out_ref, hm_scr, ct_scr, n_scr, m_scr, conv_scr):
    tm = x1_ref.shape[0]
    L = MLSTM_CHUNK
    step = pl.program_id(0)
    slot = step % 2
    hm_new, hm_old = hm_scr.at[slot], hm_scr.at[1 - slot]

    @pl.when(step == 0)
    def _():
        hm_scr[...] = jnp.zeros_like(hm_scr)

    @pl.when(jnp.minimum(step, n_tiles - 1) % tiles_per_seq == 0)
    def _():
        ct_scr[...] = jnp.zeros_like(ct_scr)
        n_scr[...] = jnp.zeros_like(n_scr)
        m_scr[...] = jnp.zeros_like(m_scr)
        conv_scr[:CONV_TAIL_ROWS, :] = jnp.zeros((CONV_TAIL_ROWS, conv_scr.shape[1]), F32)

    def chunk_inputs(c):
        return _mlstm_chunk_inputs(slice(c * L, (c + 1) * L), qk_ref, gi_ref, gf_ref, cw_ref, cb_ref, bi_ref,
                                   bf_ref, conv_scr)

    def mixer():
        for c in range(tm // L):
            rows = slice(c * L, (c + 1) * L)
            qk, b, g, cm = chunk_inputs(c)
            gates = _mlstm_gates(b, g, cm, m_scr)
            yield 1
            for h in range(MLSTM_HEADS):
                yield from _mlstm_head(h, rows, qk, gates, v_ref, o_ref, hn_ref, hm_new, ct_scr, n_scr)

    def dense():
        def dot(x, w):
            return jnp.dot(x, w, preferred_element_type=F32)

        cols = _col_groups(0, D_MODEL)
        ya, ym, x2p = [], [], []
        for lo, hi in cols:
            ya.append(dot(att_ref[...], watt_ref[:, lo:hi]))
            yield
        for lo, hi in cols:
            ym.append(dot(hm_old[...], wml_ref[:, lo:hi]))
            yield
        y = jnp.concatenate(
            [_sigmoid(ga_ref[:, lo:hi].astype(F32)) * ya[j] + _sigmoid(gm_ref[:, lo:hi].astype(F32)) * ym[j]
             for j, (lo, hi) in enumerate(cols)], axis=1).astype(BF16)
        for lo, hi in cols:
            x2p.append(dot(y, wout_ref[:, lo:hi]))
            yield
        x2 = x1_ref[...] + jnp.concatenate(x2p, axis=1)
        h2 = _rms(x2, g2_ref[...]).astype(BF16)
        acc = None
        for clo, chi in FF_CHUNKS:
            groups = _col_groups(clo, chi)
            g = []
            for lo, hi in groups:
                g.append(dot(h2, wg_ref[:, lo:hi]))
                yield
            a = []
            for j, (lo, hi) in enumerate(groups):
                u = dot(h2, wu_ref[:, lo:hi])
                a.append((g[j] * _sigmoid(g[j]) * u).astype(BF16))
                yield
            a = jnp.concatenate(a, axis=1)
            part = []
            for lo, hi in cols:
                part.append(dot(a, wd_ref[clo:chi, lo:hi]))
                yield
            acc = part if acc is None else [p + q for p, q in zip(acc, part)]
        x3 = x2 + 0.5 * jnp.concatenate(acc, axis=1)
        out_ref[...] = _rms(x3, gfin_ref[...])
        yield

    _interleave(mixer(), dense())


def _out(att, ga, gm, x1, qkm, vm, om, gi, gf, cw, cb, bi, bf, hn, watt, wml, wout, g2, wg, wu, wd, gfin, tm, seq):
    t = x1.shape[0]
    n = t // tm

    def cur(width):
        return pl.BlockSpec((tm, width), lambda i: (jnp.minimum(i, n - 1), 0))

    lagged = pl.BlockSpec((tm, D_MODEL), lambda i: (jnp.maximum(i - 1, 0), 0))
    sq = _resident((D_MODEL, D_MODEL))
    vec = _resident((1, D_MODEL))
    return pl.pallas_call(
        functools.partial(_out_kernel, seq // tm, n),
        grid=(n + 1,),
        in_specs=[lagged, lagged, lagged, lagged,
                  cur(2 * ML_QK_W), cur(ML_V_W), cur(ML_V_W), cur(LANES), cur(LANES),
                  _resident((MLSTM_CONV, 2 * ML_QK_W)), _resident((1, 2 * ML_QK_W)),
                  _resident((1, LANES)), _resident((1, LANES)), vec,
                  sq, sq, sq, vec, _resident((D_MODEL, D_FF)), _resident((D_MODEL, D_FF)),
                  _resident((D_FF, D_MODEL)), vec],
        out_specs=lagged,
        out_shape=jax.ShapeDtypeStruct((t, D_MODEL), F32),
        scratch_shapes=[pltpu.VMEM((2, tm, ML_V_W), BF16),
                        pltpu.VMEM((MLSTM_HEADS, MLSTM_QK_DIM, MLSTM_V_DIM), F32),
                        pltpu.VMEM((SUBLANES, LANES), F32),
                        pltpu.VMEM((SUBLANES, LANES), F32),
                        pltpu.VMEM((CONV_TAIL_ROWS + MLSTM_CHUNK, 2 * ML_QK_W), F32)],
        compiler_params=pltpu.CompilerParams(dimension_semantics=("arbitrary",),
                                             vmem_limit_bytes=VMEM_LIMIT_BYTES),
        name="mlstm_merge_ffn2",
    )(att, ga, gm, x1, qkm, vm, om, gi, gf, cw, cb, bi, bf, hn, watt, wml, wout, g2, wg, wu, wd, gfin)


def _rope_tables(seq):
    half = ATT_HEAD_DIM // 2
    pos = jnp.arange(seq, dtype=F32)
    inv_freq = ROPE_THETA ** (-jnp.arange(half, dtype=F32) / half)
    ang = pos[:, None] * inv_freq[None, :]
    lane = np.arange(LANES)
    sign = jnp.asarray(np.where(lane % ATT_HEAD_DIM < half, -1.0, 1.0), F32)
    return jnp.cos(ang)[:, lane % half], jnp.sin(ang)[:, lane % half] * sign


def _pad_lanes(v):
    return jnp.pad(v, ((0, 0), (0, LANES - v.shape[-1])))


def _layer(x2d, batch, seq, p):
    (ffn1_norm, ffn1_w_gate, ffn1_w_up, ffn1_w_down, mix_norm, w_in, b_i, b_f, attn_sinks, conv_w, conv_b,
     head_norm, w_att, w_mlstm, w_out, ffn2_norm, ffn2_w_gate, ffn2_w_up, ffn2_w_down, final_norm) = p
    tm = min(TOKEN_TILE, seq)
    assert seq % tm == 0 and seq % ATT_BLOCK == 0 and seq % MLSTM_CHUNK == 0

    cos, sin = _rope_tables(seq)
    sinks = jnp.broadcast_to(attn_sinks[:, None], (ATT_HEADS, LANES)).astype(F32)

    x1, (w_q, w_mid, w_gate, w_if), (watt, wml, wout, wg2, wu2, wd2) = _ffn1(
        x2d, ffn1_norm[None], ffn1_w_gate.astype(BF16), ffn1_w_up.astype(BF16), ffn1_w_down.astype(BF16),
        w_in, (w_att, w_mlstm, w_out, ffn2_w_gate, ffn2_w_up, ffn2_w_down),
        FFN1_TOKEN_TILE if (batch * seq) % FFN1_TOKEN_TILE == 0 else tm)
    att, qkm, vm, om, ga, gm, gi, gf = _inproj(
        x1, mix_norm[None], cos, sin, w_q, w_mid, w_gate, w_if, sinks, tm, seq)
    return _out(att, ga, gm, x1, qkm, vm, om, gi, gf, conv_w, conv_b[None], _pad_lanes(b_i[None]),
                _pad_lanes(b_f[None]), head_norm[None], watt, wml, wout, ffn2_norm[None], wg2, wu2, wd2,
                final_norm[None], tm, seq)


def kernel(x, ffn1_norm, ffn1_w_gate, ffn1_w_up, ffn1_w_down, mix_norm, w_in, b_i, b_f, attn_sinks, conv_w,
           conv_b, head_norm, w_att, w_mlstm, w_out, ffn2_norm, ffn2_w_gate, ffn2_w_up, ffn2_w_down, final_norm):
    batch, seq, d = x.shape
    assert d == D_MODEL and ffn1_norm.shape[0] == 1, "single-layer kernel"
    per_layer = (ffn1_norm, ffn1_w_gate, ffn1_w_up, ffn1_w_down, mix_norm, w_in, b_i, b_f, attn_sinks, conv_w,
                 conv_b, head_norm, w_att, w_mlstm, w_out, ffn2_norm, ffn2_w_gate, ffn2_w_up, ffn2_w_down)
    params = tuple(a[0] for a in per_layer) + (final_norm,)
    out = _layer(x.reshape(batch * seq, d), batch, seq, params)
    return out.reshape(batch, seq, d)
```

```python
import functools

import jax
import jax.numpy as jnp
import numpy as np
from jax import lax
from jax.experimental import pallas as pl
from jax.experimental.pallas import tpu as pltpu

F32 = jnp.float32
BF16 = jnp.bfloat16

D_MODEL = 1024
ATT_HEADS = 16
ATT_KV_HEADS = 4
ATT_HEAD_DIM = 64
ATT_BLOCK = 128
ROPE_THETA = 10000.0
MLSTM_HEADS = 4
MLSTM_V_DIM = D_MODEL // MLSTM_HEADS
MLSTM_QK_DIM = MLSTM_V_DIM // 2
MLSTM_CHUNK = 128
MLSTM_CONV = 4
D_FF = 2816
RMS_EPS = 1e-5
NEG_LOG2E = -1.4426950408889634

ATT_Q_W = ATT_HEADS * ATT_HEAD_DIM
ATT_KV_W = ATT_KV_HEADS * ATT_HEAD_DIM
ML_QK_W = MLSTM_HEADS * MLSTM_QK_DIM
ML_V_W = MLSTM_HEADS * MLSTM_V_DIM

LANES = 128
MXU_COLS = 256
SUBLANES = 8
BF16_SUBLANES = 16
CONV_TAIL_ROWS = SUBLANES
VMEM_LIMIT_BYTES = 60 * 1024 * 1024

TOKEN_TILE = 512
FFN1_TOKEN_TILE = 1024
FF_CHUNKS = ((0, 768), (768, 1536), (1536, 2304), (2304, D_FF))

Q_CHUNKS_PER_GROUP = ATT_HEADS // ATT_KV_HEADS * ATT_HEAD_DIM // LANES

_C_KA = 0
_C_VA = _C_KA + ATT_KV_W
_C_QKM = _C_VA + ATT_KV_W
_C_VM = _C_QKM + 2 * ML_QK_W
_C_OM = _C_VM + ML_V_W
W_MID = _C_OM + ML_V_W

_IN_OFFS = tuple(int(v) for v in np.cumsum((0, ATT_Q_W, ATT_KV_W, ATT_KV_W, ML_QK_W, ML_QK_W, ML_V_W, ML_V_W,
                                            MLSTM_HEADS, MLSTM_HEADS, D_MODEL, D_MODEL)))
IN_PROJ_W = _IN_OFFS[-1]


def _resident(shape):
    return pl.BlockSpec(shape, lambda *_: (0,) * len(shape), pipeline_mode=pl.Buffered(1))


def _col_groups(lo, hi):
    return [(c, min(c + MXU_COLS, hi)) for c in range(lo, hi, MXU_COLS)]


def _interleave(mixer, dense):
    for n_dense in mixer:
        for _ in range(n_dense):
            next(dense, None)
    for _ in dense:
        pass


def _rms(x, g):
    ms = jnp.mean(x * x, axis=-1, keepdims=True)
    return x * lax.rsqrt(ms + RMS_EPS) * g


def _sigmoid(x):
    return 1.0 / (1.0 + jnp.exp2(x * NEG_LOG2E))


def _swiglu(h, wg_ref, wu_ref, wd_ref):
    acc = None
    for lo, hi in FF_CHUNKS:
        g = jnp.dot(h, wg_ref[:, lo:hi], preferred_element_type=F32)
        u = jnp.dot(h, wu_ref[:, lo:hi], preferred_element_type=F32)
        a = (g * _sigmoid(g) * u).astype(BF16)
        d = jnp.dot(a, wd_ref[lo:hi, :], preferred_element_type=F32)
        acc = d if acc is None else acc + d
    return acc


def _ffn1_kernel(x_ref, g_ref, wg_ref, wu_ref, wd_ref, win_ref, *refs):
    n_plain = (len(refs) - 5) // 2
    plain_in, (o_ref, wq_ref, wmid_ref, wgate_ref, wif_ref), plain_out = (
        refs[:n_plain], refs[n_plain:n_plain + 5], refs[n_plain + 5:])
    for src, dst in zip(plain_in, plain_out):
        dst[...] = src[...].astype(BF16)
    w = win_ref[...]
    wq_ref[...] = w[:, _IN_OFFS[0]:_IN_OFFS[1]].astype(BF16)
    wmid_ref[...] = w[:, _IN_OFFS[1]:_IN_OFFS[7]].astype(BF16)
    wgate_ref[...] = w[:, _IN_OFFS[9]:].astype(BF16)
    pad = jnp.zeros((w.shape[0], LANES - MLSTM_HEADS), F32)
    wif_ref[...] = jnp.concatenate([w[:, _IN_OFFS[7]:_IN_OFFS[8]], pad, w[:, _IN_OFFS[8]:_IN_OFFS[9]], pad],
                                   axis=1).astype(BF16)

    x = x_ref[...]
    h = _rms(x, g_ref[...]).astype(BF16)
    o_ref[...] = x + 0.5 * _swiglu(h, wg_ref, wu_ref, wd_ref)


def _row_block(rows, n_steps):
    return next(r for r in range(BF16_SUBLANES, rows + 1, BF16_SUBLANES)
                if rows % r == 0 and r * n_steps >= rows)


def _ffn1(x, g, wg, wu, wd, w_in, later_weights, tm):
    t = x.shape[0]
    assert t % tm == 0
    n = t // tm
    tok = pl.BlockSpec((tm, D_MODEL), lambda i: (i, 0))

    def streamed(w):
        r = _row_block(w.shape[0], n)
        last = w.shape[0] // r - 1
        return lambda width: pl.BlockSpec((r, width), lambda i: (jnp.minimum(i, last), 0))

    in_spec = streamed(w_in)
    plain = [streamed(w) for w in later_weights]
    in_widths = (ATT_Q_W, W_MID, 2 * D_MODEL, 2 * LANES)
    out = pl.pallas_call(
        _ffn1_kernel,
        grid=(n,),
        in_specs=[tok, _resident((1, D_MODEL)), _resident((D_MODEL, D_FF)), _resident((D_MODEL, D_FF)),
                  _resident((D_FF, D_MODEL)), in_spec(IN_PROJ_W)]
        + [spec(w.shape[1]) for spec, w in zip(plain, later_weights)],
        out_specs=[tok] + [in_spec(w_) for w_ in in_widths]
        + [spec(w.shape[1]) for spec, w in zip(plain, later_weights)],
        out_shape=[jax.ShapeDtypeStruct((t, D_MODEL), F32)]
        + [jax.ShapeDtypeStruct((D_MODEL, w_), BF16) for w_ in in_widths]
        + [jax.ShapeDtypeStruct(w.shape, BF16) for w in later_weights],
        compiler_params=pltpu.CompilerParams(dimension_semantics=("arbitrary",),
                                             vmem_limit_bytes=VMEM_LIMIT_BYTES),
        name="ffn1",
    )(x, g, wg, wu, wd, w_in, *later_weights)
    return out[0], out[1:5], out[5:]


def _rope(x, cos, sin_signed):
    lane = lax.broadcasted_iota(jnp.int32, x.shape, 1)
    half = ATT_HEAD_DIM // 2
    partner = jnp.where(lane % ATT_HEAD_DIM < half, pltpu.roll(x, LANES - half, axis=1),
                        pltpu.roll(x, half, axis=1))
    return x * cos + partner * sin_signed


def _attn_scores(g, q_blk, k_cur, k_prev):
    low = lax.broadcasted_iota(jnp.int32, (ATT_BLOCK, LANES), 1) < ATT_HEAD_DIM
    c, in_high = divmod(g, 2)
    ksl = slice(c * LANES, (c + 1) * LANES)
    parts = []
    for k in (k_cur, k_prev):
        x = k[:, ksl].astype(F32)
        other = pltpu.roll(x, ATT_HEAD_DIM, axis=1)
        in_low_lanes, in_high_lanes = (other, x) if in_high else (x, other)
        parts.append((jnp.where(low, in_low_lanes, 0.0), jnp.where(low, 0.0, in_high_lanes)))
    kstack = jnp.concatenate([parts[0][0], parts[1][0], parts[0][1], parts[1][1]], axis=0).astype(BF16)
    qstack = jnp.concatenate(
        [q_blk[:, (g * Q_CHUNKS_PER_GROUP + r) * LANES:(g * Q_CHUNKS_PER_GROUP + r + 1) * LANES]
         for r in range(Q_CHUNKS_PER_GROUP)], axis=0)
    return lax.dot_general(kstack, qstack, (((1,), (1,)), ((), ())), preferred_element_type=F32)


def _attn_block(q_blk, kv_cur, kv_prev, has_prev, sink_ref, pt_scr, o_ref, rows):
    blk = ATT_BLOCK
    key = lax.broadcasted_iota(jnp.int32, (blk, blk), 0)
    qry = lax.broadcasted_iota(jnp.int32, (blk, blk), 1)
    in_cur = key <= qry
    neg_inf = jnp.float32(-jnp.inf)
    st = [_attn_scores(g, q_blk, kv_cur[:, :ATT_KV_W], kv_prev[:, :ATT_KV_W]) for g in range(ATT_KV_HEADS)]
    yield 1
    v_cat = jnp.concatenate([kv_cur[:, ATT_KV_W:], kv_prev[:, ATT_KV_W:]], axis=0).astype(F32)
    vt_all = v_cat.T.astype(BF16)
    for g in range(ATT_KV_HEADS):
        for r in range(Q_CHUNKS_PER_GROUP):
            qs = slice(r * blk, (r + 1) * blk)
            for half in range(2):
                head = (g * Q_CHUNKS_PER_GROUP + r) * 2 + half
                s_cur = st[g][(2 * half) * blk:(2 * half + 1) * blk, qs]
                s_prev = st[g][(2 * half + 1) * blk:(2 * half + 2) * blk, qs]
                s = jnp.where(in_cur, s_cur, jnp.where(has_prev, s_prev, neg_inf))
                sink = sink_ref[head:head + 1, :]
                m = jnp.maximum(jnp.max(s, axis=0, keepdims=True), sink)
                p = jnp.exp(s - m)
                denom = jnp.sum(p, axis=0, keepdims=True) + jnp.exp(sink - m)
                p = p * (1.0 / denom)
                zp = jnp.zeros_like(p)
                pt_scr[g * 2 + half, :blk, qs] = jnp.where(in_cur, p, zp).astype(BF16)
                pt_scr[g * 2 + half, blk:, qs] = jnp.where(in_cur, zp, p).astype(BF16)
        vt = vt_all[g * ATT_HEAD_DIM:(g + 1) * ATT_HEAD_DIM, :]
        ot = [jnp.dot(vt, pt_scr[g * 2 + half], preferred_element_type=F32) for half in range(2)]
        for r in range(Q_CHUNKS_PER_GROUP):
            j = g * Q_CHUNKS_PER_GROUP + r
            qs = slice(r * blk, (r + 1) * blk)
            o_t = jnp.concatenate([ot[0][:, qs], ot[1][:, qs]], axis=0)
            o_ref[rows, j * LANES:(j + 1) * LANES] = o_t.T.astype(BF16)
    yield 0


def _inproj_kernel(blocks_per_seq, x_ref, g_ref, cos_ref, sin_ref, wq_ref, wmid_ref, wgate_ref, wif_ref, sink_ref,
                   att_ref, qkm_ref, vm_ref, om_ref, ga_ref, gm_ref, gi_ref, gf_ref, q_scr, kv_scr, pt_scr):
    tm = x_ref.shape[0]
    blk = ATT_BLOCK
    step = pl.program_id(0)
    slot = step % 2
    q_new, kv_new = q_scr.at[slot], kv_scr.at[slot]
    q_old, kv_old = q_scr.at[1 - slot], kv_scr.at[1 - slot]

    @pl.when(step == 0)
    def _():
        q_scr[...] = jnp.zeros_like(q_scr)
        kv_scr[...] = jnp.zeros_like(kv_scr)

    h = _rms(x_ref[...], g_ref[...]).astype(BF16)
    kv_new[:blk, :] = kv_old[tm:, :]

    def mixer():
        for k in range(tm // blk):
            rows = slice(k * blk, (k + 1) * blk)
            has_prev = ((step - 1) * (tm // blk) + k) % blocks_per_seq != 0
            yield from _attn_block(q_old[rows, :], kv_old[blk + k * blk:2 * blk + k * blk, :],
                                   kv_old[k * blk:(k + 1) * blk, :], has_prev, sink_ref, pt_scr, att_ref, rows)

    def dense():
        def proj(w_ref, lo, width):
            return jnp.dot(h, w_ref[:, lo:lo + width], preferred_element_type=F32)

        z = proj(wq_ref, 0, ATT_Q_W) * (ATT_HEAD_DIM ** -0.5)
        cos, sin = cos_ref[...], sin_ref[...]
        for j in range(ATT_Q_W // LANES):
            sl = slice(j * LANES, (j + 1) * LANES)
            q_new[:, sl] = _rope(z[:, sl], cos, sin).astype(BF16)
        yield
        z = proj(wmid_ref, _C_KA, 2 * ATT_KV_W)
        for j in range(ATT_KV_W // LANES):
            sl = slice(j * LANES, (j + 1) * LANES)
            kv_new[blk:, sl] = _rope(z[:, sl], cos, sin).astype(BF16)
        kv_new[blk:, ATT_KV_W:] = z[:, ATT_KV_W:].astype(BF16)
        vm_ref[...] = proj(wmid_ref, _C_VM, ML_V_W).astype(BF16)
        yield
        om_ref[...] = proj(wmid_ref, _C_OM, ML_V_W).astype(BF16)
        ga_ref[...] = proj(wgate_ref, 0, D_MODEL).astype(BF16)
        yield
        gm_ref[...] = proj(wgate_ref, D_MODEL, D_MODEL).astype(BF16)
        z = proj(wif_ref, 0, 2 * LANES)
        gi_ref[...] = z[:, :LANES]
        gf_ref[...] = z[:, LANES:]
        qkm_ref[...] = proj(wmid_ref, _C_QKM, 2 * ML_QK_W).astype(BF16)
        yield

    _interleave(mixer(), dense())


def _inproj(x1, g, cos, sin, wq, wmid, wgate, wif, sinks, tm, seq):
    t = x1.shape[0]
    n = t // tm
    tiles_per_seq = seq // tm

    def cur(i):
        return jnp.minimum(i, n - 1)

    def tok(width):
        return pl.BlockSpec((tm, width), lambda i: (cur(i), 0))

    pos = pl.BlockSpec((tm, LANES), lambda i: (cur(i) % tiles_per_seq, 0))
    lagged = pl.BlockSpec((tm, ATT_Q_W), lambda i: (jnp.maximum(i - 1, 0), 0))
    widths = (2 * ML_QK_W, ML_V_W, ML_V_W, D_MODEL, D_MODEL)
    out_shape = [jax.ShapeDtypeStruct((t, ATT_Q_W), BF16)]
    out_shape += [jax.ShapeDtypeStruct((t, w_), BF16) for w_ in widths]
    out_shape += [jax.ShapeDtypeStruct((t, LANES), F32)] * 2
    return pl.pallas_call(
        functools.partial(_inproj_kernel, seq // ATT_BLOCK),
        grid=(n + 1,),
        in_specs=[tok(D_MODEL), _resident((1, D_MODEL)), pos, pos, _resident((D_MODEL, ATT_Q_W)),
                  _resident((D_MODEL, W_MID)), _resident((D_MODEL, 2 * D_MODEL)), _resident((D_MODEL, 2 * LANES)),
                  _resident((ATT_HEADS, LANES))],
        out_specs=[lagged] + [tok(w_) for w_ in widths] + [tok(LANES)] * 2,
        out_shape=out_shape,
        scratch_shapes=[pltpu.VMEM((2, tm, ATT_Q_W), BF16),
                        pltpu.VMEM((2, ATT_BLOCK + tm, 2 * ATT_KV_W), BF16),
                        pltpu.VMEM((2 * ATT_KV_HEADS, 2 * ATT_BLOCK, Q_CHUNKS_PER_GROUP * ATT_BLOCK), BF16)],
        compiler_params=pltpu.CompilerParams(dimension_semantics=("arbitrary",),
                                             vmem_limit_bytes=VMEM_LIMIT_BYTES),
        name="in_proj_attn",
    )(x1, g, cos, sin, wq, wmid, wgate, wif, sinks)


def _scan_rows(x, op, fill):
    n = x.shape[0]
    row = lax.broadcasted_iota(jnp.int32, x.shape, 0)
    shift = 1
    while shift < n:
        prev = jnp.where(row >= shift, pltpu.roll(x, shift, axis=0), fill)
        x = op(x, prev)
        shift *= 2
    return x


def _mlstm_chunk_inputs(rows, qk_ref, gi_ref, gf_ref, cw_ref, cb_ref, bi_ref, bf_ref, conv_scr):
    L = MLSTM_CHUNK
    tail = CONV_TAIL_ROWS
    x = qk_ref[rows, :].astype(F32)
    conv_scr[tail:, :] = x
    y = cb_ref[...] + cw_ref[MLSTM_CONV - 1:MLSTM_CONV, :] * x
    for j in range(1, MLSTM_CONV):
        y = y + cw_ref[MLSTM_CONV - 1 - j:MLSTM_CONV - j, :] * conv_scr[tail - j:tail - j + L, :]
    conv_scr[:tail, :] = x[L - tail:, :]
    qk = y * _sigmoid(y)

    ig = gi_ref[rows, :] + bi_ref[...]
    fpre = (gf_ref[rows, :] + bf_ref[...]).T[:SUBLANES, :]
    logf = jnp.minimum(fpre, 0.0) - jnp.log1p(jnp.exp(-jnp.abs(fpre)))
    logf = jnp.concatenate([logf, jnp.zeros((LANES - SUBLANES, L), F32)], axis=0).T
    b = _scan_rows(logf, jnp.add, 0.0)
    g = ig - b
    cm = _scan_rows(g, jnp.maximum, -jnp.inf)
    return qk, b, g, cm


def _mlstm_gates(b, g, cm, m_scr):
    L = MLSTM_CHUNK
    m_prev = m_scr[0:1, :]
    u = jnp.maximum(cm, m_prev)
    g_max = cm[L - 1:L, :]
    u_last = u[L - 1:L, :]
    gates = dict(u=u, a=jnp.exp(g - g_max), inter=jnp.exp(m_prev - u), emt=jnp.exp(-(b + u)),
                 s_old=jnp.exp(m_prev - u_last), s_new=jnp.exp(g_max - u_last),
                 g_rows=g.T)
    m_scr[0:1, :] = b[L - 1:L, :] + u_last
    return gates


def _mlstm_head(h, rows, qk, gates, v_ref, o_ref, hn_ref, out_ref, ct_scr, n_scr):
    L = MLSTM_CHUNK
    row = lax.broadcasted_iota(jnp.int32, (L, L), 0)
    col = lax.broadcasted_iota(jnp.int32, (L, L), 1)
    causal = col <= row
    qs = slice(h * MLSTM_QK_DIM, (h + 1) * MLSTM_QK_DIM)
    ks = slice(ML_QK_W + h * MLSTM_QK_DIM, ML_QK_W + (h + 1) * MLSTM_QK_DIM)
    vs = slice(h * MLSTM_V_DIM, (h + 1) * MLSTM_V_DIM)
    qh = qk[:, qs]
    kh = qk[:, ks] * (MLSTM_QK_DIM ** -0.5)
    qb = qh.astype(BF16)
    kb = kh.astype(BF16)
    vb = v_ref[rows, vs]
    inter_col = gates["inter"][:, h:h + 1]
    a_col = gates["a"][:, h:h + 1]

    decay = jnp.where(causal, jnp.exp(gates["g_rows"][h:h + 1, :] - gates["u"][:, h:h + 1]), 0.0)
    s = lax.dot_general(qb, kb, (((1,), (1,)), ((), ())), preferred_element_type=F32) * decay
    ct = ct_scr[h]
    n_row = n_scr[h:h + 1, :]
    yield 1
    num = (jnp.dot(s.astype(BF16), vb, preferred_element_type=F32)
           + inter_col * jnp.dot(qb, ct.astype(BF16), preferred_element_type=F32))
    den = jnp.sum(s + inter_col * (qh * n_row), axis=-1, keepdims=True)
    hh = num * (1.0 / jnp.maximum(jnp.abs(den), gates["emt"][:, h:h + 1]))
    hn = _rms(hh, hn_ref[:, vs])
    out_ref[rows, vs] = (_sigmoid(o_ref[rows, vs].astype(F32)) * hn).astype(BF16)
    yield 1
    av = (a_col * vb.astype(F32)).astype(BF16)
    d_ct = lax.dot_general(kb, av, (((0,), (0,)), ((), ())), preferred_element_type=F32)
    d_n = jnp.sum(a_col * kh, axis=0, keepdims=True)
    so = gates["s_old"][:, h:h + 1]
    sn = gates["s_new"][:, h:h + 1]
    ct_scr[h] = so * ct + sn * d_ct
    n_scr[h:h + 1, :] = so * n_row + sn * d_n
    yield 1


def _out_kernel(tiles_per_seq, n_tiles, att_ref, ga_ref, gm_ref, x1_ref, qk_ref, v_ref, o_ref, gi_ref, gf_ref,
                cw_ref, cb_ref, bi_ref, bf_ref, hn_ref, watt_ref, wml_ref, wout_ref, g2_ref, wg_ref, wu_ref,
                wd_ref, gfin_ref, out_ref, hm_scr, ct_scr, n_scr, m_scr, conv_scr):
    tm = x1_ref.shape[0]
    L = MLSTM_CHUNK
    step = pl.program_id(0)
    slot = step % 2
    hm_new, hm_old = hm_scr.at[slot], hm_scr.at[1 - slot]

    @pl.when(step == 0)
    def _():
        hm_scr[...] = jnp.zeros_like(hm_scr)

    @pl.when(jnp.minimum(step, n_tiles - 1) % tiles_per_seq == 0)
    def _():
        ct_scr[...] = jnp.zeros_like(ct_scr)
        n_scr[...] = jnp.zeros_like(n_scr)
        m_scr[...] = jnp.zeros_like(m_scr)
        conv_scr[:CONV_TAIL_ROWS, :] = jnp.zeros((CONV_TAIL_ROWS, conv_scr.shape[1]), F32)

    def chunk_inputs(c):
        return _mlstm_chunk_inputs(slice(c * L, (c + 1) * L), qk_ref, gi_ref, gf_ref, cw_ref, cb_ref, bi_ref,
                                   bf_ref, conv_scr)

    def mixer():
        for c in range(tm // L):
            rows = slice(c * L, (c + 1) * L)
            qk, b, g, cm = chunk_inputs(c)
            gates = _mlstm_gates(b, g, cm, m_scr)
            yield 1
            for h in range(MLSTM_HEADS):
                yield from _mlstm_head(h, rows, qk, gates, v_ref, o_ref, hn_ref, hm_new, ct_scr, n_scr)

    def dense():
        def dot(x, w):
            return jnp.dot(x, w, preferred_element_type=F32)

        cols = _col_groups(0, D_MODEL)
        ya, ym, x2p = [], [], []
        for lo, hi in cols:
            ya.append(dot(att_ref[...], watt_ref[:, lo:hi]))
            yield
        for lo, hi in cols:
            ym.append(dot(hm_old[...], wml_ref[:, lo:hi]))
            yield
        y = jnp.concatenate(
            [_sigmoid(ga_ref[:, lo:hi].astype(F32)) * ya[j] + _sigmoid(gm_ref[:, lo:hi].astype(F32)) * ym[j]
             for j, (lo, hi) in enumerate(cols)], axis=1).astype(BF16)
        for lo, hi in cols:
            x2p.append(dot(y, wout_ref[:, lo:hi]))
            yield
        x2 = x1_ref[...] + jnp.concatenate(x2p, axis=1)
        h2 = _rms(x2, g2_ref[...]).astype(BF16)
        acc = None
        for clo, chi in FF_CHUNKS:
            groups = _col_groups(clo, chi)
            g = []
            for lo, hi in groups:
                g.append(dot(h2, wg_ref[:, lo:hi]))
                yield
            a = []
            for j, (lo, hi) in enumerate(groups):
                u = dot(h2, wu_ref[:, lo:hi])
                a.append((g[j] * _sigmoid(g[j]) * u).astype(BF16))
                yield
            a = jnp.concatenate(a, axis=1)
            part = []
            for lo, hi in cols:
                part.append(dot(a, wd_ref[clo:chi, lo:hi]))
                yield
            acc = part if acc is None else [p + q for p, q in zip(acc, part)]
        x3 = x2 + 0.5 * jnp.concatenate(acc, axis=1)
        out_ref[...] = _rms(x3, gfin_ref[...])
        yield

    _interleave(mixer(), dense())


def _out(att, ga, gm, x1, qkm, vm, om, gi, gf, cw, cb, bi, bf, hn, watt, wml, wout, g2, wg, wu, wd, gfin, tm, seq):
    t = x1.shape[0]
    n = t // tm

    def cur(width):
        return pl.BlockSpec((tm, width), lambda i: (jnp.minimum(i, n - 1), 0))

    lagged = pl.BlockSpec((tm, D_MODEL), lambda i: (jnp.maximum(i - 1, 0), 0))
    sq = _resident((D_MODEL, D_MODEL))
    vec = _resident((1, D_MODEL))
    return pl.pallas_call(
        functools.partial(_out_kernel, seq // tm, n),
        grid=(n + 1,),
        in_specs=[lagged, lagged, lagged, lagged,
                  cur(2 * ML_QK_W), cur(ML_V_W), cur(ML_V_W), cur(LANES), cur(LANES),
                  _resident((MLSTM_CONV, 2 * ML_QK_W)), _resident((1, 2 * ML_QK_W)),
                  _resident((1, LANES)), _resident((1, LANES)), vec,
                  sq, sq, sq, vec, _resident((D_MODEL, D_FF)), _resident((D_MODEL, D_FF)),
                  _resident((D_FF, D_MODEL)), vec],
        out_specs=lagged,
        out_shape=jax.ShapeDtypeStruct((t, D_MODEL), F32),
        scratch_shapes=[pltpu.VMEM((2, tm, ML_V_W), BF16),
                        pltpu.VMEM((MLSTM_HEADS, MLSTM_QK_DIM, MLSTM_V_DIM), F32),
                        pltpu.VMEM((SUBLANES, LANES), F32),
                        pltpu.VMEM((SUBLANES, LANES), F32),
                        pltpu.VMEM((CONV_TAIL_ROWS + MLSTM_CHUNK, 2 * ML_QK_W), F32)],
        compiler_params=pltpu.CompilerParams(dimension_semantics=("arbitrary",),
                                             vmem_limit_bytes=VMEM_LIMIT_BYTES),
        name="mlstm_merge_ffn2",
    )(att, ga, gm, x1, qkm, vm, om, gi, gf, cw, cb, bi, bf, hn, watt, wml, wout, g2, wg, wu, wd, gfin)


def _rope_tables(seq):
    half = ATT_HEAD_DIM // 2
    pos = jnp.arange(seq, dtype=F32)
    inv_freq = ROPE_THETA ** (-jnp.arange(half, dtype=F32) / half)
    ang = pos[:, None] * inv_freq[None, :]
    lane = np.arange(LANES)
    sign = jnp.asarray(np.where(lane % ATT_HEAD_DIM < half, -1.0, 1.0), F32)
    return jnp.cos(ang)[:, lane % half], jnp.sin(ang)[:, lane % half] * sign


def _pad_lanes(v):
    return jnp.pad(v, ((0, 0), (0, LANES - v.shape[-1])))


def _layer(x2d, batch, seq, p):
    (ffn1_norm, ffn1_w_gate, ffn1_w_up, ffn1_w_down, mix_norm, w_in, b_i, b_f, attn_sinks, conv_w, conv_b,
     head_norm, w_att, w_mlstm, w_out, ffn2_norm, ffn2_w_gate, ffn2_w_up, ffn2_w_down, final_norm) = p
    tm = min(TOKEN_TILE, seq)
    assert seq % tm == 0 and seq % ATT_BLOCK == 0 and seq % MLSTM_CHUNK == 0

    cos, sin = _rope_tables(seq)
    sinks = jnp.broadcast_to(attn_sinks[:, None], (ATT_HEADS, LANES)).astype(F32)

    x1, (w_q, w_mid, w_gate, w_if), (watt, wml, wout, wg2, wu2, wd2) = _ffn1(
        x2d, ffn1_norm[None], ffn1_w_gate.astype(BF16), ffn1_w_up.astype(BF16), ffn1_w_down.astype(BF16),
        w_in, (w_att, w_mlstm, w_out, ffn2_w_gate, ffn2_w_up, ffn2_w_down),
        FFN1_TOKEN_TILE if (batch * seq) % FFN1_TOKEN_TILE == 0 else tm)
    att, qkm, vm, om, ga, gm, gi, gf = _inproj(
        x1, mix_norm[None], cos, sin, w_q, w_mid, w_gate, w_if, sinks, tm, seq)
    return _out(att, ga, gm, x1, qkm, vm, om, gi, gf, conv_w, conv_b[None], _pad_lanes(b_i[None]),
                _pad_lanes(b_f[None]), head_norm[None], watt, wml, wout, ffn2_norm[None], wg2, wu2, wd2,
                final_norm[None], tm, seq)


def kernel(x, ffn1_norm, ffn1_w_gate, ffn1_w_up, ffn1_w_down, mix_norm, w_in, b_i, b_f, attn_sinks, conv_w,
           conv_b, head_norm, w_att, w_mlstm, w_out, ffn2_norm, ffn2_w_gate, ffn2_w_up, ffn2_w_down, final_norm):
    batch, seq, d = x.shape
    assert d == D_MODEL and ffn1_norm.shape[0] == 1, "single-layer kernel"
    per_layer = (ffn1_norm, ffn1_w_gate, ffn1_w_up, ffn1_w_down, mix_norm, w_in, b_i, b_f, attn_sinks, conv_w,
                 conv_b, head_norm, w_att, w_mlstm, w_out, ffn2_norm, ffn2_w_gate, ffn2_w_up, ffn2_w_down)
    params = tuple(a[0] for a in per_layer) + (final_norm,)
    out = _layer(x.reshape(batch * seq, d), batch, seq, params)
    return out.reshape(batch, seq, d)
```

```python
import functools

import jax
import jax.numpy as jnp
import numpy as np
from jax import lax
from jax.experimental import pallas as pl
from jax.experimental.pallas import tpu as pltpu

F32 = jnp.float32
BF16 = jnp.bfloat16

D_MODEL = 1024
ATT_HEADS = 16
ATT_KV_HEADS = 4
ATT_HEAD_DIM = 64
ATT_BLOCK = 128
ROPE_THETA = 10000.0
MLSTM_HEADS = 4
MLSTM_V_DIM = D_MODEL // MLSTM_HEADS
MLSTM_QK_DIM = MLSTM_V_DIM // 2
MLSTM_CHUNK = 128
MLSTM_CONV = 4
D_FF = 2816
RMS_EPS = 1e-5
NEG_LOG2E = -1.4426950408889634

ATT_Q_W = ATT_HEADS * ATT_HEAD_DIM
ATT_KV_W = ATT_KV_HEADS * ATT_HEAD_DIM
ML_QK_W = MLSTM_HEADS * MLSTM_QK_DIM
ML_V_W = MLSTM_HEADS * MLSTM_V_DIM

LANES = 128
MXU_COLS = 256
SUBLANES = 8
BF16_SUBLANES = 16
CONV_TAIL_ROWS = SUBLANES
VMEM_LIMIT_BYTES = 60 * 1024 * 1024

TOKEN_TILE = 512
FFN1_TOKEN_TILE = 1024
FF_CHUNKS = ((0, 768), (768, 1536), (1536, 2304), (2304, D_FF))

Q_CHUNKS_PER_GROUP = ATT_HEADS // ATT_KV_HEADS * ATT_HEAD_DIM // LANES

_C_KA = 0
_C_VA = _C_KA + ATT_KV_W
_C_QKM = _C_VA + ATT_KV_W
_C_VM = _C_QKM + 2 * ML_QK_W
_C_OM = _C_VM + ML_V_W
W_MID = _C_OM + ML_V_W

_IN_OFFS = tuple(int(v) for v in np.cumsum((0, ATT_Q_W, ATT_KV_W, ATT_KV_W, ML_QK_W, ML_QK_W, ML_V_W, ML_V_W,
                                            MLSTM_HEADS, MLSTM_HEADS, D_MODEL, D_MODEL)))
IN_PROJ_W = _IN_OFFS[-1]
WT_ROWS = MXU_COLS
assert ATT_Q_W % WT_ROWS == 0 and W_MID % WT_ROWS == 0 and 2 * MLSTM_HEADS == SUBLANES
assert (2 * D_MODEL) % WT_ROWS == 0 and IN_PROJ_W % SUBLANES == 0


def _resident(shape):
    return pl.BlockSpec(shape, lambda *_: (0,) * len(shape), pipeline_mode=pl.Buffered(1))


def _col_groups(lo, hi):
    return [(c, min(c + MXU_COLS, hi)) for c in range(lo, hi, MXU_COLS)]


def _interleave(mixer, dense):
    for n_dense in mixer:
        for _ in range(n_dense):
            next(dense, None)
    for _ in dense:
        pass


def _rms(x, g):
    ms = jnp.mean(x * x, axis=-1, keepdims=True)
    return x * lax.rsqrt(ms + RMS_EPS) * g


def _sigmoid(x):
    return 1.0 / (1.0 + jnp.exp2(x * NEG_LOG2E))


def _swiglu(h, wg_ref, wu_ref, wd_ref):
    acc = None
    for lo, hi in FF_CHUNKS:
        g = jnp.dot(h, wg_ref[:, lo:hi], preferred_element_type=F32)
        u = jnp.dot(h, wu_ref[:, lo:hi], preferred_element_type=F32)
        a = (g * _sigmoid(g) * u).astype(BF16)
        d = jnp.dot(a, wd_ref[lo:hi, :], preferred_element_type=F32)
        acc = d if acc is None else acc + d
    return acc


def _ffn1_kernel(transposed, x_ref, g_ref, wg_ref, wu_ref, wd_ref, win_ref, *refs):
    if transposed:
        wtail_ref, refs = refs[0], refs[1:]
    n_plain = (len(refs) - 5) // 2
    plain_in, (o_ref, wq_ref, wmid_ref, wgate_ref, wif_ref), plain_out = (
        refs[:n_plain], refs[n_plain:n_plain + 5], refs[n_plain + 5:])
    for src, dst in zip(plain_in, plain_out):
        dst[...] = src[...].astype(BF16)
    if transposed:
        a = win_ref[...]
        at = a.T.astype(BF16)
        wq_ref[...] = at
        wmid_ref[...] = at
        wgate_ref[...] = jnp.concatenate([a[SUBLANES:], wtail_ref[...]], axis=0).T.astype(BF16)
        pad = jnp.zeros((at.shape[0], LANES - MLSTM_HEADS), BF16)
        wif_ref[...] = jnp.concatenate([at[:, :MLSTM_HEADS], pad, at[:, MLSTM_HEADS:2 * MLSTM_HEADS], pad], axis=1)
    else:
        w = win_ref[...]
        wq_ref[...] = w[:, _IN_OFFS[0]:_IN_OFFS[1]].astype(BF16)
        wmid_ref[...] = w[:, _IN_OFFS[1]:_IN_OFFS[7]].astype(BF16)
        wgate_ref[...] = w[:, _IN_OFFS[9]:].astype(BF16)
        pad = jnp.zeros((w.shape[0], LANES - MLSTM_HEADS), F32)
        wif_ref[...] = jnp.concatenate([w[:, _IN_OFFS[7]:_IN_OFFS[8]], pad, w[:, _IN_OFFS[8]:_IN_OFFS[9]], pad],
                                       axis=1).astype(BF16)

    x = x_ref[...]
    h = _rms(x, g_ref[...]).astype(BF16)
    o_ref[...] = x + 0.5 * _swiglu(h, wg_ref, wu_ref, wd_ref)


def _row_block(rows, n_steps):
    return next(r for r in range(BF16_SUBLANES, rows + 1, BF16_SUBLANES)
                if rows % r == 0 and r * n_steps >= rows)


def _ffn1(x, g, wg, wu, wd, w_in, later_weights, tm):
    t = x.shape[0]
    assert t % tm == 0
    n = t // tm
    tok = pl.BlockSpec((tm, D_MODEL), lambda i: (i, 0))

    def streamed(w):
        r = _row_block(w.shape[0], n)
        last = w.shape[0] // r - 1
        return lambda width: pl.BlockSpec((r, width), lambda i: (jnp.minimum(i, last), 0))

    plain = [streamed(w) for w in later_weights]
    in_widths = (ATT_Q_W, W_MID, 2 * D_MODEL, 2 * LANES)
    q_blks, mid_blks, gate_blks = ATT_Q_W // WT_ROWS, W_MID // WT_ROWS, 2 * D_MODEL // WT_ROWS
    if_blk = q_blks + mid_blks
    transposed = n >= if_blk + gate_blks
    if transposed:
        tail_per_blk, tail_last = WT_ROWS // SUBLANES, IN_PROJ_W // SUBLANES - 1
        win_arg = (w_in.T, w_in.T)
        win_specs = [
            pl.BlockSpec((WT_ROWS, D_MODEL), lambda i: (jnp.minimum(i, if_blk + gate_blks - 1), 0)),
            pl.BlockSpec((SUBLANES, D_MODEL), lambda i: (jnp.minimum((i + 1) * tail_per_blk, tail_last), 0))]
        col_blk = lambda f: pl.BlockSpec((D_MODEL, WT_ROWS), lambda i: (0, f(i)))
        win_out_specs = [col_blk(lambda i: jnp.minimum(i, q_blks)),
                         col_blk(lambda i: jnp.clip(i - q_blks, 0, mid_blks)),
                         col_blk(lambda i: jnp.clip(i - if_blk, 0, gate_blks)),
                         col_blk(lambda i: jnp.where(i == if_blk, 0, jnp.where(i < if_blk, 1, 2)))]
        out_widths = [w_ + WT_ROWS for w_ in in_widths[:3]] + [in_widths[3] + 2 * WT_ROWS]
    else:
        in_spec = streamed(w_in)
        win_arg, win_specs = (w_in,), [in_spec(IN_PROJ_W)]
        win_out_specs, out_widths = [in_spec(w_) for w_ in in_widths], in_widths
    out = pl.pallas_call(
        functools.partial(_ffn1_kernel, transposed),
        grid=(n,),
        in_specs=[tok, _resident((1, D_MODEL)), _resident((D_MODEL, D_FF)), _resident((D_MODEL, D_FF)),
                  _resident((D_FF, D_MODEL))] + win_specs
        + [spec(w.shape[1]) for spec, w in zip(plain, later_weights)],
        out_specs=[tok] + win_out_specs
        + [spec(w.shape[1]) for spec, w in zip(plain, later_weights)],
        out_shape=[jax.ShapeDtypeStruct((t, D_MODEL), F32)]
        + [jax.ShapeDtypeStruct((D_MODEL, w_), BF16) for w_ in out_widths]
        + [jax.ShapeDtypeStruct(w.shape, BF16) for w in later_weights],
        compiler_params=pltpu.CompilerParams(dimension_semantics=("arbitrary",),
                                             vmem_limit_bytes=VMEM_LIMIT_BYTES),
        name="ffn1",
    )(x, g, wg, wu, wd, *win_arg, *later_weights)
    return out[0], out[1:5], out[5:]


def _rope(x, cos, sin_signed):
    lane = lax.broadcasted_iota(jnp.int32, x.shape, 1)
    half = ATT_HEAD_DIM // 2
    partner = jnp.where(lane % ATT_HEAD_DIM < half, pltpu.roll(x, LANES - half, axis=1),
                        pltpu.roll(x, half, axis=1))
    return x * cos + partner * sin_signed


def _attn_scores(g, q_blk, k_cur, k_prev):
    low = lax.broadcasted_iota(jnp.int32, (ATT_BLOCK, LANES), 1) < ATT_HEAD_DIM
    c, in_high = divmod(g, 2)
    ksl = slice(c * LANES, (c + 1) * LANES)
    parts = []
    for k in (k_cur, k_prev):
        x = k[:, ksl].astype(F32)
        other = pltpu.roll(x, ATT_HEAD_DIM, axis=1)
        in_low_lanes, in_high_lanes = (other, x) if in_high else (x, other)
        parts.append((jnp.where(low, in_low_lanes, 0.0), jnp.where(low, 0.0, in_high_lanes)))
    kstack = jnp.concatenate([parts[0][0], parts[1][0], parts[0][1], parts[1][1]], axis=0).astype(BF16)
    qstack = jnp.concatenate(
        [q_blk[:, (g * Q_CHUNKS_PER_GROUP + r) * LANES:(g * Q_CHUNKS_PER_GROUP + r + 1) * LANES]
         for r in range(Q_CHUNKS_PER_GROUP)], axis=0)
    return lax.dot_general(kstack, qstack, (((1,), (1,)), ((), ())), preferred_element_type=F32)


def _attn_block(q_blk, kv_cur, kv_prev, has_prev, sink_ref, pt_scr, o_ref, rows):
    blk = ATT_BLOCK
    key = lax.broadcasted_iota(jnp.int32, (blk, blk), 0)
    qry = lax.broadcasted_iota(jnp.int32, (blk, blk), 1)
    in_cur = key <= qry
    neg_inf = jnp.float32(-jnp.inf)
    st = [_attn_scores(g, q_blk, kv_cur[:, :ATT_KV_W], kv_prev[:, :ATT_KV_W]) for g in range(ATT_KV_HEADS)]
    yield 1
    v_cat = jnp.concatenate([kv_cur[:, ATT_KV_W:], kv_prev[:, ATT_KV_W:]], axis=0).astype(F32)
    vt_all = v_cat.T.astype(BF16)
    for g in range(ATT_KV_HEADS):
        for r in range(Q_CHUNKS_PER_GROUP):
            qs = slice(r * blk, (r + 1) * blk)
            for half in range(2):
                head = (g * Q_CHUNKS_PER_GROUP + r) * 2 + half
                s_cur = st[g][(2 * half) * blk:(2 * half + 1) * blk, qs]
                s_prev = st[g][(2 * half + 1) * blk:(2 * half + 2) * blk, qs]
                s = jnp.where(in_cur, s_cur, jnp.where(has_prev, s_prev, neg_inf))
                sink = sink_ref[head:head + 1, :]
                m = jnp.maximum(jnp.max(s, axis=0, keepdims=True), sink)
                p = jnp.exp(s - m)
                denom = jnp.sum(p, axis=0, keepdims=True) + jnp.exp(sink - m)
                p = p * (1.0 / denom)
                zp = jnp.zeros_like(p)
                pt_scr[g * 2 + half, :blk, qs] = jnp.where(in_cur, p, zp).astype(BF16)
                pt_scr[g * 2 + half, blk:, qs] = jnp.where(in_cur, zp, p).astype(BF16)
        vt = vt_all[g * ATT_HEAD_DIM:(g + 1) * ATT_HEAD_DIM, :]
        ot = [jnp.dot(vt, pt_scr[g * 2 + half], preferred_element_type=F32) for half in range(2)]
        for r in range(Q_CHUNKS_PER_GROUP):
            j = g * Q_CHUNKS_PER_GROUP + r
            qs = slice(r * blk, (r + 1) * blk)
            o_t = jnp.concatenate([ot[0][:, qs], ot[1][:, qs]], axis=0)
            o_ref[rows, j * LANES:(j + 1) * LANES] = o_t.T.astype(BF16)
    yield 0


def _inproj_kernel(blocks_per_seq, x_ref, g_ref, cos_ref, sin_ref, wq_ref, wmid_ref, wgate_ref, wif_ref, sink_ref,
                   att_ref, qkm_ref, vm_ref, om_ref, ga_ref, gm_ref, gi_ref, gf_ref, q_scr, kv_scr, pt_scr):
    tm = x_ref.shape[0]
    blk = ATT_BLOCK
    step = pl.program_id(0)
    slot = step % 2
    q_new, kv_new = q_scr.at[slot], kv_scr.at[slot]
    q_old, kv_old = q_scr.at[1 - slot], kv_scr.at[1 - slot]

    @pl.when(step == 0)
    def _():
        q_scr[...] = jnp.zeros_like(q_scr)
        kv_scr[...] = jnp.zeros_like(kv_scr)

    h = _rms(x_ref[...], g_ref[...]).astype(BF16)
    kv_new[:blk, :] = kv_old[tm:, :]

    def mixer():
        for k in range(tm // blk):
            rows = slice(k * blk, (k + 1) * blk)
            has_prev = ((step - 1) * (tm // blk) + k) % blocks_per_seq != 0
            yield from _attn_block(q_old[rows, :], kv_old[blk + k * blk:2 * blk + k * blk, :],
                                   kv_old[k * blk:(k + 1) * blk, :], has_prev, sink_ref, pt_scr, att_ref, rows)

    def dense():
        def proj(w_ref, lo, width):
            return jnp.dot(h, w_ref[:, lo:lo + width], preferred_element_type=F32)

        z = proj(wq_ref, 0, ATT_Q_W) * (ATT_HEAD_DIM ** -0.5)
        cos, sin = cos_ref[...], sin_ref[...]
        for j in range(ATT_Q_W // LANES):
            sl = slice(j * LANES, (j + 1) * LANES)
            q_new[:, sl] = _rope(z[:, sl], cos, sin).astype(BF16)
        yield
        z = proj(wmid_ref, _C_KA, 2 * ATT_KV_W)
        for j in range(ATT_KV_W // LANES):
            sl = slice(j * LANES, (j + 1) * LANES)
            kv_new[blk:, sl] = _rope(z[:, sl], cos, sin).astype(BF16)
        kv_new[blk:, ATT_KV_W:] = z[:, ATT_KV_W:].astype(BF16)
        vm_ref[...] = proj(wmid_ref, _C_VM, ML_V_W).astype(BF16)
        yield
        om_ref[...] = proj(wmid_ref, _C_OM, ML_V_W).astype(BF16)
        ga_ref[...] = proj(wgate_ref, 0, D_MODEL).astype(BF16)
        yield
        gm_ref[...] = proj(wgate_ref, D_MODEL, D_MODEL).astype(BF16)
        z = proj(wif_ref, 0, 2 * LANES)
        gi_ref[...] = z[:, :LANES]
        gf_ref[...] = z[:, LANES:]
        qkm_ref[...] = proj(wmid_ref, _C_QKM, 2 * ML_QK_W).astype(BF16)
        yield

    _interleave(mixer(), dense())


def _inproj(x1, g, cos, sin, wq, wmid, wgate, wif, sinks, tm, seq):
    t = x1.shape[0]
    n = t // tm
    tiles_per_seq = seq // tm

    def cur(i):
        return jnp.minimum(i, n - 1)

    def tok(width):
        return pl.BlockSpec((tm, width), lambda i: (cur(i), 0))

    pos = pl.BlockSpec((tm, LANES), lambda i: (cur(i) % tiles_per_seq, 0))
    lagged = pl.BlockSpec((tm, ATT_Q_W), lambda i: (jnp.maximum(i - 1, 0), 0))
    widths = (2 * ML_QK_W, ML_V_W, ML_V_W, D_MODEL, D_MODEL)
    out_shape = [jax.ShapeDtypeStruct((t, ATT_Q_W), BF16)]
    out_shape += [jax.ShapeDtypeStruct((t, w_), BF16) for w_ in widths]
    out_shape += [jax.ShapeDtypeStruct((t, LANES), F32)] * 2
    return pl.pallas_call(
        functools.partial(_inproj_kernel, seq // ATT_BLOCK),
        grid=(n + 1,),
        in_specs=[tok(D_MODEL), _resident((1, D_MODEL)), pos, pos, _resident((D_MODEL, ATT_Q_W)),
                  _resident((D_MODEL, W_MID)), _resident((D_MODEL, 2 * D_MODEL)), _resident((D_MODEL, 2 * LANES)),
                  _resident((ATT_HEADS, LANES))],
        out_specs=[lagged] + [tok(w_) for w_ in widths] + [tok(LANES)] * 2,
        out_shape=out_shape,
        scratch_shapes=[pltpu.VMEM((2, tm, ATT_Q_W), BF16),
                        pltpu.VMEM((2, ATT_BLOCK + tm, 2 * ATT_KV_W), BF16),
                        pltpu.VMEM((2 * ATT_KV_HEADS, 2 * ATT_BLOCK, Q_CHUNKS_PER_GROUP * ATT_BLOCK), BF16)],
        compiler_params=pltpu.CompilerParams(dimension_semantics=("arbitrary",),
                                             vmem_limit_bytes=VMEM_LIMIT_BYTES),
        name="in_proj_attn",
    )(x1, g, cos, sin, wq, wmid, wgate, wif, sinks)


def _scan_rows(x, op, fill):
    n = x.shape[0]
    row = lax.broadcasted_iota(jnp.int32, x.shape, 0)
    shift = 1
    while shift < n:
        prev = jnp.where(row >= shift, pltpu.roll(x, shift, axis=0), fill)
        x = op(x, prev)
        shift *= 2
    return x


def _mlstm_chunk_inputs(rows, qk_ref, gi_ref, gf_ref, cw_ref, cb_ref, bi_ref, bf_ref, conv_scr):
    L = MLSTM_CHUNK
    tail = CONV_TAIL_ROWS
    x = qk_ref[rows, :].astype(F32)
    conv_scr[tail:, :] = x
    y = cb_ref[...] + cw_ref[MLSTM_CONV - 1:MLSTM_CONV, :] * x
    for j in range(1, MLSTM_CONV):
        y = y + cw_ref[MLSTM_CONV - 1 - j:MLSTM_CONV - j, :] * conv_scr[tail - j:tail - j + L, :]
    conv_scr[:tail, :] = x[L - tail:, :]
    qk = y * _sigmoid(y)

    ig = gi_ref[rows, :] + bi_ref[...]
    fpre = (gf_ref[rows, :] + bf_ref[...]).T[:SUBLANES, :]
    logf = jnp.minimum(fpre, 0.0) - jnp.log1p(jnp.exp(-jnp.abs(fpre)))
    logf = jnp.concatenate([logf, jnp.zeros((LANES - SUBLANES, L), F32)], axis=0).T
    b = _scan_rows(logf, jnp.add, 0.0)
    g = ig - b
    cm = _scan_rows(g, jnp.maximum, -jnp.inf)
    return qk, b, g, cm


def _mlstm_gates(b, g, cm, m_scr):
    L = MLSTM_CHUNK
    m_prev = m_scr[0:1, :]
    u = jnp.maximum(cm, m_prev)
    g_max = cm[L - 1:L, :]
    u_last = u[L - 1:L, :]
    gates = dict(u=u, a=jnp.exp(g - g_max), inter=jnp.exp(m_prev - u), emt=jnp.exp(-(b + u)),
                 s_old=jnp.exp(m_prev - u_last), s_new=jnp.exp(g_max - u_last),
                 g_rows=g.T)
    m_scr[0:1, :] = b[L - 1:L, :] + u_last
    return gates


def _mlstm_head(h, rows, qk, gates, v_ref, o_ref, hn_ref, out_ref, ct_scr, n_scr):
    L = MLSTM_CHUNK
    row = lax.broadcasted_iota(jnp.int32, (L, L), 0)
    col = lax.broadcasted_iota(jnp.int32, (L, L), 1)
    causal = col <= row
    qs = slice(h * MLSTM_QK_DIM, (h + 1) * MLSTM_QK_DIM)
    ks = slice(ML_QK_W + h * MLSTM_QK_DIM, ML_QK_W + (h + 1) * MLSTM_QK_DIM)
    vs = slice(h * MLSTM_V_DIM, (h + 1) * MLSTM_V_DIM)
    qh = qk[:, qs]
    kh = qk[:, ks] * (MLSTM_QK_DIM ** -0.5)
    qb = qh.astype(BF16)
    kb = kh.astype(BF16)
    vb = v_ref[rows, vs]
    inter_col = gates["inter"][:, h:h + 1]
    a_col = gates["a"][:, h:h + 1]

    decay = jnp.where(causal, jnp.exp(gates["g_rows"][h:h + 1, :] - gates["u"][:, h:h + 1]), 0.0)
    s = lax.dot_general(qb, kb, (((1,), (1,)), ((), ())), preferred_element_type=F32) * decay
    ct = ct_scr[h]
    n_row = n_scr[h:h + 1, :]
    yield 1
    num = (jnp.dot(s.astype(BF16), vb, preferred_element_type=F32)
           + inter_col * jnp.dot(qb, ct.astype(BF16), preferred_element_type=F32))
    den = jnp.sum(s + inter_col * (qh * n_row), axis=-1, keepdims=True)
    hh = num * (1.0 / jnp.maximum(jnp.abs(den), gates["emt"][:, h:h + 1]))
    hn = _rms(hh, hn_ref[:, vs])
    out_ref[rows, vs] = (_sigmoid(o_ref[rows, vs].astype(F32)) * hn).astype(BF16)
    yield 1
    av = (a_col * vb.astype(F32)).astype(BF16)
    d_ct = lax.dot_general(kb, av, (((0,), (0,)), ((), ())), preferred_element_type=F32)
    d_n = jnp.sum(a_col * kh, axis=0, keepdims=True)
    so = gates["s_old"][:, h:h + 1]
    sn = gates["s_new"][:, h:h + 1]
    ct_scr[h] = so * ct + sn * d_ct
    n_scr[h:h + 1, :] = so * n_row + sn * d_n
    yield 1


def _out_kernel(tiles_per_seq, n_tiles, att_ref, ga_ref, gm_ref, x1_ref, qk_ref, v_ref, o_ref, gi_ref, gf_ref,
                cw_ref, cb_ref, bi_ref, bf_ref, hn_ref, watt_ref, wml_ref, wout_ref, g2_ref, wg_ref, wu_ref,
                wd_ref, gfin_ref, out_ref, hm_scr, ct_scr, n_scr, m_scr, conv_scr):
    tm = x1_ref.shape[0]
    L = MLSTM_CHUNK
    step = pl.program_id(0)
    slot = step % 2
    hm_new, hm_old = hm_scr.at[slot], hm_scr.at[1 - slot]

    @pl.when(step == 0)
    def _():
        hm_scr[...] = jnp.zeros_like(hm_scr)

    @pl.when(jnp.minimum(step, n_tiles - 1) % tiles_per_seq == 0)
    def _():
        ct_scr[...] = jnp.zeros_like(ct_scr)
        n_scr[...] = jnp.zeros_like(n_scr)
        m_scr[...] = jnp.zeros_like(m_scr)
        conv_scr[:CONV_TAIL_ROWS, :] = jnp.zeros((CONV_TAIL_ROWS, conv_scr.shape[1]), F32)

    def chunk_inputs(c):
        return _mlstm_chunk_inputs(slice(c * L, (c + 1) * L), qk_ref, gi_ref, gf_ref, cw_ref, cb_ref, bi_ref,
                                   bf_ref, conv_scr)

    def mixer():
        for c in range(tm // L):
            rows = slice(c * L, (c + 1) * L)
            qk, b, g, cm = chunk_inputs(c)
            gates = _mlstm_gates(b, g, cm, m_scr)
            yield 1
            for h in range(MLSTM_HEADS):
                yield from _mlstm_head(h, rows, qk, gates, v_ref, o_ref, hn_ref, hm_new, ct_scr, n_scr)

    def dense():
        def dot(x, w):
            return jnp.dot(x, w, preferred_element_type=F32)

        cols = _col_groups(0, D_MODEL)
        ya, ym, x2p = [], [], []
        for lo, hi in cols:
            ya.append(dot(att_ref[...], watt_ref[:, lo:hi]))
            yield
        for lo, hi in cols:
            ym.append(dot(hm_old[...], wml_ref[:, lo:hi]))
            yield
        y = jnp.concatenate(
            [_sigmoid(ga_ref[:, lo:hi].astype(F32)) * ya[j] + _sigmoid(gm_ref[:, lo:hi].astype(F32)) * ym[j]
             for j, (lo, hi) in enumerate(cols)], axis=1).astype(BF16)
        for lo, hi in cols:
            x2p.append(dot(y, wout_ref[:, lo:hi]))
            yield
        x2 = x1_ref[...] + jnp.concatenate(x2p, axis=1)
        h2 = _rms(x2, g2_ref[...]).astype(BF16)
        acc = None
        for clo, chi in FF_CHUNKS:
            groups = _col_groups(clo, chi)
            g = []
            for lo, hi in groups:
                g.append(dot(h2, wg_ref[:, lo:hi]))
                yield
            a = []
            for j, (lo, hi) in enumerate(groups):
                u = dot(h2, wu_ref[:, lo:hi])
                a.append((g[j] * _sigmoid(g[j]) * u).astype(BF16))
                yield
            a = jnp.concatenate(a, axis=1)
            part = []
            for lo, hi in cols:
                part.append(dot(a, wd_ref[clo:chi, lo:hi]))
                yield
            acc = part if acc is None else [p + q for p, q in zip(acc, part)]
        x3 = x2 + 0.5 * jnp.concatenate(acc, axis=1)
        out_ref[...] = _rms(x3, gfin_ref[...])
        yield

    _interleave(mixer(), dense())


def _out(att, ga, gm, x1, qkm, vm, om, gi, gf, cw, cb, bi, bf, hn, watt, wml, wout, g2, wg, wu, wd, gfin, tm, seq):
    t = x1.shape[0]
    n = t // tm

    def cur(width):
        return pl.BlockSpec((tm, width), lambda i: (jnp.minimum(i, n - 1), 0))

    lagged = pl.BlockSpec((tm, D_MODEL), lambda i: (jnp.maximum(i - 1, 0), 0))
    sq = _resident((D_MODEL, D_MODEL))
    vec = _resident((1, D_MODEL))
    return pl.pallas_call(
        functools.partial(_out_kernel, seq // tm, n),
        grid=(n + 1,),
        in_specs=[lagged, lagged, lagged, lagged,
                  cur(2 * ML_QK_W), cur(ML_V_W), cur(ML_V_W), cur(LANES), cur(LANES),
                  _resident((MLSTM_CONV, 2 * ML_QK_W)), _resident((1, 2 * ML_QK_W)),
                  _resident((1, LANES)), _resident((1, LANES)), vec,
                  sq, sq, sq, vec, _resident((D_MODEL, D_FF)), _resident((D_MODEL, D_FF)),
                  _resident((D_FF, D_MODEL)), vec],
        out_specs=lagged,
        out_shape=jax.ShapeDtypeStruct((t, D_MODEL), F32),
        scratch_shapes=[pltpu.VMEM((2, tm, ML_V_W), BF16),
                        pltpu.VMEM((MLSTM_HEADS, MLSTM_QK_DIM, MLSTM_V_DIM), F32),
                        pltpu.VMEM((SUBLANES, LANES), F32),
                        pltpu.VMEM((SUBLANES, LANES), F32),
                        pltpu.VMEM((CONV_TAIL_ROWS + MLSTM_CHUNK, 2 * ML_QK_W), F32)],
        compiler_params=pltpu.CompilerParams(dimension_semantics=("arbitrary",),
                                             vmem_limit_bytes=VMEM_LIMIT_BYTES),
        name="mlstm_merge_ffn2",
    )(att, ga, gm, x1, qkm, vm, om, gi, gf, cw, cb, bi, bf, hn, watt, wml, wout, g2, wg, wu, wd, gfin)


def _rope_tables(seq):
    half = ATT_HEAD_DIM // 2
    pos = jnp.arange(seq, dtype=F32)
    inv_freq = ROPE_THETA ** (-jnp.arange(half, dtype=F32) / half)
    ang = pos[:, None] * inv_freq[None, :]
    lane = np.arange(LANES)
    sign = jnp.asarray(np.where(lane % ATT_HEAD_DIM < half, -1.0, 1.0), F32)
    return jnp.cos(ang)[:, lane % half], jnp.sin(ang)[:, lane % half] * sign


def _pad_lanes(v):
    return jnp.pad(v, ((0, 0), (0, LANES - v.shape[-1])))


def _layer(x2d, batch, seq, p):
    (ffn1_norm, ffn1_w_gate, ffn1_w_up, ffn1_w_down, mix_norm, w_in, b_i, b_f, attn_sinks, conv_w, conv_b,
     head_norm, w_att, w_mlstm, w_out, ffn2_norm, ffn2_w_gate, ffn2_w_up, ffn2_w_down, final_norm) = p
    tm = min(TOKEN_TILE, seq)
    assert seq % tm == 0 and seq % ATT_BLOCK == 0 and seq % MLSTM_CHUNK == 0

    cos, sin = _rope_tables(seq)
    sinks = jnp.broadcast_to(attn_sinks[:, None], (ATT_HEADS, LANES)).astype(F32)

    x1, (w_q, w_mid, w_gate, w_if), (watt, wml, wout, wg2, wu2, wd2) = _ffn1(
        x2d, ffn1_norm[None], ffn1_w_gate.astype(BF16), ffn1_w_up.astype(BF16), ffn1_w_down.astype(BF16),
        w_in, (w_att, w_mlstm, w_out, ffn2_w_gate, ffn2_w_up, ffn2_w_down),
        FFN1_TOKEN_TILE if (batch * seq) % FFN1_TOKEN_TILE == 0 else tm)
    att, qkm, vm, om, ga, gm, gi, gf = _inproj(
        x1, mix_norm[None], cos, sin, w_q, w_mid, w_gate, w_if, sinks, tm, seq)
    return _out(att, ga, gm, x1, qkm, vm, om, gi, gf, conv_w, conv_b[None], _pad_lanes(b_i[None]),
                _pad_lanes(b_f[None]), head_norm[None], watt, wml, wout, ffn2_norm[None], wg2, wu2, wd2,
                final_norm[None], tm, seq)


def kernel(x, ffn1_norm, ffn1_w_gate, ffn1_w_up, ffn1_w_down, mix_norm, w_in, b_i, b_f, attn_sinks, conv_w,
           conv_b, head_norm, w_att, w_mlstm, w_out, ffn2_norm, ffn2_w_gate, ffn2_w_up, ffn2_w_down, final_norm):
    batch, seq, d = x.shape
    assert d == D_MODEL and ffn1_norm.shape[0] == 1, "single-layer kernel"
    per_layer = (ffn1_norm, ffn1_w_gate, ffn1_w_up, ffn1_w_down, mix_norm, w_in, b_i, b_f, attn_sinks, conv_w,
                 conv_b, head_norm, w_att, w_mlstm, w_out, ffn2_norm, ffn2_w_gate, ffn2_w_up, ffn2_w_down)
    params = tuple(a[0] for a in per_layer) + (final_norm,)
    out = _layer(x.reshape(batch * seq, d), batch, seq, params)
    return out.reshape(batch, seq, d)
```

```python
import functools

import jax
import jax.numpy as jnp
import numpy as np
from jax import lax
from jax.experimental import pallas as pl
from jax.experimental.pallas import tpu as pltpu

F32 = jnp.float32
BF16 = jnp.bfloat16

D_MODEL = 1024
ATT_HEADS = 16
ATT_KV_HEADS = 4
ATT_HEAD_DIM = 64
ATT_BLOCK = 128
ROPE_THETA = 10000.0
MLSTM_HEADS = 4
MLSTM_V_DIM = D_MODEL // MLSTM_HEADS
MLSTM_QK_DIM = MLSTM_V_DIM // 2
MLSTM_CHUNK = 128
MLSTM_CONV = 4
D_FF = 2816
RMS_EPS = 1e-5
NEG_LOG2E = -1.4426950408889634

ATT_Q_W = ATT_HEADS * ATT_HEAD_DIM
ATT_KV_W = ATT_KV_HEADS * ATT_HEAD_DIM
ML_QK_W = MLSTM_HEADS * MLSTM_QK_DIM
ML_V_W = MLSTM_HEADS * MLSTM_V_DIM

LANES = 128
MXU_COLS = 256
SUBLANES = 8
BF16_SUBLANES = 16
CONV_TAIL_ROWS = SUBLANES
VMEM_LIMIT_BYTES = 60 * 1024 * 1024

TOKEN_TILE = 512
FFN1_TOKEN_TILE = 1024
FF_CHUNKS = ((0, 768), (768, 1536), (1536, 2304), (2304, D_FF))

Q_CHUNKS_PER_GROUP = ATT_HEADS // ATT_KV_HEADS * ATT_HEAD_DIM // LANES

_C_KA = 0
_C_VA = _C_KA + ATT_KV_W
_C_QKM = _C_VA + ATT_KV_W
_C_VM = _C_QKM + 2 * ML_QK_W
_C_OM = _C_VM + ML_V_W
W_MID = _C_OM + ML_V_W

_IN_OFFS = tuple(int(v) for v in np.cumsum((0, ATT_Q_W, ATT_KV_W, ATT_KV_W, ML_QK_W, ML_QK_W, ML_V_W, ML_V_W,
                                            MLSTM_HEADS, MLSTM_HEADS, D_MODEL, D_MODEL)))
IN_PROJ_W = _IN_OFFS[-1]
WT_ROWS = MXU_COLS
assert ATT_Q_W % WT_ROWS == 0 and W_MID % WT_ROWS == 0 and 2 * MLSTM_HEADS == SUBLANES
assert (2 * D_MODEL) % WT_ROWS == 0 and IN_PROJ_W % SUBLANES == 0


def _resident(shape):
    return pl.BlockSpec(shape, lambda *_: (0,) * len(shape), pipeline_mode=pl.Buffered(1))


def _col_groups(lo, hi):
    return [(c, min(c + MXU_COLS, hi)) for c in range(lo, hi, MXU_COLS)]


def _interleave(mixer, dense):
    for n_dense in mixer:
        for _ in range(n_dense):
            next(dense, None)
    for _ in dense:
        pass


def _rms(x, g):
    ms = jnp.mean(x * x, axis=-1, keepdims=True)
    return x * lax.rsqrt(ms + RMS_EPS) * g


def _sigmoid(x):
    return 1.0 / (1.0 + jnp.exp2(x * NEG_LOG2E))


def _swiglu(h, wg_ref, wu_ref, wd_ref):
    acc = None
    for lo, hi in FF_CHUNKS:
        g = jnp.dot(h, wg_ref[:, lo:hi], preferred_element_type=F32)
        u = jnp.dot(h, wu_ref[:, lo:hi], preferred_element_type=F32)
        a = (g * _sigmoid(g) * u).astype(BF16)
        d = jnp.dot(a, wd_ref[lo:hi, :], preferred_element_type=F32)
        acc = d if acc is None else acc + d
    return acc


def _ffn1_kernel(transposed, x_ref, g_ref, wg_ref, wu_ref, wd_ref, win_ref, *refs):
    if transposed:
        wtail_ref, refs = refs[0], refs[1:]
    n_plain = (len(refs) - 5) // 2
    plain_in, (o_ref, wq_ref, wmid_ref, wgate_ref, wif_ref), plain_out = (
        refs[:n_plain], refs[n_plain:n_plain + 5], refs[n_plain + 5:])
    for src, dst in zip(plain_in, plain_out):
        dst[...] = src[...].astype(BF16)
    if transposed:
        a = win_ref[...]
        at = a.T.astype(BF16)
        wq_ref[...] = at
        wmid_ref[...] = at
        wgate_ref[...] = jnp.concatenate([a[SUBLANES:], wtail_ref[...]], axis=0).T.astype(BF16)
        pad = jnp.zeros((at.shape[0], LANES - MLSTM_HEADS), BF16)
        wif_ref[...] = jnp.concatenate([at[:, :MLSTM_HEADS], pad, at[:, MLSTM_HEADS:2 * MLSTM_HEADS], pad], axis=1)
    else:
        w = win_ref[...]
        wq_ref[...] = w[:, _IN_OFFS[0]:_IN_OFFS[1]].astype(BF16)
        wmid_ref[...] = w[:, _IN_OFFS[1]:_IN_OFFS[7]].astype(BF16)
        wgate_ref[...] = w[:, _IN_OFFS[9]:].astype(BF16)
        pad = jnp.zeros((w.shape[0], LANES - MLSTM_HEADS), F32)
        wif_ref[...] = jnp.concatenate([w[:, _IN_OFFS[7]:_IN_OFFS[8]], pad, w[:, _IN_OFFS[8]:_IN_OFFS[9]], pad],
                                       axis=1).astype(BF16)

    x = x_ref[...]
    h = _rms(x, g_ref[...]).astype(BF16)
    o_ref[...] = x + 0.5 * _swiglu(h, wg_ref, wu_ref, wd_ref)


def _row_block(rows, n_steps):
    return next(r for r in range(BF16_SUBLANES, rows + 1, BF16_SUBLANES)
                if rows % r == 0 and r * n_steps >= rows)


def _ffn1(x, g, wg, wu, wd, w_in, later_weights, tm):
    t = x.shape[0]
    assert t % tm == 0
    n = t // tm
    tok = pl.BlockSpec((tm, D_MODEL), lambda i: (i, 0))

    def streamed(w):
        r = _row_block(w.shape[0], n)
        last = w.shape[0] // r - 1
        return lambda width: pl.BlockSpec((r, width), lambda i: (jnp.minimum(i, last), 0))

    plain = [streamed(w) for w in later_weights]
    in_widths = (ATT_Q_W, W_MID, 2 * D_MODEL, 2 * LANES)
    q_blks, mid_blks, gate_blks = ATT_Q_W // WT_ROWS, W_MID // WT_ROWS, 2 * D_MODEL // WT_ROWS
    if_blk = q_blks + mid_blks
    transposed = n >= if_blk + gate_blks
    if transposed:
        tail_per_blk, tail_last = WT_ROWS // SUBLANES, IN_PROJ_W // SUBLANES - 1
        win_arg = (w_in.T, w_in.T)
        win_specs = [
            pl.BlockSpec((WT_ROWS, D_MODEL), lambda i: (jnp.minimum(i, if_blk + gate_blks - 1), 0)),
            pl.BlockSpec((SUBLANES, D_MODEL), lambda i: (jnp.minimum((i + 1) * tail_per_blk, tail_last), 0))]
        col_blk = lambda f: pl.BlockSpec((D_MODEL, WT_ROWS), lambda i: (0, f(i)))
        win_out_specs = [col_blk(lambda i: jnp.minimum(i, q_blks)),
                         col_blk(lambda i: jnp.clip(i - q_blks, 0, mid_blks)),
                         col_blk(lambda i: jnp.clip(i - if_blk, 0, gate_blks)),
                         col_blk(lambda i: jnp.where(i == if_blk, 0, jnp.where(i < if_blk, 1, 2)))]
        out_widths = [w_ + WT_ROWS for w_ in in_widths[:3]] + [in_widths[3] + 2 * WT_ROWS]
    else:
        in_spec = streamed(w_in)
        win_arg, win_specs = (w_in,), [in_spec(IN_PROJ_W)]
        win_out_specs, out_widths = [in_spec(w_) for w_ in in_widths], in_widths
    out = pl.pallas_call(
        functools.partial(_ffn1_kernel, transposed),
        grid=(n,),
        in_specs=[tok, _resident((1, D_MODEL)), _resident((D_MODEL, D_FF)), _resident((D_MODEL, D_FF)),
                  _resident((D_FF, D_MODEL))] + win_specs
        + [spec(w.shape[1]) for spec, w in zip(plain, later_weights)],
        out_specs=[tok] + win_out_specs
        + [spec(w.shape[1]) for spec, w in zip(plain, later_weights)],
        out_shape=[jax.ShapeDtypeStruct((t, D_MODEL), F32)]
        + [jax.ShapeDtypeStruct((D_MODEL, w_), BF16) for w_ in out_widths]
        + [jax.ShapeDtypeStruct(w.shape, BF16) for w in later_weights],
        compiler_params=pltpu.CompilerParams(dimension_semantics=("arbitrary",),
                                             vmem_limit_bytes=VMEM_LIMIT_BYTES),
        name="ffn1",
    )(x, g, wg, wu, wd, *win_arg, *later_weights)
    return out[0], out[1:5], out[5:]


def _rope(x, cos, sin_signed):
    lane = lax.broadcasted_iota(jnp.int32, x.shape, 1)
    half = ATT_HEAD_DIM // 2
    partner = jnp.where(lane % ATT_HEAD_DIM < half, pltpu.roll(x, LANES - half, axis=1),
                        pltpu.roll(x, half, axis=1))
    return x * cos + partner * sin_signed


def _attn_scores(g, q_blk, k_cur, k_prev):
    low = lax.broadcasted_iota(jnp.int32, (ATT_BLOCK, LANES), 1) < ATT_HEAD_DIM
    c, in_high = divmod(g, 2)
    ksl = slice(c * LANES, (c + 1) * LANES)
    parts = []
    for k in (k_cur, k_prev):
        x = k[:, ksl].astype(F32)
        other = pltpu.roll(x, ATT_HEAD_DIM, axis=1)
        in_low_lanes, in_high_lanes = (other, x) if in_high else (x, other)
        parts.append((jnp.where(low, in_low_lanes, 0.0), jnp.where(low, 0.0, in_high_lanes)))
    kstack = jnp.concatenate([parts[0][0], parts[1][0], parts[0][1], parts[1][1]], axis=0).astype(BF16)
    qstack = jnp.concatenate(
        [q_blk[:, (g * Q_CHUNKS_PER_GROUP + r) * LANES:(g * Q_CHUNKS_PER_GROUP + r + 1) * LANES]
         for r in range(Q_CHUNKS_PER_GROUP)], axis=0)
    return lax.dot_general(kstack, qstack, (((1,), (1,)), ((), ())), preferred_element_type=F32)


def _attn_block(q_blk, kv_cur, kv_prev, has_prev, sink_ref, pt_scr, o_ref, rows):
    blk = ATT_BLOCK
    key = lax.broadcasted_iota(jnp.int32, (blk, blk), 0)
    qry = lax.broadcasted_iota(jnp.int32, (blk, blk), 1)
    in_cur = key <= qry
    neg_inf = jnp.float32(-jnp.inf)
    st = [_attn_scores(g, q_blk, kv_cur[:, :ATT_KV_W], kv_prev[:, :ATT_KV_W]) for g in range(ATT_KV_HEADS)]
    yield 1
    v_cat = jnp.concatenate([kv_cur[:, ATT_KV_W:], kv_prev[:, ATT_KV_W:]], axis=0).astype(F32)
    vt_all = v_cat.T.astype(BF16)
    for g in range(ATT_KV_HEADS):
        for r in range(Q_CHUNKS_PER_GROUP):
            qs = slice(r * blk, (r + 1) * blk)
            for half in range(2):
                head = (g * Q_CHUNKS_PER_GROUP + r) * 2 + half
                s_cur = st[g][(2 * half) * blk:(2 * half + 1) * blk, qs]
                s_prev = st[g][(2 * half + 1) * blk:(2 * half + 2) * blk, qs]
                s = jnp.where(in_cur, s_cur, jnp.where(has_prev, s_prev, neg_inf))
                sink = sink_ref[head:head + 1, :]
                m = jnp.maximum(jnp.max(s, axis=0, keepdims=True), sink)
                p = jnp.exp(s - m)
                denom = jnp.sum(p, axis=0, keepdims=True) + jnp.exp(sink - m)
                p = p * (1.0 / denom)
                zp = jnp.zeros_like(p)
                pt_scr[g * 2 + half, :blk, qs] = jnp.where(in_cur, p, zp).astype(BF16)
                pt_scr[g * 2 + half, blk:, qs] = jnp.where(in_cur, zp, p).astype(BF16)
        vt = vt_all[g * ATT_HEAD_DIM:(g + 1) * ATT_HEAD_DIM, :]
        ot = [jnp.dot(vt, pt_scr[g * 2 + half], preferred_element_type=F32) for half in range(2)]
        for r in range(Q_CHUNKS_PER_GROUP):
            j = g * Q_CHUNKS_PER_GROUP + r
            qs = slice(r * blk, (r + 1) * blk)
            o_t = jnp.concatenate([ot[0][:, qs], ot[1][:, qs]], axis=0)
            o_ref[rows, j * LANES:(j + 1) * LANES] = o_t.T.astype(BF16)
    yield 0


def _inproj_kernel(blocks_per_seq, x_ref, g_ref, cos_ref, sin_ref, wq_ref, wmid_ref, wgate_ref, wif_ref, sink_ref,
                   att_ref, qkm_ref, vm_ref, om_ref, ga_ref, gm_ref, gi_ref, gf_ref, q_scr, kv_scr, pt_scr):
    tm = x_ref.shape[0]
    blk = ATT_BLOCK
    step = pl.program_id(0)
    slot = step % 2
    q_new, kv_new = q_scr.at[slot], kv_scr.at[slot]
    q_old, kv_old = q_scr.at[1 - slot], kv_scr.at[1 - slot]

    @pl.when(step == 0)
    def _():
        q_scr[...] = jnp.zeros_like(q_scr)
        kv_scr[...] = jnp.zeros_like(kv_scr)

    h = _rms(x_ref[...], g_ref[...]).astype(BF16)
    kv_new[:blk, :] = kv_old[tm:, :]

    def mixer():
        for k in range(tm // blk):
            rows = slice(k * blk, (k + 1) * blk)
            has_prev = ((step - 1) * (tm // blk) + k) % blocks_per_seq != 0
            yield from _attn_block(q_old[rows, :], kv_old[blk + k * blk:2 * blk + k * blk, :],
                                   kv_old[k * blk:(k + 1) * blk, :], has_prev, sink_ref, pt_scr, att_ref, rows)

    def dense():
        def proj(w_ref, lo, width):
            return jnp.dot(h, w_ref[:, lo:lo + width], preferred_element_type=F32)

        z = proj(wq_ref, 0, ATT_Q_W) * (ATT_HEAD_DIM ** -0.5)
        cos, sin = cos_ref[...], sin_ref[...]
        for j in range(ATT_Q_W // LANES):
            sl = slice(j * LANES, (j + 1) * LANES)
            q_new[:, sl] = _rope(z[:, sl], cos, sin).astype(BF16)
        yield
        z = proj(wmid_ref, _C_KA, 2 * ATT_KV_W)
        for j in range(ATT_KV_W // LANES):
            sl = slice(j * LANES, (j + 1) * LANES)
            kv_new[blk:, sl] = _rope(z[:, sl], cos, sin).astype(BF16)
        kv_new[blk:, ATT_KV_W:] = z[:, ATT_KV_W:].astype(BF16)
        vm_ref[...] = proj(wmid_ref, _C_VM, ML_V_W).astype(BF16)
        yield
        om_ref[...] = proj(wmid_ref, _C_OM, ML_V_W).astype(BF16)
        ga_ref[...] = proj(wgate_ref, 0, D_MODEL).astype(BF16)
        yield
        gm_ref[...] = proj(wgate_ref, D_MODEL, D_MODEL).astype(BF16)
        z = proj(wif_ref, 0, 2 * LANES)
        gi_ref[...] = z[:, :LANES]
        gf_ref[...] = z[:, LANES:]
        qkm_ref[...] = proj(wmid_ref, _C_QKM, 2 * ML_QK_W).astype(BF16)
        yield

    _interleave(mixer(), dense())


def _inproj(x1, g, cos, sin, wq, wmid, wgate, wif, sinks, tm, seq):
    t = x1.shape[0]
    n = t // tm
    tiles_per_seq = seq // tm

    def cur(i):
        return jnp.minimum(i, n - 1)

    def tok(width):
        return pl.BlockSpec((tm, width), lambda i: (cur(i), 0))

    pos = pl.BlockSpec((tm, LANES), lambda i: (cur(i) % tiles_per_seq, 0))
    lagged = pl.BlockSpec((tm, ATT_Q_W), lambda i: (jnp.maximum(i - 1, 0), 0))
    widths = (2 * ML_QK_W, ML_V_W, ML_V_W, D_MODEL, D_MODEL)
    out_shape = [jax.ShapeDtypeStruct((t, ATT_Q_W), BF16)]
    out_shape += [jax.ShapeDtypeStruct((t, w_), BF16) for w_ in widths]
    out_shape += [jax.ShapeDtypeStruct((t, LANES), F32)] * 2
    return pl.pallas_call(
        functools.partial(_inproj_kernel, seq // ATT_BLOCK),
        grid=(n + 1,),
        in_specs=[tok(D_MODEL), _resident((1, D_MODEL)), pos, pos, _resident((D_MODEL, ATT_Q_W)),
                  _resident((D_MODEL, W_MID)), _resident((D_MODEL, 2 * D_MODEL)), _resident((D_MODEL, 2 * LANES)),
                  _resident((ATT_HEADS, LANES))],
        out_specs=[lagged] + [tok(w_) for w_ in widths] + [tok(LANES)] * 2,
        out_shape=out_shape,
        scratch_shapes=[pltpu.VMEM((2, tm, ATT_Q_W), BF16),
                        pltpu.VMEM((2, ATT_BLOCK + tm, 2 * ATT_KV_W), BF16),
                        pltpu.VMEM((2 * ATT_KV_HEADS, 2 * ATT_BLOCK, Q_CHUNKS_PER_GROUP * ATT_BLOCK), BF16)],
        compiler_params=pltpu.CompilerParams(dimension_semantics=("arbitrary",),
                                             vmem_limit_bytes=VMEM_LIMIT_BYTES),
        name="in_proj_attn",
    )(x1, g, cos, sin, wq, wmid, wgate, wif, sinks)


def _scan_rows(x, op, fill):
    n = x.shape[0]
    row = lax.broadcasted_iota(jnp.int32, x.shape, 0)
    shift = 1
    while shift < n:
        prev = jnp.where(row >= shift, pltpu.roll(x, shift, axis=0), fill)
        x = op(x, prev)
        shift *= 2
    return x


def _mlstm_chunk_inputs(rows, qk_ref, gi_ref, gf_ref, cw_ref, cb_ref, bi_ref, bf_ref, conv_scr):
    L = MLSTM_CHUNK
    tail = CONV_TAIL_ROWS
    x = qk_ref[rows, :].astype(F32)
    conv_scr[tail:, :] = x
    y = cb_ref[...] + cw_ref[MLSTM_CONV - 1:MLSTM_CONV, :] * x
    for j in range(1, MLSTM_CONV):
        y = y + cw_ref[MLSTM_CONV - 1 - j:MLSTM_CONV - j, :] * conv_scr[tail - j:tail - j + L, :]
    conv_scr[:tail, :] = x[L - tail:, :]
    qk = y * _sigmoid(y)

    ig = gi_ref[rows, :] + bi_ref[...]
    fpre = (gf_ref[rows, :] + bf_ref[...]).T[:SUBLANES, :]
    logf = jnp.minimum(fpre, 0.0) - jnp.log1p(jnp.exp(-jnp.abs(fpre)))
    logf = jnp.concatenate([logf, jnp.zeros((LANES - SUBLANES, L), F32)], axis=0).T
    b = _scan_rows(logf, jnp.add, 0.0)
    g = ig - b
    cm = _scan_rows(g, jnp.maximum, -jnp.inf)
    return qk, b, g, cm


def _mlstm_gates(b, g, cm, m_scr):
    L = MLSTM_CHUNK
    m_prev = m_scr[0:1, :]
    u = jnp.maximum(cm, m_prev)
    g_max = cm[L - 1:L, :]
    u_last = u[L - 1:L, :]
    gates = dict(u=u, a=jnp.exp(g - g_max), inter=jnp.exp(m_prev - u), emt=jnp.exp(-(b + u)),
                 s_old=jnp.exp(m_prev - u_last), s_new=jnp.exp(g_max - u_last),
                 g_rows=g.T)
    m_scr[0:1, :] = b[L - 1:L, :] + u_last
    return gates


def _mlstm_head(h, rows, qk, gates, v_ref, o_ref, hn_ref, out_ref, ct_scr, n_scr):
    L = MLSTM_CHUNK
    row = lax.broadcasted_iota(jnp.int32, (L, L), 0)
    col = lax.broadcasted_iota(jnp.int32, (L, L), 1)
    causal = col <= row
    qs = slice(h * MLSTM_QK_DIM, (h + 1) * MLSTM_QK_DIM)
    ks = slice(ML_QK_W + h * MLSTM_QK_DIM, ML_QK_W + (h + 1) * MLSTM_QK_DIM)
    vs = slice(h * MLSTM_V_DIM, (h + 1) * MLSTM_V_DIM)
    qh = qk[:, qs]
    kh = qk[:, ks] * (MLSTM_QK_DIM ** -0.5)
    qb = qh.astype(BF16)
    kb = kh.astype(BF16)
    vb = v_ref[rows, vs]
    inter_col = gates["inter"][:, h:h + 1]
    a_col = gates["a"][:, h:h + 1]

    decay = jnp.where(causal, jnp.exp(gates["g_rows"][h:h + 1, :] - gates["u"][:, h:h + 1]), 0.0)
    s = lax.dot_general(qb, kb, (((1,), (1,)), ((), ())), preferred_element_type=F32) * decay
    ct = ct_scr[h]
    n_row = n_scr[h:h + 1, :]
    yield 1
    num = (jnp.dot(s.astype(BF16), vb, preferred_element_type=F32)
           + inter_col * jnp.dot(qb, ct.astype(BF16), preferred_element_type=F32))
    den = jnp.sum(s + inter_col * (qh * n_row), axis=-1, keepdims=True)
    hh = num * (1.0 / jnp.maximum(jnp.abs(den), gates["emt"][:, h:h + 1]))
    hn = _rms(hh, hn_ref[:, vs])
    out_ref[rows, vs] = (_sigmoid(o_ref[rows, vs].astype(F32)) * hn).astype(BF16)
    yield 1
    av = (a_col * vb.astype(F32)).astype(BF16)
    d_ct = lax.dot_general(kb, av, (((0,), (0,)), ((), ())), preferred_element_type=F32)
    d_n = jnp.sum(a_col * kh, axis=0, keepdims=True)
    so = gates["s_old"][:, h:h + 1]
    sn = gates["s_new"][:, h:h + 1]
    ct_scr[h] = so * ct + sn * d_ct
    n_scr[h:h + 1, :] = so * n_row + sn * d_n
    yield 1


def _out_kernel(tiles_per_seq, n_tiles, att_ref, ga_ref, gm_ref, x1_ref, qk_ref, v_ref, o_ref, gi_ref, gf_ref,
                cw_ref, cb_ref, bi_ref, bf_ref, hn_ref, watt_ref, wml_ref, wout_ref, g2_ref, wg_ref, wu_ref,
                wd_ref, gfin_ref, out_ref, hm_scr, ct_scr, n_scr, m_scr, conv_scr):
    tm = x1_ref.shape[0]
    L = MLSTM_CHUNK
    step = pl.program_id(0)
    slot = step % 2
    hm_new, hm_old = hm_scr.at[slot], hm_scr.at[1 - slot]

    @pl.when(step == 0)
    def _():
        hm_scr[...] = jnp.zeros_like(hm_scr)

    @pl.when(jnp.minimum(step, n_tiles - 1) % tiles_per_seq == 0)
    def _():
        ct_scr[...] = jnp.zeros_like(ct_scr)
        n_scr[...] = jnp.zeros_like(n_scr)
        m_scr[...] = jnp.zeros_like(m_scr)
        conv_scr[:CONV_TAIL_ROWS, :] = jnp.zeros((CONV_TAIL_ROWS, conv_scr.shape[1]), F32)

    def chunk_inputs(c):
        return _mlstm_chunk_inputs(slice(c * L, (c + 1) * L), qk_ref, gi_ref, gf_ref, cw_ref, cb_ref, bi_ref,
                                   bf_ref, conv_scr)

    def mixer():
        for c in range(tm // L):
            rows = slice(c * L, (c + 1) * L)
            qk, b, g, cm = chunk_inputs(c)
            gates = _mlstm_gates(b, g, cm, m_scr)
            yield 1
            for h in range(MLSTM_HEADS):
                yield from _mlstm_head(h, rows, qk, gates, v_ref, o_ref, hn_ref, hm_new, ct_scr, n_scr)

    def dense():
        def dot(x, w):
            return jnp.dot(x, w, preferred_element_type=F32)

        cols = _col_groups(0, D_MODEL)
        ya, ym, x2p = [], [], []
        for lo, hi in cols:
            ya.append(dot(att_ref[...], watt_ref[:, lo:hi]))
            yield
        for lo, hi in cols:
            ym.append(dot(hm_old[...], wml_ref[:, lo:hi]))
            yield
        y = jnp.concatenate(
            [_sigmoid(ga_ref[:, lo:hi].astype(F32)) * ya[j] + _sigmoid(gm_ref[:, lo:hi].astype(F32)) * ym[j]
             for j, (lo, hi) in enumerate(cols)], axis=1).astype(BF16)
        for lo, hi in cols:
            x2p.append(dot(y, wout_ref[:, lo:hi]))
            yield
        x2 = x1_ref[...] + jnp.concatenate(x2p, axis=1)
        h2 = _rms(x2, g2_ref[...]).astype(BF16)
        acc = None
        for clo, chi in FF_CHUNKS:
            groups = _col_groups(clo, chi)
            g = []
            for lo, hi in groups:
                g.append(dot(h2, wg_ref[:, lo:hi]))
                yield
            a = []
            for j, (lo, hi) in enumerate(groups):
                u = dot(h2, wu_ref[:, lo:hi])
                a.append((g[j] * _sigmoid(g[j]) * u).astype(BF16))
                yield
            a = jnp.concatenate(a, axis=1)
            part = []
            for lo, hi in cols:
                part.append(dot(a, wd_ref[clo:chi, lo:hi]))
                yield
            acc = part if acc is None else [p + q for p, q in zip(acc, part)]
        x3 = x2 + 0.5 * jnp.concatenate(acc, axis=1)
        out_ref[...] = _rms(x3, gfin_ref[...])
        yield

    _interleave(mixer(), dense())


def _out(att, ga, gm, x1, qkm, vm, om, gi, gf, cw, cb, bi, bf, hn, watt, wml, wout, g2, wg, wu, wd, gfin, tm, seq):
    t = x1.shape[0]
    n = t // tm

    def cur(width):
        return pl.BlockSpec((tm, width), lambda i: (jnp.minimum(i, n - 1), 0))

    lagged = pl.BlockSpec((tm, D_MODEL), lambda i: (jnp.maximum(i - 1, 0), 0))
    sq = _resident((D_MODEL, D_MODEL))
    vec = _resident((1, D_MODEL))
    return pl.pallas_call(
        functools.partial(_out_kernel, seq // tm, n),
        grid=(n + 1,),
        in_specs=[lagged, lagged, lagged, lagged,
                  cur(2 * ML_QK_W), cur(ML_V_W), cur(ML_V_W), cur(LANES), cur(LANES),
                  _resident((MLSTM_CONV, 2 * ML_QK_W)), _resident((1, 2 * ML_QK_W)),
                  _resident((1, LANES)), _resident((1, LANES)), vec,
                  sq, sq, sq, vec, _resident((D_MODEL, D_FF)), _resident((D_MODEL, D_FF)),
                  _resident((D_FF, D_MODEL)), vec],
        out_specs=lagged,
        out_shape=jax.ShapeDtypeStruct((t, D_MODEL), F32),
        scratch_shapes=[pltpu.VMEM((2, tm, ML_V_W), BF16),
                        pltpu.VMEM((MLSTM_HEADS, MLSTM_QK_DIM, MLSTM_V_DIM), F32),
                        pltpu.VMEM((SUBLANES, LANES), F32),
                        pltpu.VMEM((SUBLANES, LANES), F32),
                        pltpu.VMEM((CONV_TAIL_ROWS + MLSTM_CHUNK, 2 * ML_QK_W), F32)],
        compiler_params=pltpu.CompilerParams(dimension_semantics=("arbitrary",),
                                             vmem_limit_bytes=VMEM_LIMIT_BYTES),
        name="mlstm_merge_ffn2",
    )(att, ga, gm, x1, qkm, vm, om, gi, gf, cw, cb, bi, bf, hn, watt, wml, wout, g2, wg, wu, wd, gfin)


def _rope_tables(seq):
    half = ATT_HEAD_DIM // 2
    pos = jnp.arange(seq, dtype=F32)
    inv_freq = ROPE_THETA ** (-jnp.arange(half, dtype=F32) / half)
    ang = pos[:, None] * inv_freq[None, :]
    s = jnp.sin(ang)
    return (jnp.tile(jnp.cos(ang), (1, LANES // half)),
            jnp.tile(jnp.concatenate([-s, s], axis=1), (1, LANES // ATT_HEAD_DIM)))


def _pad_lanes(v):
    return jnp.pad(v, ((0, 0), (0, LANES - v.shape[-1])))


def _layer(x2d, batch, seq, p):
    (ffn1_norm, ffn1_w_gate, ffn1_w_up, ffn1_w_down, mix_norm, w_in, b_i, b_f, attn_sinks, conv_w, conv_b,
     head_norm, w_att, w_mlstm, w_out, ffn2_norm, ffn2_w_gate, ffn2_w_up, ffn2_w_down, final_norm) = p
    tm = min(TOKEN_TILE, seq)
    assert seq % tm == 0 and seq % ATT_BLOCK == 0 and seq % MLSTM_CHUNK == 0

    cos, sin = _rope_tables(seq)
    sinks = jnp.broadcast_to(attn_sinks[:, None], (ATT_HEADS, LANES)).astype(F32)

    x1, (w_q, w_mid, w_gate, w_if), (watt, wml, wout, wg2, wu2, wd2) = _ffn1(
        x2d, ffn1_norm[None], ffn1_w_gate.astype(BF16), ffn1_w_up.astype(BF16), ffn1_w_down.astype(BF16),
        w_in, (w_att, w_mlstm, w_out, ffn2_w_gate, ffn2_w_up, ffn2_w_down),
        FFN1_TOKEN_TILE if (batch * seq) % FFN1_TOKEN_TILE == 0 else tm)
    att, qkm, vm, om, ga, gm, gi, gf = _inproj(
        x1, mix_norm[None], cos, sin, w_q, w_mid, w_gate, w_if, sinks, tm, seq)
    return _out(att, ga, gm, x1, qkm, vm, om, gi, gf, conv_w, conv_b[None], _pad_lanes(b_i[None]),
                _pad_lanes(b_f[None]), head_norm[None], watt, wml, wout, ffn2_norm[None], wg2, wu2, wd2,
                final_norm[None], tm, seq)


def kernel(x, ffn1_norm, ffn1_w_gate, ffn1_w_up, ffn1_w_down, mix_norm, w_in, b_i, b_f, attn_sinks, conv_w,
           conv_b, head_norm, w_att, w_mlstm, w_out, ffn2_norm, ffn2_w_gate, ffn2_w_up, ffn2_w_down, final_norm):
    batch, seq, d = x.shape
    assert d == D_MODEL and ffn1_norm.shape[0] == 1, "single-layer kernel"
    per_layer = (ffn1_norm, ffn1_w_gate, ffn1_w_up, ffn1_w_down, mix_norm, w_in, b_i, b_f, attn_sinks, conv_w,
                 conv_b, head_norm, w_att, w_mlstm, w_out, ffn2_norm, ffn2_w_gate, ffn2_w_up, ffn2_w_down)
    params = tuple(a[0] for a in per_layer) + (final_norm,)
    out = _layer(x.reshape(batch * seq, d), batch, seq, params)
    return out.reshape(batch, seq, d)
```
